```python
import math
import jax, jax.numpy as jnp
from jax import lax
import numpy as np


D_MODEL = 1024
BATCH = 1
SEQ = 16384
DEPTH = 2
DEC_BATCH = 32
DEC_SEQ = 16
PAST_LEN = 2048

CHUNK = 64
MIX_WIDTH = D_MODEL
SSM_WIDTH = MIX_WIDTH // 2
CONV_CH = MIX_WIDTH - SSM_WIDTH
SSM_GROUP_CH = 16
SSM_GROUPS = SSM_WIDTH // SSM_GROUP_CH
SSM_STATE = 64
CONV_K = 3
IN_PROJ_WIDTH = SSM_WIDTH + 3 * CONV_CH
N_EXPERT_GROUPS = 4
EXPERTS_PER_GROUP = 8
N_EXPERTS = N_EXPERT_GROUPS * EXPERTS_PER_GROUP
TOP_K = 2
D_FF_EXPERT = 512
EPS = 1e-6
DT_MIN = 1e-3
DT_MAX = 1e-1

kernel_name = 'hymba_s5_shortconv_hmoe_step'


def _rmsnorm(x, g):
    xf = x.astype(jnp.float32)
    y = xf * lax.rsqrt(jnp.mean(xf * xf, axis=-1, keepdims=True) + EPS) * g.astype(jnp.float32)
    return y.astype(x.dtype)


def _discretise(lam_re, lam_im, log_dt, b_re, b_im):
    lam_re = lam_re.astype(jnp.float32)
    lam_im = lam_im.astype(jnp.float32)
    dt = jnp.exp(log_dt.astype(jnp.float32))[:, None]
    mag = jnp.exp(lam_re * dt)
    ab_re = mag * jnp.cos(lam_im * dt)
    ab_im = mag * jnp.sin(lam_im * dt)
    denom = lam_re * lam_re + lam_im * lam_im
    num_re = ab_re - 1.0
    f_re = (num_re * lam_re + ab_im * lam_im) / denom
    f_im = (ab_im * lam_re - num_re * lam_im) / denom
    b_re = b_re.astype(jnp.float32)
    b_im = b_im.astype(jnp.float32)
    bb_re = f_re[..., None] * b_re - f_im[..., None] * b_im
    bb_im = f_re[..., None] * b_im + f_im[..., None] * b_re
    return ab_re, ab_im, bb_re, bb_im


def _scan_combine(e1, e2):
    a1r, a1i, b1r, b1i = e1
    a2r, a2i, b2r, b2i = e2
    ar = a2r * a1r - a2i * a1i
    ai = a2r * a1i + a2i * a1r
    br = a2r * b1r - a2i * b1i + b2r
    bi = a2r * b1i + a2i * b1r + b2i
    return (ar, ai, br, bi)


def _ssm_mixer(u, h0_re, h0_im, lam_re, lam_im, log_dt, b_re, b_im, c_re, c_im, d_skip, w_glu, b_glu):
    bsz, s, _ = u.shape
    ab_re, ab_im, bb_re, bb_im = _discretise(lam_re, lam_im, log_dt, b_re, b_im)
    ug = u.astype(jnp.float32).reshape(bsz, s, SSM_GROUPS, SSM_GROUP_CH)
    bu_re = jnp.einsum('bsgh,gph->bsgp', ug, bb_re)
    bu_im = jnp.einsum('bsgh,gph->bsgp', ug, bb_im)
    h0_re = h0_re.astype(jnp.float32)
    h0_im = h0_im.astype(jnp.float32)
    bu_re = bu_re.at[:, 0].add(ab_re * h0_re - ab_im * h0_im)
    bu_im = bu_im.at[:, 0].add(ab_re * h0_im + ab_im * h0_re)
    a_re = jnp.broadcast_to(ab_re, bu_re.shape)
    a_im = jnp.broadcast_to(ab_im, bu_im.shape)
    _, _, hr, hi = lax.associative_scan(_scan_combine, (a_re, a_im, bu_re, bu_im), axis=1)
    y = (jnp.einsum('bsgp,ghp->bsgh', hr, c_re.astype(jnp.float32))
         - jnp.einsum('bsgp,ghp->bsgh', hi, c_im.astype(jnp.float32))
         + d_skip.astype(jnp.float32).reshape(SSM_GROUPS, SSM_GROUP_CH) * ug)
    y = y.reshape(bsz, s, SSM_WIDTH).astype(u.dtype)
    z = jax.nn.gelu(y)
    out = z * jax.nn.sigmoid(jnp.einsum('bsc,ce->bse', z, w_glu) + b_glu)
    return out, hr[:, -1], hi[:, -1]


def _conv_mixer(gb, gc, v, buf, w_conv):
    s = v.shape[1]
    z = gc * v
    zp = jnp.concatenate([buf.astype(z.dtype), z], axis=1)
    c = w_conv[0] * zp[:, 0:s]
    for k in range(1, CONV_K):
        c = c + w_conv[k] * zp[:, k:k + s]
    return gb * c, zp[:, -(CONV_K - 1):]


def _hier_moe(h, w_rg, b_rg, w_re, b_re, w_gate, w_up, w_down):
    bsz, s, d = h.shape
    hf = h.reshape(bsz * s, d)
    coarse = jnp.einsum('td,gk->tk', hf, w_rg).astype(jnp.float32) if False else (hf @ w_rg).astype(jnp.float32) + b_rg.astype(jnp.float32)
    p_group = jax.nn.softmax(coarse, axis=-1)
    g = jnp.argmax(coarse, axis=-1)
    fine_all = jnp.einsum('td,gde->tge', hf, w_re).astype(jnp.float32) + b_re.astype(jnp.float32)
    fine = jnp.take_along_axis(fine_all, g[:, None, None], axis=1)[:, 0]
    top_val, top_idx = lax.top_k(fine, TOP_K)
    gate = jax.nn.softmax(top_val, axis=-1) * jnp.take_along_axis(p_group, g[:, None], axis=1)
    eid = g[:, None] * EXPERTS_PER_GROUP + top_idx
    combine = jnp.sum(jax.nn.one_hot(eid, N_EXPERTS, dtype=jnp.float32) * gate[..., None], axis=1)
    out = jnp.zeros((bsz * s, d), jnp.float32)
    for e in range(N_EXPERTS):
        hid = jax.nn.silu(hf @ w_gate[e]) * (hf @ w_up[e])
        out = out + combine[:, e:e + 1] * (hid @ w_down[e]).astype(jnp.float32)
    return out.astype(h.dtype).reshape(bsz, s, d)


def _trunk(x, st_re, st_im, cbuf, norm_mix, w_in, lam_re, lam_im, log_dt, ssm_b_re, ssm_b_im,
           ssm_c_re, ssm_c_im, ssm_d, w_glu, b_glu, w_conv, norm_a, norm_b, w_out, norm_ffn,
           w_router_group, b_router_group, w_router_expert, b_router_expert, w_gate, w_up, w_down,
           norm_final):
    new_re, new_im, new_buf = [], [], []
    for l in range(DEPTH):
        h = _rmsnorm(x, norm_mix[l])
        proj = jnp.einsum('bsd,de->bse', h, w_in[l])
        u, gb, gc, v = jnp.split(proj, [SSM_WIDTH, SSM_WIDTH + CONV_CH, SSM_WIDTH + 2 * CONV_CH], axis=-1)
        ya, hr, hi = _ssm_mixer(u, st_re[l], st_im[l], lam_re[l], lam_im[l], log_dt[l], ssm_b_re[l],
                                ssm_b_im[l], ssm_c_re[l], ssm_c_im[l], ssm_d[l], w_glu[l], b_glu[l])
        yb, nb = _conv_mixer(gb, gc, v, cbuf[l], w_conv[l])
        mix = jnp.concatenate([_rmsnorm(ya, norm_a[l]), _rmsnorm(yb, norm_b[l])], axis=-1)
        x = x + jnp.einsum('bsc,cd->bsd', mix, w_out[l])
        x = x + _hier_moe(_rmsnorm(x, norm_ffn[l]), w_router_group[l], b_router_group[l],
                          w_router_expert[l], b_router_expert[l], w_gate[l], w_up[l], w_down[l])
        new_re.append(hr)
        new_im.append(hi)
        new_buf.append(nb)
    return _rmsnorm(x, norm_final), jnp.stack(new_re), jnp.stack(new_im), jnp.stack(new_buf)


def setup_inputs(seed: int = 0) -> dict:
    key = jax.random.key(seed)
    ks = jax.random.split(key, 32)
    f32 = jnp.float32

    def nrm(k, shape, scale):
        return scale * jax.random.normal(k, shape, f32)

    n_idx = jnp.arange(SSM_STATE, dtype=f32)
    G, P, H = SSM_GROUPS, SSM_STATE, SSM_GROUP_CH
    return {
        'x_prompt': nrm(ks[0], (BATCH, SEQ, D_MODEL), 1.0),
        'x_sample': nrm(ks[1], (DEC_BATCH, DEC_SEQ, D_MODEL), 1.0),
        'state_ssm_re': nrm(ks[2], (DEPTH, DEC_BATCH, G, P), 0.5),
        'state_ssm_im': nrm(ks[3], (DEPTH, DEC_BATCH, G, P), 0.5),
        'cache_conv': nrm(ks[4], (DEPTH, DEC_BATCH, CONV_K - 1, CONV_CH), 1.0),
        'norm_mix': 1.0 + nrm(ks[5], (DEPTH, D_MODEL), 0.02),
        'w_in': nrm(ks[6], (DEPTH, D_MODEL, IN_PROJ_WIDTH), D_MODEL ** -0.5),
        'lam_re': -0.5 + nrm(ks[7], (DEPTH, G, P), 0.01),
        'lam_im': math.pi * n_idx + nrm(ks[8], (DEPTH, G, P), 0.01),
        'log_dt': jax.random.uniform(ks[9], (DEPTH, G), f32, math.log(DT_MIN), math.log(DT_MAX)),
        'ssm_b_re': nrm(ks[10], (DEPTH, G, P, H), (2 * H) ** -0.5),
        'ssm_b_im': nrm(ks[11], (DEPTH, G, P, H), (2 * H) ** -0.5),
        'ssm_c_re': nrm(ks[12], (DEPTH, G, H, P), (2 * P) ** -0.5),
        'ssm_c_im': nrm(ks[13], (DEPTH, G, H, P), (2 * P) ** -0.5),
        'ssm_d': nrm(ks[14], (DEPTH, SSM_WIDTH), 1.0),
        'w_glu': nrm(ks[15], (DEPTH, SSM_WIDTH, SSM_WIDTH), SSM_WIDTH ** -0.5),
        'b_glu': nrm(ks[16], (DEPTH, SSM_WIDTH), 0.01),
        'w_conv': nrm(ks[17], (DEPTH, CONV_K, CONV_CH), CONV_K ** -0.5),
        'norm_a': 1.0 + nrm(ks[18], (DEPTH, SSM_WIDTH), 0.02),
        'norm_b': 1.0 + nrm(ks[19], (DEPTH, CONV_CH), 0.02),
        'w_out': nrm(ks[20], (DEPTH, MIX_WIDTH, D_MODEL), MIX_WIDTH ** -0.5),
        'norm_ffn': 1.0 + nrm(ks[21], (DEPTH, D_MODEL), 0.02),
        'w_router_group': nrm(ks[22], (DEPTH, D_MODEL, N_EXPERT_GROUPS), D_MODEL ** -0.5),
        'b_router_group': nrm(ks[23], (DEPTH, N_EXPERT_GROUPS), 0.01),
        'w_router_expert': nrm(ks[24], (DEPTH, N_EXPERT_GROUPS, D_MODEL, EXPERTS_PER_GROUP), D_MODEL ** -0.5),
        'b_router_expert': nrm(ks[25], (DEPTH, N_EXPERT_GROUPS, EXPERTS_PER_GROUP), 0.01),
        'w_gate': nrm(ks[26], (DEPTH, N_EXPERTS, D_MODEL, D_FF_EXPERT), D_MODEL ** -0.5),
        'w_up': nrm(ks[27], (DEPTH, N_EXPERTS, D_MODEL, D_FF_EXPERT), D_MODEL ** -0.5),
        'w_down': nrm(ks[28], (DEPTH, N_EXPERTS, D_FF_EXPERT, D_MODEL), D_FF_EXPERT ** -0.5),
        'norm_final': 1.0 + nrm(ks[29], (D_MODEL,), 0.02),
    }


def reference(x_prompt, x_sample, state_ssm_re, state_ssm_im, cache_conv, norm_mix, w_in, lam_re,
              lam_im, log_dt, ssm_b_re, ssm_b_im, ssm_c_re, ssm_c_im, ssm_d, w_glu, b_glu, w_conv,
              norm_a, norm_b, w_out, norm_ffn, w_router_group, b_router_group, w_router_expert,
              b_router_expert, w_gate, w_up, w_down, norm_final):
    weights = (norm_mix, w_in, lam_re, lam_im, log_dt, ssm_b_re, ssm_b_im, ssm_c_re, ssm_c_im, ssm_d,
               w_glu, b_glu, w_conv, norm_a, norm_b, w_out, norm_ffn, w_router_group, b_router_group,
               w_router_expert, b_router_expert, w_gate, w_up, w_down, norm_final)
    zero_re = jnp.zeros((DEPTH, x_prompt.shape[0], SSM_GROUPS, SSM_STATE), jnp.float32)
    zero_buf = jnp.zeros((DEPTH, x_prompt.shape[0], CONV_K - 1, CONV_CH), x_prompt.dtype)
    y_prompt, p_re, p_im, p_buf = _trunk(x_prompt, zero_re, zero_re, zero_buf, *weights)
    y_sample, s_re, s_im, s_buf = _trunk(x_sample, state_ssm_re, state_ssm_im, cache_conv, *weights)
    return (y_prompt, y_sample, p_re, p_im, p_buf, s_re, s_im, s_buf)
```

```python
import functools

import jax
import jax.numpy as jnp
from jax import lax
from jax.experimental import pallas as pl
from jax.experimental.pallas import tpu as pltpu

F32 = jnp.float32
BF16 = jnp.bfloat16
I32 = jnp.int32
U32 = jnp.uint32

D_MODEL = 1024
DEPTH = 2
T_PROMPT = 16384
N_SEQ_S = 32
LEN_S = 16
T_SAMPLE = N_SEQ_S * LEN_S
T_ALL = T_PROMPT + T_SAMPLE
SSM_W = 512
CONV_W = 512
N_G = 32
G_CH = 16
N_ST = 64
ST_W = N_G * N_ST
N_EGRP = 4
EPG = 8
N_EXP = 32
D_FF = 512
EPS = 1e-6

LANES = 128
TB = 512
NB = T_ALL // TB
CHUNK = 8
GPS = LANES // G_CH
N_SG = SSM_W // LANES
SG_ST = GPS * N_ST
CW = CHUNK * LANES
CR_P = T_PROMPT // CHUNK
CR_S = T_SAMPLE // CHUNK
CPS = LEN_S // CHUNK
CB = 256
TM = 256
P_CAP = 2 * T_ALL + N_EXP * TM
NT = P_CAP // TM
RT_ROWS = 40

VMEM_LIMIT = 56 * 1024 * 1024


def _rms(x, g):
    return x * lax.rsqrt(jnp.mean(x * x, axis=-1, keepdims=True) + EPS) * g


def _sigmoid(x):
    return 1.0 / (1.0 + jnp.exp(-x))


def _inproj_kernel(x_ref, gmix_ref, win_ref, wconv_ref, gb_ref, cz1_ref, cz2_ref,
                   u_ref, mixb_ref, ptail_ref, zs_ref, carry_ref):
    i = pl.program_id(0)

    @pl.when(i == 0)
    def _():
        carry_ref[...] = jnp.zeros_like(carry_ref)

    h = _rms(x_ref[...], gmix_ref[...]).astype(BF16)
    proj = jnp.dot(h, win_ref[...], preferred_element_type=F32)
    u_ref[...] = proj[:, :SSM_W]
    gate_b = proj[:, SSM_W:SSM_W + CONV_W]
    gate_c = proj[:, SSM_W + CONV_W:SSM_W + 2 * CONV_W]
    v = proj[:, SSM_W + 2 * CONV_W:]
    z = gate_c * v
    row = lax.broadcasted_iota(I32, (TB, 1), 0)
    r1 = pltpu.roll(z, 1, 0)
    r2 = pltpu.roll(z, 2, 0)
    c6 = carry_ref[6:7, :]
    c7 = carry_ref[7:8, :]
    z1p = jnp.where(row == 0, c7, r1)
    z2p = jnp.where(row == 0, c6, jnp.where(row == 1, c7, r2))
    pos = row % LEN_S
    z1s = jnp.where(pos == 0, cz1_ref[...], r1)
    z2s = jnp.where(pos < 2, cz2_ref[...], r2)
    is_sample = i == NB - 1
    z1 = jnp.where(is_sample, z1s, z1p)
    z2 = jnp.where(is_sample, z2s, z2p)
    w = wconv_ref[...]
    conv = w[0:1, :] * z2 + w[1:2, :] * z1 + w[2:3, :] * z
    yb = gate_b * conv
    mixb_ref[...] = _rms(yb, gb_ref[...]).astype(BF16)
    carry_ref[...] = z[TB - 8:, :]

    @pl.when(i == NB - 2)
    def _():
        ptail_ref[...] = z[TB - 8:, :]

    @pl.when(i == NB - 1)
    def _():
        zs_ref[...] = z


def _inproj_call(x, gmix, win, wconv, gb, cz1, cz2):
    const = lambda i: (0, 0)
    return pl.pallas_call(
        _inproj_kernel,
        grid=(NB,),
        in_specs=[
            pl.BlockSpec((TB, D_MODEL), lambda i: (i, 0)),
            pl.BlockSpec((1, D_MODEL), const),
            pl.BlockSpec((D_MODEL, 4 * SSM_W), const),
            pl.BlockSpec((3, CONV_W), const),
            pl.BlockSpec((1, CONV_W), const),
            pl.BlockSpec((TB, CONV_W), const),
            pl.BlockSpec((TB, CONV_W), const),
        ],
        out_specs=[
            pl.BlockSpec((TB, SSM_W), lambda i: (i, 0)),
            pl.BlockSpec((TB, CONV_W), lambda i: (i, 0)),
            pl.BlockSpec((8, CONV_W), const),
            pl.BlockSpec((TB, CONV_W), const),
        ],
        out_shape=[
            jax.ShapeDtypeStruct((T_ALL, SSM_W), F32),
            jax.ShapeDtypeStruct((T_ALL, CONV_W), BF16),
            jax.ShapeDtypeStruct((8, CONV_W), F32),
            jax.ShapeDtypeStruct((TB, CONV_W), F32),
        ],
        scratch_shapes=[pltpu.VMEM((8, CONV_W), F32)],
        compiler_params=pltpu.CompilerParams(
            dimension_semantics=("arbitrary",), vmem_limit_bytes=VMEM_LIMIT),
        name="inproj",
    )(x, gmix, win, wconv, gb, cz1, cz2)


def _ucat(uv_ref, sg):
    parts = [uv_ref[:, t * SSM_W + sg * LANES:t * SSM_W + (sg + 1) * LANES] for t in range(CHUNK)]
    return jnp.concatenate(parts, axis=1).astype(BF16)


def _ssm_local(uv_ref, ws_ref, sre, sim):
    for sg in range(N_SG):
        s = jnp.dot(_ucat(uv_ref, sg), ws_ref[sg], preferred_element_type=F32)
        sre[:, sg * SG_ST:(sg + 1) * SG_ST] = s[:, :SG_ST]
        sim[:, sg * SG_ST:(sg + 1) * SG_ST] = s[:, SG_ST:]


def _ssm_out(uv_ref, wy_ref, kb_ref, sre, sim, yv_ref):
    for sg in range(N_SG):
        sp = jnp.concatenate([sre[:, sg * SG_ST:(sg + 1) * SG_ST],
                              sim[:, sg * SG_ST:(sg + 1) * SG_ST]], axis=1).astype(BF16)
        y = (jnp.dot(sp, wy_ref[sg], preferred_element_type=F32)
             + jnp.dot(_ucat(uv_ref, sg), kb_ref[sg], preferred_element_type=F32))
        for t in range(CHUNK):
            yv_ref[:, t * SSM_W + sg * LANES:t * SSM_W + (sg + 1) * LANES] = (
                y[:, t * LANES:(t + 1) * LANES])


def _ssm_prompt_kernel(uv_ref, ws_ref, wy_ref, kb_ref, alr_ref, ali_ref,
                       yv_ref, str_ref, sti_ref, sre, sim, cre, cim):
    i = pl.program_id(0)

    @pl.when(i == 0)
    def _():
        cre[...] = jnp.zeros_like(cre)
        cim[...] = jnp.zeros_like(cim)

    _ssm_local(uv_ref, ws_ref, sre, sim)
    ar = alr_ref[...]
    ai = ali_ref[...]

    def step(c, carry):
        sr, si = carry
        lr = sre[pl.ds(c, 1), :]
        li = sim[pl.ds(c, 1), :]
        sre[pl.ds(c, 1), :] = sr
        sim[pl.ds(c, 1), :] = si
        return ar * sr - ai * si + lr, ar * si + ai * sr + li

    sr, si = lax.fori_loop(0, CB, step, (cre[...], cim[...]))
    cre[...] = sr
    cim[...] = si
    str_ref[...] = sr
    sti_ref[...] = si
    _ssm_out(uv_ref, wy_ref, kb_ref, sre, sim, yv_ref)


def _ssm_sample_kernel(uv_ref, ws_ref, wy_ref, kb_ref, alr_ref, ali_ref, h0r_ref, h0i_ref,
                       yv_ref, str_ref, sti_ref, sre, sim):
    _ssm_local(uv_ref, ws_ref, sre, sim)
    ar = alr_ref[...]
    ai = ali_ref[...]

    def per_stream(q, _):
        sr = h0r_ref[pl.ds(q, 1), :]
        si = h0i_ref[pl.ds(q, 1), :]
        for k in range(CPS):
            c = q * CPS + k
            lr = sre[pl.ds(c, 1), :]
            li = sim[pl.ds(c, 1), :]
            sre[pl.ds(c, 1), :] = sr
            sim[pl.ds(c, 1), :] = si
            sr, si = ar * sr - ai * si + lr, ar * si + ai * sr + li
        str_ref[pl.ds(q, 1), :] = sr
        sti_ref[pl.ds(q, 1), :] = si
        return 0

    lax.fori_loop(0, N_SEQ_S, per_stream, 0)
    _ssm_out(uv_ref, wy_ref, kb_ref, sre, sim, yv_ref)


def _ssm_calls(uv, ws, wy, kb, alr, ali, h0r, h0i):
    c2 = lambda i: (0, 0)
    c3 = lambda i: (0, 0, 0)
    one = pl.Buffered(1)
    mats = [pl.BlockSpec((N_SG, CW, 2 * SG_ST), c3, pipeline_mode=one),
            pl.BlockSpec((N_SG, 2 * SG_ST, CW), c3, pipeline_mode=one),
            pl.BlockSpec((N_SG, CW, CW), c3, pipeline_mode=one),
            pl.BlockSpec((1, ST_W), c2),
            pl.BlockSpec((1, ST_W), c2)]
    yv_p, pst_r, pst_i = pl.pallas_call(
        _ssm_prompt_kernel,
        grid=(CR_P // CB,),
        in_specs=[pl.BlockSpec((CB, CHUNK * SSM_W), lambda i: (i, 0))] + mats,
        out_specs=[pl.BlockSpec((CB, CHUNK * SSM_W), lambda i: (i, 0)),
                   pl.BlockSpec((1, ST_W), c2),
                   pl.BlockSpec((1, ST_W), c2)],
        out_shape=[jax.ShapeDtypeStruct((CR_P, CHUNK * SSM_W), F32),
                   jax.ShapeDtypeStruct((1, ST_W), F32),
                   jax.ShapeDtypeStruct((1, ST_W), F32)],
        scratch_shapes=[pltpu.VMEM((CB, ST_W), F32), pltpu.VMEM((CB, ST_W), F32),
                        pltpu.VMEM((1, ST_W), F32), pltpu.VMEM((1, ST_W), F32)],
        compiler_params=pltpu.CompilerParams(
            dimension_semantics=("arbitrary",), vmem_limit_bytes=VMEM_LIMIT),
        name="ssm_prompt",
    )(uv, ws, wy, kb, alr, ali)
    sblk = CR_P // CR_S
    yv_s, sst_r, sst_i = pl.pallas_call(
        _ssm_sample_kernel,
        grid=(1,),
        in_specs=[pl.BlockSpec((CR_S, CHUNK * SSM_W), lambda i: (sblk, 0))] + mats + [
            pl.BlockSpec((N_SEQ_S, ST_W), c2),
            pl.BlockSpec((N_SEQ_S, ST_W), c2)],
        out_specs=[pl.BlockSpec((CR_S, CHUNK * SSM_W), c2),
                   pl.BlockSpec((N_SEQ_S, ST_W), c2),
                   pl.BlockSpec((N_SEQ_S, ST_W), c2)],
        out_shape=[jax.ShapeDtypeStruct((CR_S, CHUNK * SSM_W), F32),
                   jax.ShapeDtypeStruct((N_SEQ_S, ST_W), F32),
                   jax.ShapeDtypeStruct((N_SEQ_S, ST_W), F32)],
        scratch_shapes=[pltpu.VMEM((CR_S, ST_W), F32), pltpu.VMEM((CR_S, ST_W), F32)],
        compiler_params=pltpu.CompilerParams(
            dimension_semantics=("arbitrary",), vmem_limit_bytes=VMEM_LIMIT),
        name="ssm_sample",
    )(uv, ws, wy, kb, alr, ali, h0r, h0i)
    return yv_p, yv_s, pst_r, pst_i, sst_r, sst_i


def _post_kernel(x_ref, yp_ref, ys_ref, mixb_ref, wglu_ref, bglu_ref, ga_ref, wout_ref, gffn_ref,
                 wrt_ref, brt_ref,
                 xmid_ref, ridx_ref, rgate_ref, cnt_ref, base_ref):
    i = pl.program_id(0)

    @pl.when(i == 0)
    def _():
        base_ref[...] = jnp.zeros_like(base_ref)

    y = jnp.where(i == NB - 1, ys_ref[...], yp_ref[...])
    z = 0.5 * y * (1.0 + jnp.tanh(0.7978845608028654 * (y + 0.044715 * (y * y * y))))
    gl = jnp.dot(z.astype(BF16), wglu_ref[...], preferred_element_type=F32) + bglu_ref[...]
    out_a = z * _sigmoid(gl)
    mix = jnp.concatenate([_rms(out_a, ga_ref[...]).astype(BF16), mixb_ref[...]], axis=1)
    xm = x_ref[...] + jnp.dot(mix, wout_ref[...], preferred_element_type=F32)
    xmid_ref[...] = xm
    hb = _rms(xm, gffn_ref[...]).astype(BF16)

    logit = lax.dot_general(wrt_ref[...], hb, (((1,), (1,)), ((), ())),
                            preferred_element_type=F32) + brt_ref[...]
    fine = logit[0:N_EXP, :]
    coarse = logit[N_EXP:N_EXP + N_EGRP, :]
    cmax = jnp.max(coarse, axis=0, keepdims=True)
    gi = lax.broadcasted_iota(I32, (N_EGRP, TB), 0).astype(F32)
    grp = jnp.min(jnp.where(coarse == cmax, gi, float(N_EGRP)), axis=0, keepdims=True)
    pg = 1.0 / jnp.sum(jnp.exp(coarse - cmax), axis=0, keepdims=True)
    eidx = lax.broadcasted_iota(I32, (N_EXP, TB), 0)
    ei = eidx.astype(F32)
    egrp = (eidx // EPG).astype(F32)
    neg = jnp.float32(-jnp.inf)
    fm = jnp.where(egrp == grp, fine, neg)
    v0 = jnp.max(fm, axis=0, keepdims=True)
    e0 = jnp.min(jnp.where(fm == v0, ei, float(N_EXP)), axis=0, keepdims=True)
    fm2 = jnp.where(ei == e0, neg, fm)
    v1 = jnp.max(fm2, axis=0, keepdims=True)
    e1 = jnp.min(jnp.where(fm2 == v1, ei, float(N_EXP)), axis=0, keepdims=True)
    tt = jnp.exp(v1 - v0)
    g0 = pg / (1.0 + tt)
    g1 = pg * tt / (1.0 + tt)

    sel0 = ei == e0
    sel1 = ei == e1
    cnt = jnp.where(sel0 | sel1, 1.0, 0.0)
    ta = lax.broadcasted_iota(I32, (TB, TB), 0)
    tb = lax.broadcasted_iota(I32, (TB, TB), 1)
    before = jnp.where(ta < tb, 1.0, 0.0).astype(BF16)
    cum = jnp.dot(cnt.astype(BF16), before, preferred_element_type=F32) + base_ref[...]
    rank0 = jnp.sum(jnp.where(sel0, cum, 0.0), axis=0, keepdims=True)
    rank1 = jnp.sum(jnp.where(sel1, cum, 0.0), axis=0, keepdims=True)
    base_new = base_ref[...] + jnp.sum(cnt, axis=1, keepdims=True)
    base_ref[...] = base_new
    cnt_ref[...] = jnp.broadcast_to(base_new, (N_EXP, LANES)).astype(I32)

    zi = jnp.zeros((1, TB), I32)
    ridx_ref[...] = jnp.concatenate(
        [e0.astype(I32), e1.astype(I32), rank0.astype(I32), rank1.astype(I32), zi, zi, zi, zi],
        axis=0)
    zf = jnp.zeros((1, TB), F32)
    rgate_ref[...] = jnp.concatenate([g0, g1, zf, zf, zf, zf, zf, zf], axis=0)


def _post_call(x, y_prompt, y_sample, mixb, wglu, bglu, ga, wout, gffn, wrt, brt):
    const = lambda i: (0, 0)
    return pl.pallas_call(
        _post_kernel,
        grid=(NB,),
        in_specs=[
            pl.BlockSpec((TB, D_MODEL), lambda i: (i, 0)),
            pl.BlockSpec((TB, SSM_W), lambda i: (jnp.minimum(i, NB - 2), 0)),
            pl.BlockSpec((TB, SSM_W), const),
            pl.BlockSpec((TB, CONV_W), lambda i: (i, 0)),
            pl.BlockSpec((SSM_W, SSM_W), const),
            pl.BlockSpec((1, SSM_W), const),
            pl.BlockSpec((1, SSM_W), const),
            pl.BlockSpec((D_MODEL, D_MODEL), const),
            pl.BlockSpec((1, D_MODEL), const),
            pl.BlockSpec((RT_ROWS, D_MODEL), const),
            pl.BlockSpec((RT_ROWS, 1), const),
        ],
        out_specs=[
            pl.BlockSpec((TB, D_MODEL), lambda i: (i, 0)),
            pl.BlockSpec((8, TB), lambda i: (0, i)),
            pl.BlockSpec((8, TB), lambda i: (0, i)),
            pl.BlockSpec((N_EXP, LANES), const),
        ],
        out_shape=[
            jax.ShapeDtypeStruct((T_ALL, D_MODEL), F32),
            jax.ShapeDtypeStruct((8, T_ALL), I32),
            jax.ShapeDtypeStruct((8, T_ALL), F32),
            jax.ShapeDtypeStruct((N_EXP, LANES), I32),
        ],
        scratch_shapes=[pltpu.VMEM((N_EXP, 1), F32)],
        compiler_params=pltpu.CompilerParams(
            dimension_semantics=("arbitrary",), vmem_limit_bytes=VMEM_LIMIT),
        name="post",
    )(x, y_prompt, y_sample, mixb, wglu, bglu, ga, wout, gffn, wrt, brt)


def _dispatch_kernel(pos_ref, xmid_ref, xs_in_ref, xs_ref, sem):
    del xs_in_ref

    def issue(r, _):
        for k in range(2):
            p = pos_ref[0, k, r]
            pltpu.make_async_copy(xmid_ref.at[pl.ds(r, 1)], xs_ref.at[pl.ds(p, 1)], sem).start()
        return 0

    lax.fori_loop(0, TB, issue, 0)
    for k in range(2):
        pltpu.make_async_copy(xmid_ref, xmid_ref, sem).wait()


def _dispatch_call(pos, xmid, xs_zero):
    return pl.pallas_call(
        _dispatch_kernel,
        grid=(NB,),
        in_specs=[
            pl.BlockSpec((1, 2, TB), lambda i: (i, 0, 0), memory_space=pltpu.SMEM),
            pl.BlockSpec((TB, D_MODEL), lambda i: (i, 0)),
            pl.BlockSpec(memory_space=pl.ANY),
        ],
        out_specs=pl.BlockSpec(memory_space=pl.ANY),
        out_shape=jax.ShapeDtypeStruct((P_CAP, D_MODEL), F32),
        scratch_shapes=[pltpu.SemaphoreType.DMA(())],
        input_output_aliases={2: 0},
        compiler_params=pltpu.CompilerParams(dimension_semantics=("arbitrary",)),
        name="dispatch",
    )(pos, xmid, xs_zero)


def _expert_kernel(te_ref, nu_ref, xs_ref, gffn_ref, wg_ref, wu_ref, wd_ref, r_ref,
                   wgb, wub, wdb):
    i = pl.program_id(0)
    prev = te_ref[jnp.maximum(i - 1, 0)]
    fresh = (i == 0) | (te_ref[i] != prev)

    @pl.when(fresh)
    def _():
        wgb[...] = wg_ref[...].astype(BF16)
        wub[...] = wu_ref[...].astype(BF16)
        wdb[...] = wd_ref[...].astype(BF16)

    used = i < nu_ref[0]

    @pl.when(used)
    def _():
        hb = _rms(xs_ref[...], gffn_ref[...]).astype(BF16)
        g = jnp.dot(hb, wgb[...], preferred_element_type=F32)
        u = jnp.dot(hb, wub[...], preferred_element_type=F32)
        hid = (g * _sigmoid(g) * u).astype(BF16)
        r_ref[...] = jnp.dot(hid, wdb[...], preferred_element_type=F32)

    @pl.when(jnp.logical_not(used))
    def _():
        r_ref[...] = jnp.zeros_like(r_ref)


def _expert_call(layer, te, nu, xs, gffn, w_gate, w_up, w_down):
    def xmap(i, te, nu):
        return (jnp.minimum(i, nu[0] - 1), 0)

    def wmap(i, te, nu):
        return (layer, te[i], 0, 0)

    return pl.pallas_call(
        _expert_kernel,
        grid_spec=pltpu.PrefetchScalarGridSpec(
            num_scalar_prefetch=2,
            grid=(NT,),
            in_specs=[
                pl.BlockSpec((TM, D_MODEL), xmap),
                pl.BlockSpec((1, D_MODEL), lambda i, te, nu: (0, 0)),
                pl.BlockSpec((None, None, D_MODEL, D_FF), wmap),
                pl.BlockSpec((None, None, D_MODEL, D_FF), wmap),
                pl.BlockSpec((None, None, D_FF, D_MODEL), wmap),
            ],
            out_specs=pl.BlockSpec((TM, D_MODEL), lambda i, te, nu: (i, 0)),
            scratch_shapes=[pltpu.VMEM((D_MODEL, D_FF), BF16),
                            pltpu.VMEM((D_MODEL, D_FF), BF16),
                            pltpu.VMEM((D_FF, D_MODEL), BF16)],
        ),
        out_shape=jax.ShapeDtypeStruct((P_CAP, D_MODEL), F32),
        compiler_params=pltpu.CompilerParams(
            dimension_semantics=("arbitrary",), vmem_limit_bytes=VMEM_LIMIT),
        name="experts",
    )(te, nu, xs, gffn, w_gate, w_up, w_down)


def _combine_kernel(pos_ref, xmid_ref, gate_ref, gfin_ref, r_ref, out_ref, buf, sem, *, final):
    def issue(r, _):
        for k in range(2):
            p = pos_ref[0, k, r]
            pltpu.make_async_copy(r_ref.at[pl.ds(p, 1)], buf.at[k, pl.ds(r, 1)], sem).start()
        return 0

    lax.fori_loop(0, TB, issue, 0)
    for k in range(2):
        pltpu.make_async_copy(buf.at[k], buf.at[k], sem).wait()
    gt = gate_ref[...]
    out = xmid_ref[...] + gt[:, 0:1] * buf[0] + gt[:, 1:2] * buf[1]
    if final:
        out = _rms(out, gfin_ref[...])
    out_ref[...] = out


def _combine_call(pos, xmid, gate_t, gfin, r, final):
    return pl.pallas_call(
        functools.partial(_combine_kernel, final=final),
        grid=(NB,),
        in_specs=[
            pl.BlockSpec((1, 2, TB), lambda i: (i, 0, 0), memory_space=pltpu.SMEM),
            pl.BlockSpec((TB, D_MODEL), lambda i: (i, 0)),
            pl.BlockSpec((TB, 2), lambda i: (i, 0)),
            pl.BlockSpec((1, D_MODEL), lambda i: (0, 0)),
            pl.BlockSpec(memory_space=pl.ANY),
        ],
        out_specs=pl.BlockSpec((TB, D_MODEL), lambda i: (i, 0)),
        out_shape=jax.ShapeDtypeStruct((T_ALL, D_MODEL), F32),
        scratch_shapes=[pltpu.VMEM((2, TB, D_MODEL), F32), pltpu.SemaphoreType.DMA(())],
        compiler_params=pltpu.CompilerParams(
            dimension_semantics=("arbitrary",), vmem_limit_bytes=VMEM_LIMIT),
        name="combine",
    )(pos, xmid, gate_t, gfin, r)


def _cmul(ar, ai, br, bi):
    return ar * br - ai * bi, ar * bi + ai * br


def _ssm_matrices(lam_re, lam_im, log_dt, b_re, b_im, c_re, c_im, d_skip):
    hp = lax.Precision.HIGHEST
    dt = jnp.exp(log_dt)[:, None]
    mag = jnp.exp(lam_re * dt)
    ab_re = mag * jnp.cos(lam_im * dt)
    ab_im = mag * jnp.sin(lam_im * dt)
    denom = lam_re * lam_re + lam_im * lam_im
    num_re = ab_re - 1.0
    f_re = (num_re * lam_re + ab_im * lam_im) / denom
    f_im = (ab_im * lam_re - num_re * lam_im) / denom
    bb_re = f_re[..., None] * b_re - f_im[..., None] * b_im
    bb_im = f_re[..., None] * b_im + f_im[..., None] * b_re
    pw_re = [jnp.ones_like(ab_re)]
    pw_im = [jnp.zeros_like(ab_im)]
    for _ in range(CHUNK):
        nr, ni = _cmul(pw_re[-1], pw_im[-1], ab_re, ab_im)
        pw_re.append(nr)
        pw_im.append(ni)
    eye = jnp.eye(GPS, dtype=F32)

    abr, abi = [], []
    for t in range(CHUNK):
        k = CHUNK - 1 - t
        r, m = _cmul(pw_re[k][..., None], pw_im[k][..., None], bb_re, bb_im)
        abr.append(r)
        abi.append(m)
    ab = jnp.stack([jnp.stack(abr), jnp.stack(abi)], axis=-1)
    ab = ab.reshape(CHUNK, N_SG, GPS, N_ST, G_CH, 2)
    ws = jnp.einsum('tsjphr,ji->stjhrip', ab, eye, precision=hp).reshape(N_SG, CW, 2 * SG_ST)

    car, cai = [], []
    for t in range(CHUNK + 1):
        r, m = _cmul(c_re, c_im, pw_re[t][:, None, :], pw_im[t][:, None, :])
        car.append(r)
        cai.append(m)
    ca = jnp.stack([jnp.stack(car[1:]), -jnp.stack(cai[1:])], axis=-1)
    ca = ca.reshape(CHUNK, N_SG, GPS, G_CH, N_ST, 2)
    wy = jnp.einsum('tsjhpr,ji->sriptjh', ca, eye, precision=hp).reshape(N_SG, 2 * SG_ST, CW)

    kk = []
    for k in range(CHUNK):
        kk.append(jnp.einsum('ghp,gpq->ghq', car[k], bb_re, precision=hp)
                  - jnp.einsum('ghp,gpq->ghq', cai[k], bb_im, precision=hp))
    kk[0] = kk[0] + d_skip.reshape(N_G, G_CH)[:, :, None] * jnp.eye(G_CH, dtype=F32)[None]
    zero = jnp.zeros_like(kk[0])
    rows = []
    for tp in range(CHUNK):
        rows.append(jnp.stack([kk[t - tp] if t >= tp else zero for t in range(CHUNK)]))
    kt = jnp.stack(rows).reshape(CHUNK, CHUNK, N_SG, GPS, G_CH, G_CH)
    kb = jnp.einsum('utsjhq,ji->suiqtjh', kt, eye, precision=hp).reshape(N_SG, CW, CW)

    al_re = pw_re[CHUNK].reshape(1, ST_W)
    al_im = pw_im[CHUNK].reshape(1, ST_W)
    return ws.astype(BF16), wy.astype(BF16), kb.astype(BF16), al_re, al_im


def _tile_tables(counts):
    padded = ((counts + TM - 1) // TM) * TM
    ends = jnp.cumsum(padded)
    offs = ends - padded
    n_used = (ends[-1] // TM).astype(I32)
    tile = jnp.arange(NT, dtype=I32)
    live = jnp.minimum(tile, n_used - 1)
    te = jnp.sum((ends // TM)[None, :] <= live[:, None], axis=1).astype(I32)
    return offs.astype(I32), te, n_used.reshape(1)


def kernel(x_prompt, x_sample, state_ssm_re, state_ssm_im, cache_conv, norm_mix, w_in, lam_re,
           lam_im, log_dt, ssm_b_re, ssm_b_im, ssm_c_re, ssm_c_im, ssm_d, w_glu, b_glu, w_conv,
           norm_a, norm_b, w_out, norm_ffn, w_router_group, b_router_group, w_router_expert,
           b_router_expert, w_gate, w_up, w_down, norm_final):
    x = jnp.concatenate([x_prompt.reshape(T_PROMPT, D_MODEL),
                         x_sample.reshape(T_SAMPLE, D_MODEL)], axis=0)
    gfin = norm_final.reshape(1, D_MODEL)
    p_re, p_im, p_buf, s_re, s_im, s_buf = [], [], [], [], [], []
    for l in range(DEPTH):
        cache = cache_conv[l]
        cz1 = jnp.pad(cache[:, 1:2], ((0, 0), (0, LEN_S - 1), (0, 0))).reshape(T_SAMPLE, CONV_W)
        cz2 = jnp.pad(cache, ((0, 0), (0, LEN_S - 2), (0, 0))).reshape(T_SAMPLE, CONV_W)
        u, mixb, ptail, zs = _inproj_call(
            x, norm_mix[l].reshape(1, D_MODEL), w_in[l].astype(BF16), w_conv[l],
            norm_b[l].reshape(1, CONV_W), cz1, cz2)

        ws, wy, kb, al_re, al_im = _ssm_matrices(
            lam_re[l], lam_im[l], log_dt[l], ssm_b_re[l], ssm_b_im[l], ssm_c_re[l], ssm_c_im[l],
            ssm_d[l])
        yv_p, yv_s, pst_r, pst_i, sst_r, sst_i = _ssm_calls(
            u.reshape(CR_P + CR_S, CHUNK * SSM_W), ws, wy, kb, al_re, al_im,
            state_ssm_re[l].reshape(N_SEQ_S, ST_W), state_ssm_im[l].reshape(N_SEQ_S, ST_W))

        wrt = jnp.concatenate([
            w_router_expert[l].transpose(0, 2, 1).reshape(N_EXP, D_MODEL),
            w_router_group[l].T,
            jnp.zeros((RT_ROWS - N_EXP - N_EGRP, D_MODEL), F32)], axis=0).astype(BF16)
        brt = jnp.concatenate([b_router_expert[l].reshape(N_EXP), b_router_group[l],
                               jnp.zeros((RT_ROWS - N_EXP - N_EGRP,), F32)]).reshape(RT_ROWS, 1)
        gffn = norm_ffn[l].reshape(1, D_MODEL)
        xmid, ridx, rgate, cnt = _post_call(
            x, yv_p.reshape(T_PROMPT, SSM_W), yv_s.reshape(T_SAMPLE, SSM_W), mixb,
            w_glu[l].astype(BF16), b_glu[l].reshape(1, SSM_W),
            norm_a[l].reshape(1, SSM_W), w_out[l].astype(BF16), gffn, wrt, brt)

        offs, te, n_used = _tile_tables(cnt[:, 0])
        pos = jnp.stack([offs[ridx[0]] + ridx[2], offs[ridx[1]] + ridx[3]])
        pos = pos.reshape(2, NB, TB).transpose(1, 0, 2)
        xs = _dispatch_call(pos, xmid, jnp.zeros((P_CAP, D_MODEL), F32))
        r = _expert_call(l, te, n_used, xs, gffn, w_gate, w_up, w_down)
        x = _combine_call(pos, xmid, rgate[:2].T, gfin, r, final=(l == DEPTH - 1))

        p_re.append(pst_r.reshape(1, N_G, N_ST))
        p_im.append(pst_i.reshape(1, N_G, N_ST))
        p_buf.append(ptail[6:8].reshape(1, 2, CONV_W))
        s_re.append(sst_r.reshape(N_SEQ_S, N_G, N_ST))
        s_im.append(sst_i.reshape(N_SEQ_S, N_G, N_ST))
        s_buf.append(zs.reshape(N_SEQ_S, LEN_S, CONV_W)[:, LEN_S - 2:])

    y_prompt = x[:T_PROMPT].reshape(1, T_PROMPT, D_MODEL)
    y_sample = x[T_PROMPT:].reshape(N_SEQ_S, LEN_S, D_MODEL)
    return (y_prompt, y_sample, jnp.stack(p_re), jnp.stack(p_im), jnp.stack(p_buf),
            jnp.stack(s_re), jnp.stack(s_im), jnp.stack(s_buf))
```

```python
import functools

import jax
import jax.numpy as jnp
from jax import lax
from jax.experimental import pallas as pl
from jax.experimental.pallas import tpu as pltpu

F32 = jnp.float32
BF16 = jnp.bfloat16
I32 = jnp.int32

D_MODEL = 1024
DEPTH = 2
T_PROMPT = 16384
N_SEQ_S = 32
LEN_S = 16
T_SAMPLE = N_SEQ_S * LEN_S
T_ALL = T_PROMPT + T_SAMPLE
SSM_W = 512
CONV_W = 512
N_G = 32
G_CH = 16
N_ST = 64
ST_W = N_G * N_ST
N_EGRP = 4
EPG = 8
N_EXP = 32
D_FF = 512
EPS = 1e-6

LANES = 128
SUBLANES = 8
TB = 512
NB = T_ALL // TB
NBP = T_PROMPT // TB
CHUNK = 8
CRB = TB // CHUNK
GPS = LANES // G_CH
N_SG = SSM_W // LANES
SG_ST = GPS * N_ST
CW = CHUNK * LANES
UVW = CHUNK * SSM_W
CR_P = T_PROMPT // CHUNK
CR_S = T_SAMPLE // CHUNK
CPS = LEN_S // CHUNK
CB = 256
NPB = CR_P // CB
TM = 256
P_CAP = 2 * T_ALL + N_EXP * TM
NT = P_CAP // TM
NZ = 2 * N_EXP
RT_ROWS = 40

VMEM_LIMIT = 56 * 1024 * 1024

assert T_SAMPLE == TB and TB % CHUNK == 0 and CR_S <= CB and LEN_S % CHUNK == 0


def _rms(x, g):
    return x * lax.rsqrt(jnp.mean(x * x, axis=-1, keepdims=True) + EPS) * g


def _sigmoid(x):
    return 1.0 / (1.0 + jnp.exp(-x))


def _stream_specs(s_block):
    return [pl.BlockSpec((TB, D_MODEL), lambda i: (jnp.minimum(i, NBP - 1), 0)),
            pl.BlockSpec((TB, D_MODEL), lambda i: (s_block, 0))]


def _inproj_kernel(xp_ref, xs_ref, gmix_ref, win_ref, wconv_ref, gb_ref, cz1_ref, cz2_ref,
                   uvp_ref, uvs_ref, mixb_ref, ptail_ref, zs_ref, carry_ref, u_scr):
    i = pl.program_id(0)
    is_sample = i == NB - 1

    @pl.when(i == 0)
    def _():
        carry_ref[...] = jnp.zeros_like(carry_ref)

    x = jnp.where(is_sample, xs_ref[...], xp_ref[...])
    h = _rms(x, gmix_ref[...]).astype(BF16)
    proj = jnp.dot(h, win_ref[...], preferred_element_type=F32)
    for k in range(N_SG):
        u_scr[k] = proj[:, k * LANES:(k + 1) * LANES]
    gate_b = proj[:, SSM_W:SSM_W + CONV_W]
    gate_c = proj[:, SSM_W + CONV_W:SSM_W + 2 * CONV_W]
    v = proj[:, SSM_W + 2 * CONV_W:]
    z = gate_c * v
    row = lax.broadcasted_iota(I32, (TB, 1), 0)
    r1 = pltpu.roll(z, 1, 0)
    r2 = pltpu.roll(z, 2, 0)
    c6 = carry_ref[6:7, :]
    c7 = carry_ref[7:8, :]
    z1p = jnp.where(row == 0, c7, r1)
    z2p = jnp.where(row == 0, c6, jnp.where(row == 1, c7, r2))
    pos = row % LEN_S
    z1s = jnp.where(pos == 0, cz1_ref[...], r1)
    z2s = jnp.where(pos < 2, cz2_ref[...], r2)
    z1 = jnp.where(is_sample, z1s, z1p)
    z2 = jnp.where(is_sample, z2s, z2p)
    w = wconv_ref[...]
    conv = w[0:1, :] * z2 + w[1:2, :] * z1 + w[2:3, :] * z
    yb = gate_b * conv
    mixb_ref[...] = _rms(yb, gb_ref[...]).astype(BF16)
    carry_ref[...] = z[TB - SUBLANES:, :]

    def put_chunk_rows(uv_ref):
        for t in range(CHUNK):
            for k in range(N_SG):
                uv_ref[:, t * SSM_W + k * LANES:t * SSM_W + (k + 1) * LANES] = (
                    u_scr[k, pl.ds(t, CRB, stride=CHUNK), :])

    @pl.when(jnp.logical_not(is_sample))
    def _():
        put_chunk_rows(uvp_ref)

    @pl.when(i == NB - 2)
    def _():
        ptail_ref[...] = z[TB - SUBLANES:, :]

    @pl.when(is_sample)
    def _():
        put_chunk_rows(uvs_ref)
        zs_ref[...] = z


def _inproj_call(xp, xs, s_block, gmix, win, wconv, gb, cz1, cz2):
    const = lambda i: (0, 0)
    return pl.pallas_call(
        _inproj_kernel,
        grid=(NB,),
        in_specs=_stream_specs(s_block) + [
            pl.BlockSpec((1, D_MODEL), const),
            pl.BlockSpec((D_MODEL, 4 * SSM_W), const),
            pl.BlockSpec((3, CONV_W), const),
            pl.BlockSpec((1, CONV_W), const),
            pl.BlockSpec((TB, CONV_W), const),
            pl.BlockSpec((TB, CONV_W), const),
        ],
        out_specs=[
            pl.BlockSpec((CRB, UVW), lambda i: (jnp.minimum(i, NBP - 1), 0)),
            pl.BlockSpec((CRB, UVW), const),
            pl.BlockSpec((TB, CONV_W), lambda i: (i, 0)),
            pl.BlockSpec((SUBLANES, CONV_W), const),
            pl.BlockSpec((TB, CONV_W), const),
        ],
        out_shape=[
            jax.ShapeDtypeStruct((CR_P, UVW), F32),
            jax.ShapeDtypeStruct((CR_S, UVW), F32),
            jax.ShapeDtypeStruct((T_ALL, CONV_W), BF16),
            jax.ShapeDtypeStruct((SUBLANES, CONV_W), F32),
            jax.ShapeDtypeStruct((TB, CONV_W), F32),
        ],
        scratch_shapes=[pltpu.VMEM((SUBLANES, CONV_W), F32), pltpu.VMEM((N_SG, TB, LANES), F32)],
        compiler_params=pltpu.CompilerParams(
            dimension_semantics=("arbitrary",), vmem_limit_bytes=VMEM_LIMIT),
        name="inproj",
    )(xp, xs, gmix, win, wconv, gb, cz1, cz2)


def _expand_chunk_matrices(abd_ref, cac_ref, kc_ref, ws, wy, kb):
    row_g = lax.broadcasted_iota(I32, (LANES, LANES), 0) // G_CH
    lane_half = lax.broadcasted_iota(I32, (LANES, LANES), 1) // N_ST
    col_g = lax.broadcasted_iota(I32, (N_ST, LANES), 1) // G_CH
    zero_blk = jnp.zeros((LANES, LANES), BF16)
    for s in range(N_SG):
        for t in range(CHUNK):
            for r in range(2):
                a = abd_ref[s, t, r]
                for m in range(GPS // 2):
                    ws[s, t * LANES:(t + 1) * LANES,
                       r * SG_ST + m * LANES:r * SG_ST + (m + 1) * LANES] = jnp.where(
                           row_g == 2 * m + lane_half, a, jnp.zeros_like(a))
                c = cac_ref[s, t, r]
                for g in range(GPS):
                    wy[s, r * SG_ST + g * N_ST:r * SG_ST + (g + 1) * N_ST,
                       t * LANES:(t + 1) * LANES] = jnp.where(col_g == g, c, jnp.zeros_like(c))
        for tp in range(CHUNK):
            for t in range(CHUNK):
                kb[s, tp * LANES:(tp + 1) * LANES, t * LANES:(t + 1) * LANES] = (
                    kc_ref[s, t - tp] if t >= tp else zero_blk)


def _ucat(uv_ref, sg):
    parts = [uv_ref[:, t * SSM_W + sg * LANES:t * SSM_W + (sg + 1) * LANES] for t in range(CHUNK)]
    return jnp.concatenate(parts, axis=1).astype(BF16)


def _ssm_local(uv_ref, ws, sre, sim):
    for sg in range(N_SG):
        s = jnp.dot(_ucat(uv_ref, sg), ws[sg], preferred_element_type=F32)
        sre[:, sg * SG_ST:(sg + 1) * SG_ST] = s[:, :SG_ST]
        sim[:, sg * SG_ST:(sg + 1) * SG_ST] = s[:, SG_ST:]


def _ssm_out(uv_ref, wy, kb, sre, sim, yv_ref):
    for sg in range(N_SG):
        sp = jnp.concatenate([sre[:, sg * SG_ST:(sg + 1) * SG_ST],
                              sim[:, sg * SG_ST:(sg + 1) * SG_ST]], axis=1).astype(BF16)
        y = (jnp.dot(sp, wy[sg], preferred_element_type=F32)
             + jnp.dot(_ucat(uv_ref, sg), kb[sg], preferred_element_type=F32))
        for t in range(CHUNK):
            yv_ref[:, t * SSM_W + sg * LANES:t * SSM_W + (sg + 1) * LANES] = (
                y[:, t * LANES:(t + 1) * LANES])


def _ssm_kernel(uvp_ref, uvs_ref, abd_ref, cac_ref, kc_ref, alr_ref, ali_ref, h0r_ref, h0i_ref,
                yvp_ref, yvs_ref, pstr_ref, psti_ref, sstr_ref, ssti_ref,
                ws, wy, kb, sre, sim, cre, cim):
    i = pl.program_id(0)

    @pl.when(i == 0)
    def _():
        cre[...] = jnp.zeros_like(cre)
        cim[...] = jnp.zeros_like(cim)
        _expand_chunk_matrices(abd_ref, cac_ref, kc_ref, ws, wy, kb)

    ar = alr_ref[...]
    ai = ali_ref[...]

    @pl.when(i < NPB)
    def _():
        _ssm_local(uvp_ref, ws, sre, sim)

        def step(c, carry):
            sr, si = carry
            lr = sre[pl.ds(c, 1), :]
            li = sim[pl.ds(c, 1), :]
            sre[pl.ds(c, 1), :] = sr
            sim[pl.ds(c, 1), :] = si
            return ar * sr - ai * si + lr, ar * si + ai * sr + li

        sr, si = lax.fori_loop(0, CB, step, (cre[...], cim[...]))
        cre[...] = sr
        cim[...] = si
        pstr_ref[...] = sr
        psti_ref[...] = si
        _ssm_out(uvp_ref, wy, kb, sre, sim, yvp_ref)

    @pl.when(i == NPB)
    def _():
        sre_s = sre.at[0:CR_S]
        sim_s = sim.at[0:CR_S]
        _ssm_local(uvs_ref, ws, sre_s, sim_s)

        def per_stream(q, _):
            sr = h0r_ref[pl.ds(q, 1), :]
            si = h0i_ref[pl.ds(q, 1), :]
            for k in range(CPS):
                c = q * CPS + k
                lr = sre_s[pl.ds(c, 1), :]
                li = sim_s[pl.ds(c, 1), :]
                sre_s[pl.ds(c, 1), :] = sr
                sim_s[pl.ds(c, 1), :] = si
                sr, si = ar * sr - ai * si + lr, ar * si + ai * sr + li
            sstr_ref[pl.ds(q, 1), :] = sr
            ssti_ref[pl.ds(q, 1), :] = si
            return 0

        lax.fori_loop(0, N_SEQ_S, per_stream, 0)
        _ssm_out(uvs_ref, wy, kb, sre_s, sim_s, yvs_ref)


def _ssm_call(uvp, uvs, abd, cac, kc, alr, ali, h0r, h0i):
    c2 = lambda i: (0, 0)
    c4 = lambda i: (0, 0, 0, 0)
    c5 = lambda i: (0, 0, 0, 0, 0)
    one = pl.Buffered(1)
    pblk = lambda i: (jnp.minimum(i, NPB - 1), 0)
    return pl.pallas_call(
        _ssm_kernel,
        grid=(NPB + 1,),
        in_specs=[pl.BlockSpec((CB, UVW), pblk),
                  pl.BlockSpec((CR_S, UVW), c2),
                  pl.BlockSpec((N_SG, CHUNK, 2, LANES, LANES), c5, pipeline_mode=one),
                  pl.BlockSpec((N_SG, CHUNK, 2, N_ST, LANES), c5, pipeline_mode=one),
                  pl.BlockSpec((N_SG, CHUNK, LANES, LANES), c4, pipeline_mode=one),
                  pl.BlockSpec((1, ST_W), c2),
                  pl.BlockSpec((1, ST_W), c2),
                  pl.BlockSpec((N_SEQ_S, ST_W), c2),
                  pl.BlockSpec((N_SEQ_S, ST_W), c2)],
        out_specs=[pl.BlockSpec((CB, UVW), pblk),
                   pl.BlockSpec((CR_S, UVW), c2),
                   pl.BlockSpec((1, ST_W), c2),
                   pl.BlockSpec((1, ST_W), c2),
                   pl.BlockSpec((N_SEQ_S, ST_W), c2),
                   pl.BlockSpec((N_SEQ_S, ST_W), c2)],
        out_shape=[jax.ShapeDtypeStruct((CR_P, UVW), F32),
                   jax.ShapeDtypeStruct((CR_S, UVW), F32),
                   jax.ShapeDtypeStruct((1, ST_W), F32),
                   jax.ShapeDtypeStruct((1, ST_W), F32),
                   jax.ShapeDtypeStruct((N_SEQ_S, ST_W), F32),
                   jax.ShapeDtypeStruct((N_SEQ_S, ST_W), F32)],
        scratch_shapes=[pltpu.VMEM((N_SG, CW, 2 * SG_ST), BF16),
                        pltpu.VMEM((N_SG, 2 * SG_ST, CW), BF16),
                        pltpu.VMEM((N_SG, CW, CW), BF16),
                        pltpu.VMEM((CB, ST_W), F32), pltpu.VMEM((CB, ST_W), F32),
                        pltpu.VMEM((1, ST_W), F32), pltpu.VMEM((1, ST_W), F32)],
        compiler_params=pltpu.CompilerParams(
            dimension_semantics=("arbitrary",), vmem_limit_bytes=VMEM_LIMIT),
        name="ssm",
    )(uvp, uvs, abd, cac, kc, alr, ali, h0r, h0i)


def _post_kernel(xp_ref, xs_ref, yvp_ref, yvs_ref, mixb_ref, wglu_ref, bglu_ref, ga_ref,
                 wout_ref, gffn_ref, wrt_ref, brt_ref,
                 xmid_ref, ridx_ref, rgate_ref, cnt_ref, base_ref, y_scr):
    i = pl.program_id(0)
    is_sample = i == NB - 1

    @pl.when(i == 0)
    def _():
        base_ref[...] = jnp.zeros_like(base_ref)

    yv = jnp.where(is_sample, yvs_ref[...], yvp_ref[...])
    for t in range(CHUNK):
        for k in range(N_SG):
            y_scr[k, pl.ds(t, CRB, stride=CHUNK), :] = (
                yv[:, t * SSM_W + k * LANES:t * SSM_W + (k + 1) * LANES])
    y = jnp.concatenate([y_scr[k] for k in range(N_SG)], axis=1)
    z = 0.5 * y * (1.0 + jnp.tanh(0.7978845608028654 * (y + 0.044715 * (y * y * y))))
    gl = jnp.dot(z.astype(BF16), wglu_ref[...], preferred_element_type=F32) + bglu_ref[...]
    out_a = z * _sigmoid(gl)
    mix = jnp.concatenate([_rms(out_a, ga_ref[...]).astype(BF16), mixb_ref[...]], axis=1)
    x = jnp.where(is_sample, xs_ref[...], xp_ref[...])
    xm = x + jnp.dot(mix, wout_ref[...], preferred_element_type=F32)
    xmid_ref[...] = xm
    hb = _rms(xm, gffn_ref[...]).astype(BF16)

    logit = lax.dot_general(wrt_ref[...], hb, (((1,), (1,)), ((), ())),
                            preferred_element_type=F32) + brt_ref[...]
    fine = logit[0:N_EXP, :]
    coarse = logit[N_EXP:N_EXP + N_EGRP, :]
    cmax = jnp.max(coarse, axis=0, keepdims=True)
    gi = lax.broadcasted_iota(I32, (N_EGRP, TB), 0).astype(F32)
    grp = jnp.min(jnp.where(coarse == cmax, gi, float(N_EGRP)), axis=0, keepdims=True)
    pg = 1.0 / jnp.sum(jnp.exp(coarse - cmax), axis=0, keepdims=True)
    eidx = lax.broadcasted_iota(I32, (N_EXP, TB), 0)
    ei = eidx.astype(F32)
    egrp = (eidx // EPG).astype(F32)
    neg = jnp.float32(-jnp.inf)
    fm = jnp.where(egrp == grp, fine, neg)
    v0 = jnp.max(fm, axis=0, keepdims=True)
    e0 = jnp.min(jnp.where(fm == v0, ei, float(N_EXP)), axis=0, keepdims=True)
    fm2 = jnp.where(ei == e0, neg, fm)
    v1 = jnp.max(fm2, axis=0, keepdims=True)
    e1 = jnp.min(jnp.where(fm2 == v1, ei, float(N_EXP)), axis=0, keepdims=True)
    tt = jnp.exp(v1 - v0)
    g0 = pg / (1.0 + tt)
    g1 = pg * tt / (1.0 + tt)

    sel0 = ei == e0
    sel1 = ei == e1
    cnt = jnp.where(sel0 | sel1, 1.0, 0.0)
    ta = lax.broadcasted_iota(I32, (TB, TB), 0)
    tb = lax.broadcasted_iota(I32, (TB, TB), 1)
    before = jnp.where(ta < tb, 1.0, 0.0).astype(BF16)
    cum = jnp.dot(cnt.astype(BF16), before, preferred_element_type=F32) + base_ref[...]
    rank0 = jnp.sum(jnp.where(sel0, cum, 0.0), axis=0, keepdims=True)
    rank1 = jnp.sum(jnp.where(sel1, cum, 0.0), axis=0, keepdims=True)
    base_new = base_ref[...] + jnp.sum(cnt, axis=1, keepdims=True)
    base_ref[...] = base_new
    cnt_ref[...] = jnp.broadcast_to(base_new, (N_EXP, LANES)).astype(I32)

    zi = jnp.zeros((1, TB), I32)
    ridx_ref[...] = jnp.concatenate(
        [e0.astype(I32), e1.astype(I32), rank0.astype(I32), rank1.astype(I32), zi, zi, zi, zi],
        axis=0)
    zf = jnp.zeros((1, TB), F32)
    rgate_ref[...] = jnp.concatenate([g0, g1, zf, zf, zf, zf, zf, zf], axis=0)


def _post_call(xp, xs, s_block, yvp, yvs, mixb, wglu, bglu, ga, wout, gffn, wrt, brt):
    const = lambda i: (0, 0)
    return pl.pallas_call(
        _post_kernel,
        grid=(NB,),
        in_specs=_stream_specs(s_block) + [
            pl.BlockSpec((CRB, UVW), lambda i: (jnp.minimum(i, NBP - 1), 0)),
            pl.BlockSpec((CRB, UVW), const),
            pl.BlockSpec((TB, CONV_W), lambda i: (i, 0)),
            pl.BlockSpec((SSM_W, SSM_W), const),
            pl.BlockSpec((1, SSM_W), const),
            pl.BlockSpec((1, SSM_W), const),
            pl.BlockSpec((D_MODEL, D_MODEL), const),
            pl.BlockSpec((1, D_MODEL), const),
            pl.BlockSpec((RT_ROWS, D_MODEL), const),
            pl.BlockSpec((RT_ROWS, 1), const),
        ],
        out_specs=[
            pl.BlockSpec((TB, D_MODEL), lambda i: (i, 0)),
            pl.BlockSpec((SUBLANES, TB), lambda i: (0, i)),
            pl.BlockSpec((SUBLANES, TB), lambda i: (0, i)),
            pl.BlockSpec((N_EXP, LANES), const),
        ],
        out_shape=[
            jax.ShapeDtypeStruct((T_ALL, D_MODEL), F32),
            jax.ShapeDtypeStruct((SUBLANES, T_ALL), I32),
            jax.ShapeDtypeStruct((SUBLANES, T_ALL), F32),
            jax.ShapeDtypeStruct((N_EXP, LANES), I32),
        ],
        scratch_shapes=[pltpu.VMEM((N_EXP, 1), F32), pltpu.VMEM((N_SG, TB, LANES), F32)],
        compiler_params=pltpu.CompilerParams(
            dimension_semantics=("arbitrary",), vmem_limit_bytes=VMEM_LIMIT),
        name="post",
    )(xp, xs, yvp, yvs, mixb, wglu, bglu, ga, wout, gffn, wrt, brt)


def _dispatch_kernel(zt_ref, pos_ref, xmid_ref, xs_ref, zbuf, sem, zsem):
    i = pl.program_id(0)

    @pl.when(i == 0)
    def _():
        zbuf[...] = jnp.zeros_like(zbuf)

        def fill(j, _):
            start = pl.multiple_of(zt_ref[j] * TM, TM)
            pltpu.make_async_copy(zbuf, xs_ref.at[pl.ds(start, TM)], zsem).start()
            return 0

        n_fill = zt_ref[NZ]
        lax.fori_loop(0, n_fill, fill, 0)

        def fill_wait(j, _):
            pltpu.make_async_copy(zbuf, xs_ref.at[pl.ds(0, TM)], zsem).wait()
            return 0

        lax.fori_loop(0, n_fill, fill_wait, 0)

    def issue(r, _):
        for k in range(2):
            p = pos_ref[0, k, r]
            pltpu.make_async_copy(xmid_ref.at[pl.ds(r, 1)], xs_ref.at[pl.ds(p, 1)], sem).start()
        return 0

    lax.fori_loop(0, TB, issue, 0)
    for k in range(2):
        pltpu.make_async_copy(xmid_ref, xmid_ref, sem).wait()


def _dispatch_call(zt, pos, xmid):
    return pl.pallas_call(
        _dispatch_kernel,
        grid_spec=pltpu.PrefetchScalarGridSpec(
            num_scalar_prefetch=1,
            grid=(NB,),
            in_specs=[
                pl.BlockSpec((1, 2, TB), lambda i, zt: (i, 0, 0), memory_space=pltpu.SMEM),
                pl.BlockSpec((TB, D_MODEL), lambda i, zt: (i, 0)),
            ],
            out_specs=pl.BlockSpec(memory_space=pl.ANY),
            scratch_shapes=[pltpu.VMEM((TM, D_MODEL), F32),
                            pltpu.SemaphoreType.DMA(()), pltpu.SemaphoreType.DMA(())],
        ),
        out_shape=jax.ShapeDtypeStruct((P_CAP, D_MODEL), F32),
        compiler_params=pltpu.CompilerParams(dimension_semantics=("arbitrary",)),
        name="dispatch",
    )(zt, pos, xmid)


def _expert_kernel(te_ref, nu_ref, xs_ref, gffn_ref, wg_ref, wu_ref, wd_ref, r_ref,
                   wgb, wub, wdb):
    i = pl.program_id(0)
    prev = te_ref[jnp.maximum(i - 1, 0)]
    fresh = (i == 0) | (te_ref[i] != prev)

    @pl.when(fresh)
    def _():
        wgb[...] = wg_ref[...].astype(BF16)
        wub[...] = wu_ref[...].astype(BF16)
        wdb[...] = wd_ref[...].astype(BF16)

    used = i < nu_ref[0]

    @pl.when(used)
    def _():
        hb = _rms(xs_ref[...], gffn_ref[...]).astype(BF16)
        g = jnp.dot(hb, wgb[...], preferred_element_type=F32)
        u = jnp.dot(hb, wub[...], preferred_element_type=F32)
        hid = (g * _sigmoid(g) * u).astype(BF16)
        r_ref[...] = jnp.dot(hid, wdb[...], preferred_element_type=F32)

    @pl.when(jnp.logical_not(used))
    def _():
        r_ref[...] = jnp.zeros_like(r_ref)


def _expert_call(layer, te, nu, xs, gffn, w_gate, w_up, w_down):
    def xmap(i, te, nu):
        return (jnp.minimum(i, nu[0] - 1), 0)

    def wmap(i, te, nu):
        return (layer, te[i], 0, 0)

    return pl.pallas_call(
        _expert_kernel,
        grid_spec=pltpu.PrefetchScalarGridSpec(
            num_scalar_prefetch=2,
            grid=(NT,),
            in_specs=[
                pl.BlockSpec((TM, D_MODEL), xmap),
                pl.BlockSpec((1, D_MODEL), lambda i, te, nu: (0, 0)),
                pl.BlockSpec((None, None, D_MODEL, D_FF), wmap),
                pl.BlockSpec((None, None, D_MODEL, D_FF), wmap),
                pl.BlockSpec((None, None, D_FF, D_MODEL), wmap),
            ],
            out_specs=pl.BlockSpec((TM, D_MODEL), lambda i, te, nu: (i, 0)),
            scratch_shapes=[pltpu.VMEM((D_MODEL, D_FF), BF16),
                            pltpu.VMEM((D_MODEL, D_FF), BF16),
                            pltpu.VMEM((D_FF, D_MODEL), BF16)],
        ),
        out_shape=jax.ShapeDtypeStruct((P_CAP, D_MODEL), F32),
        compiler_params=pltpu.CompilerParams(
            dimension_semantics=("arbitrary",), vmem_limit_bytes=VMEM_LIMIT),
        name="experts",
    )(te, nu, xs, gffn, w_gate, w_up, w_down)


def _combine_kernel(pos_ref, xmid_ref, gate_ref, gfin_ref, r_ref, *rest, final):
    if final:
        outp_ref, outs_ref, buf, sem = rest
    else:
        out_ref, buf, sem = rest
    i = pl.program_id(0)

    def issue(r, _):
        for k in range(2):
            p = pos_ref[0, k, r]
            pltpu.make_async_copy(r_ref.at[pl.ds(p, 1)], buf.at[k, pl.ds(r, 1)], sem).start()
        return 0

    lax.fori_loop(0, TB, issue, 0)
    for k in range(2):
        pltpu.make_async_copy(buf.at[k], buf.at[k], sem).wait()
    gt = gate_ref[...]
    out = xmid_ref[...] + gt[:, 0:1] * buf[0] + gt[:, 1:2] * buf[1]
    if not final:
        out_ref[...] = out
    else:
        out = _rms(out, gfin_ref[...])

        @pl.when(i < NB - 1)
        def _():
            outp_ref[...] = out

        @pl.when(i == NB - 1)
        def _():
            outs_ref[...] = out


def _combine_call(pos, xmid, gate_t, gfin, r, final):
    if final:
        out_specs = [pl.BlockSpec((TB, D_MODEL), lambda i: (jnp.minimum(i, NBP - 1), 0)),
                     pl.BlockSpec((TB, D_MODEL), lambda i: (0, 0))]
        out_shape = [jax.ShapeDtypeStruct((T_PROMPT, D_MODEL), F32),
                     jax.ShapeDtypeStruct((T_SAMPLE, D_MODEL), F32)]
    else:
        out_specs = pl.BlockSpec((TB, D_MODEL), lambda i: (i, 0))
        out_shape = jax.ShapeDtypeStruct((T_ALL, D_MODEL), F32)
    return pl.pallas_call(
        functools.partial(_combine_kernel, final=final),
        grid=(NB,),
        in_specs=[
            pl.BlockSpec((1, 2, TB), lambda i: (i, 0, 0), memory_space=pltpu.SMEM),
            pl.BlockSpec((TB, D_MODEL), lambda i: (i, 0)),
            pl.BlockSpec((TB, 2), lambda i: (i, 0)),
            pl.BlockSpec((1, D_MODEL), lambda i: (0, 0)),
            pl.BlockSpec(memory_space=pl.ANY),
        ],
        out_specs=out_specs,
        out_shape=out_shape,
        scratch_shapes=[pltpu.VMEM((2, TB, D_MODEL), F32), pltpu.SemaphoreType.DMA(())],
        compiler_params=pltpu.CompilerParams(
            dimension_semantics=("arbitrary",), vmem_limit_bytes=VMEM_LIMIT),
        name="combine",
    )(pos, xmid, gate_t, gfin, r)


def _ssm_compact(lam_re, lam_im, log_dt, b_re, b_im, c_re, c_im, d_skip):
    nl = lam_re.shape[0]
    dt = jnp.exp(log_dt)[..., None]
    kpow = jnp.arange(CHUNK + 1, dtype=F32).reshape(CHUNK + 1, 1, 1, 1)
    mag = jnp.exp(kpow * (lam_re * dt))
    pw_re = mag * jnp.cos(kpow * (lam_im * dt))
    pw_im = mag * jnp.sin(kpow * (lam_im * dt))
    ab_re, ab_im = pw_re[1], pw_im[1]
    denom = lam_re * lam_re + lam_im * lam_im
    num_re = ab_re - 1.0
    f_re = (num_re * lam_re + ab_im * lam_im) / denom
    f_im = (ab_im * lam_re - num_re * lam_im) / denom
    bb_re = f_re[..., None] * b_re - f_im[..., None] * b_im
    bb_im = f_re[..., None] * b_im + f_im[..., None] * b_re

    krev = (CHUNK - 1.0) - kpow[:CHUNK]
    rmag = jnp.exp(krev * (lam_re * dt))
    rev_re = (rmag * jnp.cos(krev * (lam_im * dt)))[..., None]
    rev_im = (rmag * jnp.sin(krev * (lam_im * dt)))[..., None]
    ab = jnp.stack([rev_re * bb_re - rev_im * bb_im, rev_re * bb_im + rev_im * bb_re])
    ab = ab.reshape(2, CHUNK, nl, N_SG, GPS, N_ST, G_CH)
    ab = ab.transpose(2, 3, 1, 0, 4, 6, 5).reshape(nl, N_SG, CHUNK, 2, LANES, N_ST)
    abd = jnp.concatenate([ab, ab], axis=-1).astype(BF16)

    pr = pw_re[:, :, :, None, :]
    pi = pw_im[:, :, :, None, :]
    ca_re = c_re * pr - c_im * pi
    ca_im = c_re * pi + c_im * pr
    ca = jnp.stack([ca_re[1:], -ca_im[1:]])
    ca = ca.reshape(2, CHUNK, nl, N_SG, GPS, G_CH, N_ST)
    cac = ca.transpose(2, 3, 1, 0, 6, 4, 5).reshape(nl, N_SG, CHUNK, 2, N_ST, LANES).astype(BF16)

    bq_re = jnp.swapaxes(bb_re, -1, -2)[:, :, None]
    bq_im = jnp.swapaxes(bb_im, -1, -2)[:, :, None]
    kk = jnp.sum(ca_re[:CHUNK, :, :, :, None, :] * bq_re
                 - ca_im[:CHUNK, :, :, :, None, :] * bq_im, axis=-1)
    skip = d_skip.reshape(nl, N_G, G_CH)[..., None] * jnp.eye(G_CH, dtype=F32)
    kk = jnp.concatenate([kk[:1] + skip[None], kk[1:]], axis=0)
    kk = kk.reshape(CHUNK, nl, N_SG, GPS, G_CH, G_CH)
    kk = kk.transpose(1, 2, 0, 5, 3, 4)
    eye = jnp.eye(GPS, dtype=F32).reshape(GPS, 1, GPS, 1)
    kc = (kk[:, :, :, None] * eye).reshape(nl, N_SG, CHUNK, LANES, LANES).astype(BF16)

    al_re = pw_re[CHUNK].reshape(nl, 1, ST_W)
    al_im = pw_im[CHUNK].reshape(nl, 1, ST_W)
    return abd, cac, kc, al_re, al_im


def _tile_tables(counts):
    padded = ((counts + TM - 1) // TM) * TM
    ends = jnp.cumsum(padded)
    offs = ends - padded
    end_tile = ends // TM
    n_used = end_tile[-1].astype(I32)
    tile = jnp.arange(NT, dtype=I32)
    live = jnp.minimum(tile, n_used - 1)
    te = jnp.sum(end_tile[None, :] <= live[:, None], axis=1).astype(I32)
    is_last = jnp.any((end_tile[None, :] - 1 == tile[:, None]) & (padded[None, :] > 0), axis=1)
    need = is_last | (tile >= n_used)
    slot = jnp.cumsum(need.astype(I32)) - 1
    hit = need[None, :] & (slot[None, :] == jnp.arange(NZ, dtype=I32)[:, None])
    zt = jnp.sum(jnp.where(hit, tile[None, :], 0), axis=1).astype(I32)
    zt = jnp.concatenate([zt, jnp.sum(need.astype(I32)).reshape(1)])
    return offs.astype(I32), te, n_used.reshape(1), zt


def kernel(x_prompt, x_sample, state_ssm_re, state_ssm_im, cache_conv, norm_mix, w_in, lam_re,
           lam_im, log_dt, ssm_b_re, ssm_b_im, ssm_c_re, ssm_c_im, ssm_d, w_glu, b_glu, w_conv,
           norm_a, norm_b, w_out, norm_ffn, w_router_group, b_router_group, w_router_expert,
           b_router_expert, w_gate, w_up, w_down, norm_final):
    xp = x_prompt.reshape(T_PROMPT, D_MODEL)
    xs = x_sample.reshape(T_SAMPLE, D_MODEL)
    s_block = 0
    gfin = norm_final.reshape(1, D_MODEL)
    abd, cac, kc, al_re, al_im = _ssm_compact(
        lam_re, lam_im, log_dt, ssm_b_re, ssm_b_im, ssm_c_re, ssm_c_im, ssm_d)
    p_re, p_im, p_buf, s_re, s_im, s_buf = [], [], [], [], [], []
    for l in range(DEPTH):
        cache = cache_conv[l]
        cz1 = jnp.pad(cache[:, 1:2], ((0, 0), (0, LEN_S - 1), (0, 0))).reshape(T_SAMPLE, CONV_W)
        cz2 = jnp.pad(cache, ((0, 0), (0, LEN_S - 2), (0, 0))).reshape(T_SAMPLE, CONV_W)
        uvp, uvs, mixb, ptail, zs = _inproj_call(
            xp, xs, s_block, norm_mix[l].reshape(1, D_MODEL), w_in[l].astype(BF16), w_conv[l],
            norm_b[l].reshape(1, CONV_W), cz1, cz2)

        yvp, yvs, pst_r, pst_i, sst_r, sst_i = _ssm_call(
            uvp, uvs, abd[l], cac[l], kc[l], al_re[l], al_im[l],
            state_ssm_re[l].reshape(N_SEQ_S, ST_W), state_ssm_im[l].reshape(N_SEQ_S, ST_W))

        wrt = jnp.concatenate([
            w_router_expert[l].transpose(0, 2, 1).reshape(N_EXP, D_MODEL),
            w_router_group[l].T,
            jnp.zeros((RT_ROWS - N_EXP - N_EGRP, D_MODEL), F32)], axis=0).astype(BF16)
        brt = jnp.concatenate([b_router_expert[l].reshape(N_EXP), b_router_group[l],
                               jnp.zeros((RT_ROWS - N_EXP - N_EGRP,), F32)]).reshape(RT_ROWS, 1)
        gffn = norm_ffn[l].reshape(1, D_MODEL)
        xmid, ridx, rgate, cnt = _post_call(
            xp, xs, s_block, yvp, yvs, mixb, w_glu[l].astype(BF16), b_glu[l].reshape(1, SSM_W),
            norm_a[l].reshape(1, SSM_W), w_out[l].astype(BF16), gffn, wrt, brt)

        offs, te, n_used, zt = _tile_tables(cnt[:, 0])
        pos = jnp.stack([offs[ridx[0]] + ridx[2], offs[ridx[1]] + ridx[3]])
        pos = pos.reshape(2, NB, TB).transpose(1, 0, 2)
        xsort = _dispatch_call(zt, pos, xmid)
        r = _expert_call(l, te, n_used, xsort, gffn, w_gate, w_up, w_down)
        final = l == DEPTH - 1
        out = _combine_call(pos, xmid, rgate[:2].T, gfin, r, final=final)
        if final:
            y_prompt, y_sample = out
        else:
            xp = xs = out
            s_block = NB - 1

        p_re.append(pst_r.reshape(1, N_G, N_ST))
        p_im.append(pst_i.reshape(1, N_G, N_ST))
        p_buf.append(ptail[SUBLANES - 2:].reshape(1, 2, CONV_W))
        s_re.append(sst_r.reshape(N_SEQ_S, N_G, N_ST))
        s_im.append(sst_i.reshape(N_SEQ_S, N_G, N_ST))
        s_buf.append(zs.reshape(N_SEQ_S, LEN_S, CONV_W)[:, LEN_S - 2:])

    return (y_prompt.reshape(1, T_PROMPT, D_MODEL), y_sample.reshape(N_SEQ_S, LEN_S, D_MODEL),
            jnp.stack(p_re), jnp.stack(p_im), jnp.stack(p_buf),
            jnp.stack(s_re), jnp.stack(s_im), jnp.stack(s_buf))
```

```python
import functools

import jax
import jax.numpy as jnp
from jax import lax
from jax.experimental import pallas as pl
from jax.experimental.pallas import tpu as pltpu

F32 = jnp.float32
BF16 = jnp.bfloat16
I32 = jnp.int32

D_MODEL = 1024
DEPTH = 2
T_PROMPT = 16384
N_SEQ_S = 32
LEN_S = 16
T_SAMPLE = N_SEQ_S * LEN_S
T_ALL = T_PROMPT + T_SAMPLE
SSM_W = 512
CONV_W = 512
N_G = 32
G_CH = 16
N_ST = 64
ST_W = N_G * N_ST
N_EGRP = 4
EPG = 8
N_EXP = 32
D_FF = 512
EPS = 1e-6

LANES = 128
SUBLANES = 8
TB = 512
NB = T_ALL // TB
NBP = T_PROMPT // TB
CHUNK = 8
CRB = TB // CHUNK
GPS = LANES // G_CH
N_SG = SSM_W // LANES
SG_ST = GPS * N_ST
CW = CHUNK * LANES
UVW = CHUNK * SSM_W
CR_P = T_PROMPT // CHUNK
CR_S = T_SAMPLE // CHUNK
CPS = LEN_S // CHUNK
CB = 256
NPB = CR_P // CB
TM = 256
P_CAP = 2 * T_ALL + N_EXP * TM
NT = P_CAP // TM
N_PAIRS = 2 * T_ALL
ISSUE_UNROLL = 8
RT_ROWS = 40

VMEM_LIMIT = 56 * 1024 * 1024

assert T_SAMPLE == TB and TB % CHUNK == 0 and CR_S <= CB and LEN_S % CHUNK == 0


def _rms(x, g):
    return x * lax.rsqrt(jnp.mean(x * x, axis=-1, keepdims=True) + EPS) * g


def _sigmoid(x):
    return 1.0 / (1.0 + jnp.exp(-x))


def _stream_specs(s_block):
    return [pl.BlockSpec((TB, D_MODEL), lambda i: (jnp.minimum(i, NBP - 1), 0)),
            pl.BlockSpec((TB, D_MODEL), lambda i: (s_block, 0))]


def _inproj_kernel(xp_ref, xs_ref, gmix_ref, win_ref, wconv_ref, gb_ref, cz1_ref, cz2_ref,
                   uvp_ref, uvs_ref, mixb_ref, ptail_ref, zs_ref, carry_ref, u_scr):
    i = pl.program_id(0)
    is_sample = i == NB - 1

    @pl.when(i == 0)
    def _():
        carry_ref[...] = jnp.zeros_like(carry_ref)

    x = jnp.where(is_sample, xs_ref[...], xp_ref[...])
    h = _rms(x, gmix_ref[...]).astype(BF16)
    proj = jnp.dot(h, win_ref[...], preferred_element_type=F32)
    for k in range(N_SG):
        u_scr[k] = proj[:, k * LANES:(k + 1) * LANES]
    gate_b = proj[:, SSM_W:SSM_W + CONV_W]
    gate_c = proj[:, SSM_W + CONV_W:SSM_W + 2 * CONV_W]
    v = proj[:, SSM_W + 2 * CONV_W:]
    z = gate_c * v
    row = lax.broadcasted_iota(I32, (TB, 1), 0)
    r1 = pltpu.roll(z, 1, 0)
    r2 = pltpu.roll(z, 2, 0)
    c6 = carry_ref[6:7, :]
    c7 = carry_ref[7:8, :]
    z1p = jnp.where(row == 0, c7, r1)
    z2p = jnp.where(row == 0, c6, jnp.where(row == 1, c7, r2))
    pos = row % LEN_S
    z1s = jnp.where(pos == 0, cz1_ref[...], r1)
    z2s = jnp.where(pos < 2, cz2_ref[...], r2)
    z1 = jnp.where(is_sample, z1s, z1p)
    z2 = jnp.where(is_sample, z2s, z2p)
    w = wconv_ref[...]
    conv = w[0:1, :] * z2 + w[1:2, :] * z1 + w[2:3, :] * z
    yb = gate_b * conv
    mixb_ref[...] = _rms(yb, gb_ref[...]).astype(BF16)
    carry_ref[...] = z[TB - SUBLANES:, :]

    def put_chunk_rows(uv_ref):
        for t in range(CHUNK):
            for k in range(N_SG):
                uv_ref[:, t * SSM_W + k * LANES:t * SSM_W + (k + 1) * LANES] = (
                    u_scr[k, pl.ds(t, CRB, stride=CHUNK), :])

    @pl.when(jnp.logical_not(is_sample))
    def _():
        put_chunk_rows(uvp_ref)

    @pl.when(i == NB - 2)
    def _():
        ptail_ref[...] = z[TB - SUBLANES:, :]

    @pl.when(is_sample)
    def _():
        put_chunk_rows(uvs_ref)
        zs_ref[...] = z


def _inproj_call(xp, xs, s_block, gmix, win, wconv, gb, cz1, cz2):
    const = lambda i: (0, 0)
    return pl.pallas_call(
        _inproj_kernel,
        grid=(NB,),
        in_specs=_stream_specs(s_block) + [
            pl.BlockSpec((1, D_MODEL), const),
            pl.BlockSpec((D_MODEL, 4 * SSM_W), const),
            pl.BlockSpec((3, CONV_W), const),
            pl.BlockSpec((1, CONV_W), const),
            pl.BlockSpec((TB, CONV_W), const),
            pl.BlockSpec((TB, CONV_W), const),
        ],
        out_specs=[
            pl.BlockSpec((CRB, UVW), lambda i: (jnp.minimum(i, NBP - 1), 0)),
            pl.BlockSpec((CRB, UVW), const),
            pl.BlockSpec((TB, CONV_W), lambda i: (i, 0)),
            pl.BlockSpec((SUBLANES, CONV_W), const),
            pl.BlockSpec((TB, CONV_W), const),
        ],
        out_shape=[
            jax.ShapeDtypeStruct((CR_P, UVW), F32),
            jax.ShapeDtypeStruct((CR_S, UVW), F32),
            jax.ShapeDtypeStruct((T_ALL, CONV_W), BF16),
            jax.ShapeDtypeStruct((SUBLANES, CONV_W), F32),
            jax.ShapeDtypeStruct((TB, CONV_W), F32),
        ],
        scratch_shapes=[pltpu.VMEM((SUBLANES, CONV_W), F32), pltpu.VMEM((N_SG, TB, LANES), F32)],
        compiler_params=pltpu.CompilerParams(
            dimension_semantics=("arbitrary",), vmem_limit_bytes=VMEM_LIMIT),
        name="inproj",
    )(xp, xs, gmix, win, wconv, gb, cz1, cz2)


def _expand_chunk_matrices(abd_ref, cac_ref, kc_ref, ws, wy, kb):
    row_g = lax.broadcasted_iota(I32, (LANES, LANES), 0) // G_CH
    lane_half = lax.broadcasted_iota(I32, (LANES, LANES), 1) // N_ST
    col_g = lax.broadcasted_iota(I32, (N_ST, LANES), 1) // G_CH
    zero_blk = jnp.zeros((LANES, LANES), BF16)
    for s in range(N_SG):
        for t in range(CHUNK):
            for r in range(2):
                a = abd_ref[s, t, r]
                for m in range(GPS // 2):
                    ws[s, t * LANES:(t + 1) * LANES,
                       r * SG_ST + m * LANES:r * SG_ST + (m + 1) * LANES] = jnp.where(
                           row_g == 2 * m + lane_half, a, jnp.zeros_like(a))
                c = cac_ref[s, t, r]
                for g in range(GPS):
                    wy[s, r * SG_ST + g * N_ST:r * SG_ST + (g + 1) * N_ST,
                       t * LANES:(t + 1) * LANES] = jnp.where(col_g == g, c, jnp.zeros_like(c))
        for tp in range(CHUNK):
            for t in range(CHUNK):
                kb[s, tp * LANES:(tp + 1) * LANES, t * LANES:(t + 1) * LANES] = (
                    kc_ref[s, t - tp] if t >= tp else zero_blk)


def _ucat(uv_ref, sg):
    parts = [uv_ref[:, t * SSM_W + sg * LANES:t * SSM_W + (sg + 1) * LANES] for t in range(CHUNK)]
    return jnp.concatenate(parts, axis=1).astype(BF16)


def _ssm_local(uv_ref, ws, sre, sim):
    for sg in range(N_SG):
        s = jnp.dot(_ucat(uv_ref, sg), ws[sg], preferred_element_type=F32)
        sre[:, sg * SG_ST:(sg + 1) * SG_ST] = s[:, :SG_ST]
        sim[:, sg * SG_ST:(sg + 1) * SG_ST] = s[:, SG_ST:]


def _ssm_out(uv_ref, wy, kb, sre, sim, yv_ref):
    for sg in range(N_SG):
        sp = jnp.concatenate([sre[:, sg * SG_ST:(sg + 1) * SG_ST],
                              sim[:, sg * SG_ST:(sg + 1) * SG_ST]], axis=1).astype(BF16)
        y = (jnp.dot(sp, wy[sg], preferred_element_type=F32)
             + jnp.dot(_ucat(uv_ref, sg), kb[sg], preferred_element_type=F32))
        for t in range(CHUNK):
            yv_ref[:, t * SSM_W + sg * LANES:t * SSM_W + (sg + 1) * LANES] = (
                y[:, t * LANES:(t + 1) * LANES])


def _ssm_kernel(uvp_ref, uvs_ref, abd_ref, cac_ref, kc_ref, alr_ref, ali_ref, h0r_ref, h0i_ref,
                yvp_ref, yvs_ref, pstr_ref, psti_ref, sstr_ref, ssti_ref,
                ws, wy, kb, sre, sim, cre, cim):
    i = pl.program_id(0)

    @pl.when(i == 0)
    def _():
        cre[...] = jnp.zeros_like(cre)
        cim[...] = jnp.zeros_like(cim)
        _expand_chunk_matrices(abd_ref, cac_ref, kc_ref, ws, wy, kb)

    ar = alr_ref[...]
    ai = ali_ref[...]

    @pl.when(i < NPB)
    def _():
        _ssm_local(uvp_ref, ws, sre, sim)

        def step(c, carry):
            sr, si = carry
            lr = sre[pl.ds(c, 1), :]
            li = sim[pl.ds(c, 1), :]
            sre[pl.ds(c, 1), :] = sr
            sim[pl.ds(c, 1), :] = si
            return ar * sr - ai * si + lr, ar * si + ai * sr + li

        sr, si = lax.fori_loop(0, CB, step, (cre[...], cim[...]))
        cre[...] = sr
        cim[...] = si
        pstr_ref[...] = sr
        psti_ref[...] = si
        _ssm_out(uvp_ref, wy, kb, sre, sim, yvp_ref)

    @pl.when(i == NPB)
    def _():
        sre_s = sre.at[0:CR_S]
        sim_s = sim.at[0:CR_S]
        _ssm_local(uvs_ref, ws, sre_s, sim_s)

        def per_stream(q, _):
            sr = h0r_ref[pl.ds(q, 1), :]
            si = h0i_ref[pl.ds(q, 1), :]
            for k in range(CPS):
                c = q * CPS + k
                lr = sre_s[pl.ds(c, 1), :]
                li = sim_s[pl.ds(c, 1), :]
                sre_s[pl.ds(c, 1), :] = sr
                sim_s[pl.ds(c, 1), :] = si
                sr, si = ar * sr - ai * si + lr, ar * si + ai * sr + li
            sstr_ref[pl.ds(q, 1), :] = sr
            ssti_ref[pl.ds(q, 1), :] = si
            return 0

        lax.fori_loop(0, N_SEQ_S, per_stream, 0)
        _ssm_out(uvs_ref, wy, kb, sre_s, sim_s, yvs_ref)


def _ssm_call(uvp, uvs, abd, cac, kc, alr, ali, h0r, h0i):
    c2 = lambda i: (0, 0)
    c4 = lambda i: (0, 0, 0, 0)
    c5 = lambda i: (0, 0, 0, 0, 0)
    one = pl.Buffered(1)
    pblk = lambda i: (jnp.minimum(i, NPB - 1), 0)
    return pl.pallas_call(
        _ssm_kernel,
        grid=(NPB + 1,),
        in_specs=[pl.BlockSpec((CB, UVW), pblk),
                  pl.BlockSpec((CR_S, UVW), c2),
                  pl.BlockSpec((N_SG, CHUNK, 2, LANES, LANES), c5, pipeline_mode=one),
                  pl.BlockSpec((N_SG, CHUNK, 2, N_ST, LANES), c5, pipeline_mode=one),
                  pl.BlockSpec((N_SG, CHUNK, LANES, LANES), c4, pipeline_mode=one),
                  pl.BlockSpec((1, ST_W), c2),
                  pl.BlockSpec((1, ST_W), c2),
                  pl.BlockSpec((N_SEQ_S, ST_W), c2),
                  pl.BlockSpec((N_SEQ_S, ST_W), c2)],
        out_specs=[pl.BlockSpec((CB, UVW), pblk),
                   pl.BlockSpec((CR_S, UVW), c2),
                   pl.BlockSpec((1, ST_W), c2),
                   pl.BlockSpec((1, ST_W), c2),
                   pl.BlockSpec((N_SEQ_S, ST_W), c2),
                   pl.BlockSpec((N_SEQ_S, ST_W), c2)],
        out_shape=[jax.ShapeDtypeStruct((CR_P, UVW), F32),
                   jax.ShapeDtypeStruct((CR_S, UVW), F32),
                   jax.ShapeDtypeStruct((1, ST_W), F32),
                   jax.ShapeDtypeStruct((1, ST_W), F32),
                   jax.ShapeDtypeStruct((N_SEQ_S, ST_W), F32),
                   jax.ShapeDtypeStruct((N_SEQ_S, ST_W), F32)],
        scratch_shapes=[pltpu.VMEM((N_SG, CW, 2 * SG_ST), BF16),
                        pltpu.VMEM((N_SG, 2 * SG_ST, CW), BF16),
                        pltpu.VMEM((N_SG, CW, CW), BF16),
                        pltpu.VMEM((CB, ST_W), F32), pltpu.VMEM((CB, ST_W), F32),
                        pltpu.VMEM((1, ST_W), F32), pltpu.VMEM((1, ST_W), F32)],
        compiler_params=pltpu.CompilerParams(
            dimension_semantics=("arbitrary",), vmem_limit_bytes=VMEM_LIMIT),
        name="ssm",
    )(uvp, uvs, abd, cac, kc, alr, ali, h0r, h0i)


def _post_kernel(xp_ref, xs_ref, yvp_ref, yvs_ref, mixb_ref, wglu_ref, bglu_ref, ga_ref,
                 wout_ref, gffn_ref, wrt_ref, brt_ref,
                 xmid_ref, ridx_ref, rgate_ref, cnt_ref, base_ref, y_scr):
    i = pl.program_id(0)
    is_sample = i == NB - 1

    @pl.when(i == 0)
    def _():
        base_ref[...] = jnp.zeros_like(base_ref)

    yv = jnp.where(is_sample, yvs_ref[...], yvp_ref[...])
    for t in range(CHUNK):
        for k in range(N_SG):
            y_scr[k, pl.ds(t, CRB, stride=CHUNK), :] = (
                yv[:, t * SSM_W + k * LANES:t * SSM_W + (k + 1) * LANES])
    y = jnp.concatenate([y_scr[k] for k in range(N_SG)], axis=1)
    z = 0.5 * y * (1.0 + jnp.tanh(0.7978845608028654 * (y + 0.044715 * (y * y * y))))
    gl = jnp.dot(z.astype(BF16), wglu_ref[...], preferred_element_type=F32) + bglu_ref[...]
    out_a = z * _sigmoid(gl)
    mix = jnp.concatenate([_rms(out_a, ga_ref[...]).astype(BF16), mixb_ref[...]], axis=1)
    x = jnp.where(is_sample, xs_ref[...], xp_ref[...])
    xm = x + jnp.dot(mix, wout_ref[...], preferred_element_type=F32)
    xmid_ref[...] = xm
    hb = _rms(xm, gffn_ref[...]).astype(BF16)

    logit = lax.dot_general(wrt_ref[...], hb, (((1,), (1,)), ((), ())),
                            preferred_element_type=F32) + brt_ref[...]
    fine = logit[0:N_EXP, :]
    coarse = logit[N_EXP:N_EXP + N_EGRP, :]
    cmax = jnp.max(coarse, axis=0, keepdims=True)
    gi = lax.broadcasted_iota(I32, (N_EGRP, TB), 0).astype(F32)
    grp = jnp.min(jnp.where(coarse == cmax, gi, float(N_EGRP)), axis=0, keepdims=True)
    pg = 1.0 / jnp.sum(jnp.exp(coarse - cmax), axis=0, keepdims=True)
    eidx = lax.broadcasted_iota(I32, (N_EXP, TB), 0)
    ei = eidx.astype(F32)
    egrp = (eidx // EPG).astype(F32)
    neg = jnp.float32(-jnp.inf)
    fm = jnp.where(egrp == grp, fine, neg)
    v0 = jnp.max(fm, axis=0, keepdims=True)
    e0 = jnp.min(jnp.where(fm == v0, ei, float(N_EXP)), axis=0, keepdims=True)
    fm2 = jnp.where(ei == e0, neg, fm)
    v1 = jnp.max(fm2, axis=0, keepdims=True)
    e1 = jnp.min(jnp.where(fm2 == v1, ei, float(N_EXP)), axis=0, keepdims=True)
    tt = jnp.exp(v1 - v0)
    g0 = pg / (1.0 + tt)
    g1 = pg * tt / (1.0 + tt)

    sel0 = ei == e0
    sel1 = ei == e1
    cnt = jnp.where(sel0 | sel1, 1.0, 0.0)
    ta = lax.broadcasted_iota(I32, (TB, TB), 0)
    tb = lax.broadcasted_iota(I32, (TB, TB), 1)
    before = jnp.where(ta < tb, 1.0, 0.0).astype(BF16)
    cum = jnp.dot(cnt.astype(BF16), before, preferred_element_type=F32) + base_ref[...]
    rank0 = jnp.sum(jnp.where(sel0, cum, 0.0), axis=0, keepdims=True)
    rank1 = jnp.sum(jnp.where(sel1, cum, 0.0), axis=0, keepdims=True)
    base_new = base_ref[...] + jnp.sum(cnt, axis=1, keepdims=True)
    base_ref[...] = base_new
    cnt_ref[...] = jnp.broadcast_to(base_new, (N_EXP, LANES)).astype(I32)

    zi = jnp.zeros((1, TB), I32)
    ridx_ref[...] = jnp.concatenate(
        [e0.astype(I32), e1.astype(I32), rank0.astype(I32), rank1.astype(I32), zi, zi, zi, zi],
        axis=0)
    zf = jnp.zeros((1, TB), F32)
    rgate_ref[...] = jnp.concatenate([g0, g1, zf, zf, zf, zf, zf, zf], axis=0)


def _post_call(xp, xs, s_block, yvp, yvs, mixb, wglu, bglu, ga, wout, gffn, wrt, brt):
    const = lambda i: (0, 0)
    return pl.pallas_call(
        _post_kernel,
        grid=(NB,),
        in_specs=_stream_specs(s_block) + [
            pl.BlockSpec((CRB, UVW), lambda i: (jnp.minimum(i, NBP - 1), 0)),
            pl.BlockSpec((CRB, UVW), const),
            pl.BlockSpec((TB, CONV_W), lambda i: (i, 0)),
            pl.BlockSpec((SSM_W, SSM_W), const),
            pl.BlockSpec((1, SSM_W), const),
            pl.BlockSpec((1, SSM_W), const),
            pl.BlockSpec((D_MODEL, D_MODEL), const),
            pl.BlockSpec((1, D_MODEL), const),
            pl.BlockSpec((RT_ROWS, D_MODEL), const),
            pl.BlockSpec((RT_ROWS, 1), const),
        ],
        out_specs=[
            pl.BlockSpec((TB, D_MODEL), lambda i: (i, 0)),
            pl.BlockSpec((SUBLANES, TB), lambda i: (0, i)),
            pl.BlockSpec((SUBLANES, TB), lambda i: (0, i)),
            pl.BlockSpec((N_EXP, LANES), const),
        ],
        out_shape=[
            jax.ShapeDtypeStruct((T_ALL, D_MODEL), F32),
            jax.ShapeDtypeStruct((SUBLANES, T_ALL), I32),
            jax.ShapeDtypeStruct((SUBLANES, T_ALL), F32),
            jax.ShapeDtypeStruct((N_EXP, LANES), I32),
        ],
        scratch_shapes=[pltpu.VMEM((N_EXP, 1), F32), pltpu.VMEM((N_SG, TB, LANES), F32)],
        compiler_params=pltpu.CompilerParams(
            dimension_semantics=("arbitrary",), vmem_limit_bytes=VMEM_LIMIT),
        name="post",
    )(xp, xs, yvp, yvs, mixb, wglu, bglu, ga, wout, gffn, wrt, brt)


def _invert_kernel(cpos_ref, csrc_ref):
    i = pl.program_id(0)

    def put(r, _):
        tok = i * TB + r
        csrc_ref[cpos_ref[0, 0, r]] = tok
        csrc_ref[cpos_ref[0, 1, r]] = tok
        return 0

    lax.fori_loop(0, TB, put, 0, unroll=ISSUE_UNROLL)


def _invert_call(cpos):
    return pl.pallas_call(
        _invert_kernel,
        grid=(NB,),
        in_specs=[pl.BlockSpec((1, 2, TB), lambda i: (i, 0, 0), memory_space=pltpu.SMEM)],
        out_specs=pl.BlockSpec(memory_space=pltpu.SMEM),
        out_shape=jax.ShapeDtypeStruct((N_PAIRS,), I32),
        compiler_params=pltpu.CompilerParams(dimension_semantics=("arbitrary",)),
        name="invert",
    )(cpos)


def _expert_kernel(te_ref, nu_ref, ts_ref, csrc_ref, xmid_ref, gffn_ref, wg_ref, wu_ref, wd_ref,
                   r_ref, wgb, wub, wdb, hbuf, xbuf, sems):
    i = pl.program_id(0)
    n_used = nu_ref[0]
    prev = te_ref[jnp.maximum(i - 1, 0)]
    fresh = (i == 0) | (te_ref[i] != prev)

    @pl.when(fresh)
    def _():
        wgb[...] = wg_ref[...].astype(BF16)
        wub[...] = wu_ref[...].astype(BF16)
        wdb[...] = wd_ref[...].astype(BF16)

    def gather_rows(tile, slot):
        base = ts_ref[tile]
        for r in range(TM):
            tok = csrc_ref[jnp.minimum(base + r, N_PAIRS - 1)]
            pltpu.make_async_copy(xmid_ref.at[pl.ds(tok, 1)], xbuf.at[slot, pl.ds(r, 1)],
                                  sems.at[slot]).start()

    def wait_rows(slot):
        pltpu.make_async_copy(xbuf.at[slot], xbuf.at[slot], sems.at[slot]).wait()

    @pl.when(i == 0)
    def _():
        gather_rows(0, 0)

    used = i < n_used
    slot = i % 2

    @pl.when(used)
    def _():
        wait_rows(slot)
        hbuf[...] = _rms(xbuf[slot], gffn_ref[...]).astype(BF16)
        gather_rows(jnp.minimum(i + 1, n_used - 1), 1 - slot)
        hb = hbuf[...]
        g = jnp.dot(hb, wgb[...], preferred_element_type=F32)
        u = jnp.dot(hb, wub[...], preferred_element_type=F32)
        hid = (g * _sigmoid(g) * u).astype(BF16)
        r_ref[...] = jnp.dot(hid, wdb[...], preferred_element_type=F32)

    @pl.when(i == n_used - 1)
    def _():
        wait_rows(1 - slot)

    @pl.when(jnp.logical_not(used))
    def _():
        r_ref[...] = jnp.zeros_like(r_ref)


def _expert_call(layer, te, nu, ts, csrc, xmid, gffn, w_gate, w_up, w_down):
    def wmap(i, te, nu, ts, csrc):
        return (layer, te[i], 0, 0)

    return pl.pallas_call(
        _expert_kernel,
        grid_spec=pltpu.PrefetchScalarGridSpec(
            num_scalar_prefetch=4,
            grid=(NT,),
            in_specs=[
                pl.BlockSpec(memory_space=pl.ANY),
                pl.BlockSpec((1, D_MODEL), lambda i, *_: (0, 0)),
                pl.BlockSpec((None, None, D_MODEL, D_FF), wmap),
                pl.BlockSpec((None, None, D_MODEL, D_FF), wmap),
                pl.BlockSpec((None, None, D_FF, D_MODEL), wmap),
            ],
            out_specs=pl.BlockSpec((TM, D_MODEL), lambda i, *_: (i, 0)),
            scratch_shapes=[pltpu.VMEM((D_MODEL, D_FF), BF16),
                            pltpu.VMEM((D_MODEL, D_FF), BF16),
                            pltpu.VMEM((D_FF, D_MODEL), BF16),
                            pltpu.VMEM((TM, D_MODEL), BF16),
                            pltpu.VMEM((2, TM, D_MODEL), F32),
                            pltpu.SemaphoreType.DMA((2,))],
        ),
        out_shape=jax.ShapeDtypeStruct((P_CAP, D_MODEL), F32),
        compiler_params=pltpu.CompilerParams(
            dimension_semantics=("arbitrary",), vmem_limit_bytes=VMEM_LIMIT),
        name="experts",
    )(te, nu, ts, csrc, xmid, gffn, w_gate, w_up, w_down)


def _combine_kernel(pos_ref, xmid_ref, gate_ref, gfin_ref, r_ref, *rest, final):
    if final:
        outp_ref, outs_ref, buf, sem = rest
    else:
        out_ref, buf, sem = rest
    i = pl.program_id(0)

    def issue(r, _):
        for k in range(2):
            p = pos_ref[0, k, r]
            pltpu.make_async_copy(r_ref.at[pl.ds(p, 1)], buf.at[k, pl.ds(r, 1)], sem).start()
        return 0

    lax.fori_loop(0, TB, issue, 0, unroll=ISSUE_UNROLL)
    for k in range(2):
        pltpu.make_async_copy(buf.at[k], buf.at[k], sem).wait()
    gt = gate_ref[...]
    out = xmid_ref[...] + gt[:, 0:1] * buf[0] + gt[:, 1:2] * buf[1]
    if not final:
        out_ref[...] = out
    else:
        out = _rms(out, gfin_ref[...])

        @pl.when(i < NB - 1)
        def _():
            outp_ref[...] = out

        @pl.when(i == NB - 1)
        def _():
            outs_ref[...] = out


def _combine_call(pos, xmid, gate_t, gfin, r, final):
    if final:
        out_specs = [pl.BlockSpec((TB, D_MODEL), lambda i: (jnp.minimum(i, NBP - 1), 0)),
                     pl.BlockSpec((TB, D_MODEL), lambda i: (0, 0))]
        out_shape = [jax.ShapeDtypeStruct((T_PROMPT, D_MODEL), F32),
                     jax.ShapeDtypeStruct((T_SAMPLE, D_MODEL), F32)]
    else:
        out_specs = pl.BlockSpec((TB, D_MODEL), lambda i: (i, 0))
        out_shape = jax.ShapeDtypeStruct((T_ALL, D_MODEL), F32)
    return pl.pallas_call(
        functools.partial(_combine_kernel, final=final),
        grid=(NB,),
        in_specs=[
            pl.BlockSpec((1, 2, TB), lambda i: (i, 0, 0), memory_space=pltpu.SMEM),
            pl.BlockSpec((TB, D_MODEL), lambda i: (i, 0)),
            pl.BlockSpec((TB, 2), lambda i: (i, 0)),
            pl.BlockSpec((1, D_MODEL), lambda i: (0, 0)),
            pl.BlockSpec(memory_space=pl.ANY),
        ],
        out_specs=out_specs,
        out_shape=out_shape,
        scratch_shapes=[pltpu.VMEM((2, TB, D_MODEL), F32), pltpu.SemaphoreType.DMA(())],
        compiler_params=pltpu.CompilerParams(
            dimension_semantics=("arbitrary",), vmem_limit_bytes=VMEM_LIMIT),
        name="combine",
    )(pos, xmid, gate_t, gfin, r)


def _ssm_compact(lam_re, lam_im, log_dt, b_re, b_im, c_re, c_im, d_skip):
    nl = lam_re.shape[0]
    dt = jnp.exp(log_dt)[..., None]
    kpow = jnp.arange(CHUNK + 1, dtype=F32).reshape(CHUNK + 1, 1, 1, 1)
    mag = jnp.exp(kpow * (lam_re * dt))
    pw_re = mag * jnp.cos(kpow * (lam_im * dt))
    pw_im = mag * jnp.sin(kpow * (lam_im * dt))
    ab_re, ab_im = pw_re[1], pw_im[1]
    denom = lam_re * lam_re + lam_im * lam_im
    num_re = ab_re - 1.0
    f_re = (num_re * lam_re + ab_im * lam_im) / denom
    f_im = (ab_im * lam_re - num_re * lam_im) / denom
    bb_re = f_re[..., None] * b_re - f_im[..., None] * b_im
    bb_im = f_re[..., None] * b_im + f_im[..., None] * b_re

    krev = (CHUNK - 1.0) - kpow[:CHUNK]
    rmag = jnp.exp(krev * (lam_re * dt))
    rev_re = (rmag * jnp.cos(krev * (lam_im * dt)))[..., None]
    rev_im = (rmag * jnp.sin(krev * (lam_im * dt)))[..., None]
    ab = jnp.stack([rev_re * bb_re - rev_im * bb_im, rev_re * bb_im + rev_im * bb_re])
    ab = ab.reshape(2, CHUNK, nl, N_SG, GPS, N_ST, G_CH)
    ab = ab.transpose(2, 3, 1, 0, 4, 6, 5).reshape(nl, N_SG, CHUNK, 2, LANES, N_ST)
    abd = jnp.concatenate([ab, ab], axis=-1).astype(BF16)

    pr = pw_re[:, :, :, None, :]
    pi = pw_im[:, :, :, None, :]
    ca_re = c_re * pr - c_im * pi
    ca_im = c_re * pi + c_im * pr
    ca = jnp.stack([ca_re[1:], -ca_im[1:]])
    ca = ca.reshape(2, CHUNK, nl, N_SG, GPS, G_CH, N_ST)
    cac = ca.transpose(2, 3, 1, 0, 6, 4, 5).reshape(nl, N_SG, CHUNK, 2, N_ST, LANES).astype(BF16)

    bq_re = jnp.swapaxes(bb_re, -1, -2)[:, :, None]
    bq_im = jnp.swapaxes(bb_im, -1, -2)[:, :, None]
    kk = jnp.sum(ca_re[:CHUNK, :, :, :, None, :] * bq_re
                 - ca_im[:CHUNK, :, :, :, None, :] * bq_im, axis=-1)
    skip = d_skip.reshape(nl, N_G, G_CH)[..., None] * jnp.eye(G_CH, dtype=F32)
    kk = jnp.concatenate([kk[:1] + skip[None], kk[1:]], axis=0)
    kk = kk.reshape(CHUNK, nl, N_SG, GPS, G_CH, G_CH)
    kk = kk.transpose(1, 2, 0, 5, 3, 4)
    eye = jnp.eye(GPS, dtype=F32).reshape(GPS, 1, GPS, 1)
    kc = (kk[:, :, :, None] * eye).reshape(nl, N_SG, CHUNK, LANES, LANES).astype(BF16)

    al_re = pw_re[CHUNK].reshape(nl, 1, ST_W)
    al_im = pw_im[CHUNK].reshape(nl, 1, ST_W)
    return abd, cac, kc, al_re, al_im


def _tile_tables(counts):
    padded = ((counts + TM - 1) // TM) * TM
    ends = jnp.cumsum(padded)
    offs = ends - padded
    end_tile = ends // TM
    n_used = end_tile[-1].astype(I32)
    tile = jnp.arange(NT, dtype=I32)
    live = jnp.minimum(tile, n_used - 1)
    te = jnp.sum(end_tile[None, :] <= live[:, None], axis=1).astype(I32)
    cstart = jnp.cumsum(counts) - counts
    ts = (cstart[te] + (live - (offs // TM)[te]) * TM).astype(I32)
    return offs.astype(I32), cstart.astype(I32), te, n_used.reshape(1), ts


def kernel(x_prompt, x_sample, state_ssm_re, state_ssm_im, cache_conv, norm_mix, w_in, lam_re,
           lam_im, log_dt, ssm_b_re, ssm_b_im, ssm_c_re, ssm_c_im, ssm_d, w_glu, b_glu, w_conv,
           norm_a, norm_b, w_out, norm_ffn, w_router_group, b_router_group, w_router_expert,
           b_router_expert, w_gate, w_up, w_down, norm_final):
    xp = x_prompt.reshape(T_PROMPT, D_MODEL)
    xs = x_sample.reshape(T_SAMPLE, D_MODEL)
    s_block = 0
    gfin = norm_final.reshape(1, D_MODEL)
    abd, cac, kc, al_re, al_im = _ssm_compact(
        lam_re, lam_im, log_dt, ssm_b_re, ssm_b_im, ssm_c_re, ssm_c_im, ssm_d)
    p_re, p_im, p_buf, s_re, s_im, s_buf = [], [], [], [], [], []
    for l in range(DEPTH):
        cache = cache_conv[l]
        cz1 = jnp.pad(cache[:, 1:2], ((0, 0), (0, LEN_S - 1), (0, 0))).reshape(T_SAMPLE, CONV_W)
        cz2 = jnp.pad(cache, ((0, 0), (0, LEN_S - 2), (0, 0))).reshape(T_SAMPLE, CONV_W)
        uvp, uvs, mixb, ptail, zs = _inproj_call(
            xp, xs, s_block, norm_mix[l].reshape(1, D_MODEL), w_in[l].astype(BF16), w_conv[l],
            norm_b[l].reshape(1, CONV_W), cz1, cz2)

        yvp, yvs, pst_r, pst_i, sst_r, sst_i = _ssm_call(
            uvp, uvs, abd[l], cac[l], kc[l], al_re[l], al_im[l],
            state_ssm_re[l].reshape(N_SEQ_S, ST_W), state_ssm_im[l].reshape(N_SEQ_S, ST_W))

        wrt = jnp.concatenate([
            w_router_expert[l].transpose(0, 2, 1).reshape(N_EXP, D_MODEL),
            w_router_group[l].T,
            jnp.zeros((RT_ROWS - N_EXP - N_EGRP, D_MODEL), F32)], axis=0).astype(BF16)
        brt = jnp.concatenate([b_router_expert[l].reshape(N_EXP), b_router_group[l],
                               jnp.zeros((RT_ROWS - N_EXP - N_EGRP,), F32)]).reshape(RT_ROWS, 1)
        gffn = norm_ffn[l].reshape(1, D_MODEL)
        xmid, ridx, rgate, cnt = _post_call(
            xp, xs, s_block, yvp, yvs, mixb, w_glu[l].astype(BF16), b_glu[l].reshape(1, SSM_W),
            norm_a[l].reshape(1, SSM_W), w_out[l].astype(BF16), gffn, wrt, brt)

        offs, cstart, te, n_used, ts = _tile_tables(cnt[:, 0])
        pos = jnp.stack([offs[ridx[0]] + ridx[2], offs[ridx[1]] + ridx[3]])
        pos = pos.reshape(2, NB, TB).transpose(1, 0, 2)
        cpos = jnp.stack([cstart[ridx[0]] + ridx[2], cstart[ridx[1]] + ridx[3]])
        csrc = _invert_call(cpos.reshape(2, NB, TB).transpose(1, 0, 2))
        r = _expert_call(l, te, n_used, ts, csrc, xmid, gffn, w_gate, w_up, w_down)
        final = l == DEPTH - 1
        out = _combine_call(pos, xmid, rgate[:2].T, gfin, r, final=final)
        if final:
            y_prompt, y_sample = out
        else:
            xp = xs = out
            s_block = NB - 1

        p_re.append(pst_r.reshape(1, N_G, N_ST))
        p_im.append(pst_i.reshape(1, N_G, N_ST))
        p_buf.append(ptail[SUBLANES - 2:].reshape(1, 2, CONV_W))
        s_re.append(sst_r.reshape(N_SEQ_S, N_G, N_ST))
        s_im.append(sst_i.reshape(N_SEQ_S, N_G, N_ST))
        s_buf.append(zs.reshape(N_SEQ_S, LEN_S, CONV_W)[:, LEN_S - 2:])

    return (y_prompt.reshape(1, T_PROMPT, D_MODEL), y_sample.reshape(N_SEQ_S, LEN_S, D_MODEL),
            jnp.stack(p_re), jnp.stack(p_im), jnp.stack(p_buf),
            jnp.stack(s_re), jnp.stack(s_im), jnp.stack(s_buf))
```

```python
import functools

import jax
import jax.numpy as jnp
from jax import lax
from jax.experimental import pallas as pl
from jax.experimental.pallas import tpu as pltpu

F32 = jnp.float32
BF16 = jnp.bfloat16
I32 = jnp.int32

D_MODEL = 1024
DEPTH = 2
T_PROMPT = 16384
N_SEQ_S = 32
LEN_S = 16
T_SAMPLE = N_SEQ_S * LEN_S
T_ALL = T_PROMPT + T_SAMPLE
SSM_W = 512
CONV_W = 512
N_G = 32
G_CH = 16
N_ST = 64
ST_W = N_G * N_ST
N_EGRP = 4
EPG = 8
N_EXP = 32
D_FF = 512
EPS = 1e-6

LANES = 128
SUBLANES = 8
TB = 512
NB = T_ALL // TB
NBP = T_PROMPT // TB
CHUNK = 8
CRB = TB // CHUNK
GPS = LANES // G_CH
N_SG = SSM_W // LANES
SG_ST = GPS * N_ST
CW = CHUNK * LANES
UVW = CHUNK * SSM_W
CR_P = T_PROMPT // CHUNK
CR_S = T_SAMPLE // CHUNK
CPS = LEN_S // CHUNK
CB = 256
NPB = CR_P // CB
TM = 256
P_CAP = 2 * T_ALL + N_EXP * TM
NT = P_CAP // TM
N_PAIRS = 2 * T_ALL
N_DMA_THREADS = 2
ISSUE_UNROLL = 8
RT_ROWS = 40

VMEM_LIMIT = 56 * 1024 * 1024

assert T_SAMPLE == TB and TB % CHUNK == 0 and CR_S <= CB and LEN_S % CHUNK == 0


def _rms(x, g):
    return x * lax.rsqrt(jnp.mean(x * x, axis=-1, keepdims=True) + EPS) * g


def _sigmoid(x):
    return 1.0 / (1.0 + jnp.exp(-x))


def _stream_specs(s_block):
    return [pl.BlockSpec((TB, D_MODEL), lambda i: (jnp.minimum(i, NBP - 1), 0)),
            pl.BlockSpec((TB, D_MODEL), lambda i: (s_block, 0))]


def _inproj_kernel(xp_ref, xs_ref, gmix_ref, win_ref, wconv_ref, gb_ref, cz1_ref, cz2_ref,
                   uvp_ref, uvs_ref, mixb_ref, ptail_ref, zs_ref, carry_ref, u_scr):
    i = pl.program_id(0)
    is_sample = i == NB - 1

    @pl.when(i == 0)
    def _():
        carry_ref[...] = jnp.zeros_like(carry_ref)

    x = jnp.where(is_sample, xs_ref[...], xp_ref[...])
    h = _rms(x, gmix_ref[...]).astype(BF16)
    proj = jnp.dot(h, win_ref[...], preferred_element_type=F32)
    for k in range(N_SG):
        u_scr[k] = proj[:, k * LANES:(k + 1) * LANES]
    gate_b = proj[:, SSM_W:SSM_W + CONV_W]
    gate_c = proj[:, SSM_W + CONV_W:SSM_W + 2 * CONV_W]
    v = proj[:, SSM_W + 2 * CONV_W:]
    z = gate_c * v
    row = lax.broadcasted_iota(I32, (TB, 1), 0)
    r1 = pltpu.roll(z, 1, 0)
    r2 = pltpu.roll(z, 2, 0)
    c6 = carry_ref[6:7, :]
    c7 = carry_ref[7:8, :]
    z1p = jnp.where(row == 0, c7, r1)
    z2p = jnp.where(row == 0, c6, jnp.where(row == 1, c7, r2))
    pos = row % LEN_S
    z1s = jnp.where(pos == 0, cz1_ref[...], r1)
    z2s = jnp.where(pos < 2, cz2_ref[...], r2)
    z1 = jnp.where(is_sample, z1s, z1p)
    z2 = jnp.where(is_sample, z2s, z2p)
    w = wconv_ref[...]
    conv = w[0:1, :] * z2 + w[1:2, :] * z1 + w[2:3, :] * z
    yb = gate_b * conv
    mixb_ref[...] = _rms(yb, gb_ref[...]).astype(BF16)
    carry_ref[...] = z[TB - SUBLANES:, :]

    def put_chunk_rows(uv_ref):
        for t in range(CHUNK):
            for k in range(N_SG):
                uv_ref[:, t * SSM_W + k * LANES:t * SSM_W + (k + 1) * LANES] = (
                    u_scr[k, pl.ds(t, CRB, stride=CHUNK), :])

    @pl.when(jnp.logical_not(is_sample))
    def _():
        put_chunk_rows(uvp_ref)

    @pl.when(i == NB - 2)
    def _():
        ptail_ref[...] = z[TB - SUBLANES:, :]

    @pl.when(is_sample)
    def _():
        put_chunk_rows(uvs_ref)
        zs_ref[...] = z


def _inproj_call(xp, xs, s_block, gmix, win, wconv, gb, cz1, cz2):
    const = lambda i: (0, 0)
    return pl.pallas_call(
        _inproj_kernel,
        grid=(NB,),
        in_specs=_stream_specs(s_block) + [
            pl.BlockSpec((1, D_MODEL), const),
            pl.BlockSpec((D_MODEL, 4 * SSM_W), const),
            pl.BlockSpec((3, CONV_W), const),
            pl.BlockSpec((1, CONV_W), const),
            pl.BlockSpec((TB, CONV_W), const),
            pl.BlockSpec((TB, CONV_W), const),
        ],
        out_specs=[
            pl.BlockSpec((CRB, UVW), lambda i: (jnp.minimum(i, NBP - 1), 0)),
            pl.BlockSpec((CRB, UVW), const),
            pl.BlockSpec((TB, CONV_W), lambda i: (i, 0)),
            pl.BlockSpec((SUBLANES, CONV_W), const),
            pl.BlockSpec((TB, CONV_W), const),
        ],
        out_shape=[
            jax.ShapeDtypeStruct((CR_P, UVW), F32),
            jax.ShapeDtypeStruct((CR_S, UVW), F32),
            jax.ShapeDtypeStruct((T_ALL, CONV_W), BF16),
            jax.ShapeDtypeStruct((SUBLANES, CONV_W), F32),
            jax.ShapeDtypeStruct((TB, CONV_W), F32),
        ],
        scratch_shapes=[pltpu.VMEM((SUBLANES, CONV_W), F32), pltpu.VMEM((N_SG, TB, LANES), F32)],
        compiler_params=pltpu.CompilerParams(
            dimension_semantics=("arbitrary",), vmem_limit_bytes=VMEM_LIMIT),
        name="inproj",
    )(xp, xs, gmix, win, wconv, gb, cz1, cz2)


def _expand_chunk_matrices(abd_ref, cac_ref, kc_ref, ws, wy, kb):
    row_g = lax.broadcasted_iota(I32, (LANES, LANES), 0) // G_CH
    lane_half = lax.broadcasted_iota(I32, (LANES, LANES), 1) // N_ST
    col_g = lax.broadcasted_iota(I32, (N_ST, LANES), 1) // G_CH
    zero_blk = jnp.zeros((LANES, LANES), BF16)
    for s in range(N_SG):
        for t in range(CHUNK):
            for r in range(2):
                a = abd_ref[s, t, r]
                for m in range(GPS // 2):
                    ws[s, t * LANES:(t + 1) * LANES,
                       r * SG_ST + m * LANES:r * SG_ST + (m + 1) * LANES] = jnp.where(
                           row_g == 2 * m + lane_half, a, jnp.zeros_like(a))
                c = cac_ref[s, t, r]
                for g in range(GPS):
                    wy[s, r * SG_ST + g * N_ST:r * SG_ST + (g + 1) * N_ST,
                       t * LANES:(t + 1) * LANES] = jnp.where(col_g == g, c, jnp.zeros_like(c))
        for tp in range(CHUNK):
            for t in range(CHUNK):
                kb[s, tp * LANES:(tp + 1) * LANES, t * LANES:(t + 1) * LANES] = (
                    kc_ref[s, t - tp] if t >= tp else zero_blk)


def _ucat(uv_ref, sg):
    parts = [uv_ref[:, t * SSM_W + sg * LANES:t * SSM_W + (sg + 1) * LANES] for t in range(CHUNK)]
    return jnp.concatenate(parts, axis=1).astype(BF16)


def _ssm_local(uv_ref, ws, sre, sim):
    for sg in range(N_SG):
        s = jnp.dot(_ucat(uv_ref, sg), ws[sg], preferred_element_type=F32)
        sre[:, sg * SG_ST:(sg + 1) * SG_ST] = s[:, :SG_ST]
        sim[:, sg * SG_ST:(sg + 1) * SG_ST] = s[:, SG_ST:]


def _ssm_out(uv_ref, wy, kb, sre, sim, yv_ref):
    for sg in range(N_SG):
        sp = jnp.concatenate([sre[:, sg * SG_ST:(sg + 1) * SG_ST],
                              sim[:, sg * SG_ST:(sg + 1) * SG_ST]], axis=1).astype(BF16)
        y = (jnp.dot(sp, wy[sg], preferred_element_type=F32)
             + jnp.dot(_ucat(uv_ref, sg), kb[sg], preferred_element_type=F32))
        for t in range(CHUNK):
            yv_ref[:, t * SSM_W + sg * LANES:t * SSM_W + (sg + 1) * LANES] = (
                y[:, t * LANES:(t + 1) * LANES])


def _ssm_kernel(uvp_ref, uvs_ref, abd_ref, cac_ref, kc_ref, alr_ref, ali_ref, h0r_ref, h0i_ref,
                yvp_ref, yvs_ref, pstr_ref, psti_ref, sstr_ref, ssti_ref,
                ws, wy, kb, sre, sim, cre, cim):
    i = pl.program_id(0)

    @pl.when(i == 0)
    def _():
        cre[...] = jnp.zeros_like(cre)
        cim[...] = jnp.zeros_like(cim)
        _expand_chunk_matrices(abd_ref, cac_ref, kc_ref, ws, wy, kb)

    ar = alr_ref[...]
    ai = ali_ref[...]

    @pl.when(i < NPB)
    def _():
        _ssm_local(uvp_ref, ws, sre, sim)

        def step(c, carry):
            sr, si = carry
            lr = sre[pl.ds(c, 1), :]
            li = sim[pl.ds(c, 1), :]
            sre[pl.ds(c, 1), :] = sr
            sim[pl.ds(c, 1), :] = si
            return ar * sr - ai * si + lr, ar * si + ai * sr + li

        sr, si = lax.fori_loop(0, CB, step, (cre[...], cim[...]))
        cre[...] = sr
        cim[...] = si
        pstr_ref[...] = sr
        psti_ref[...] = si
        _ssm_out(uvp_ref, wy, kb, sre, sim, yvp_ref)

    @pl.when(i == NPB)
    def _():
        sre_s = sre.at[0:CR_S]
        sim_s = sim.at[0:CR_S]
        _ssm_local(uvs_ref, ws, sre_s, sim_s)

        def per_stream(q, _):
            sr = h0r_ref[pl.ds(q, 1), :]
            si = h0i_ref[pl.ds(q, 1), :]
            for k in range(CPS):
                c = q * CPS + k
                lr = sre_s[pl.ds(c, 1), :]
                li = sim_s[pl.ds(c, 1), :]
                sre_s[pl.ds(c, 1), :] = sr
                sim_s[pl.ds(c, 1), :] = si
                sr, si = ar * sr - ai * si + lr, ar * si + ai * sr + li
            sstr_ref[pl.ds(q, 1), :] = sr
            ssti_ref[pl.ds(q, 1), :] = si
            return 0

        lax.fori_loop(0, N_SEQ_S, per_stream, 0)
        _ssm_out(uvs_ref, wy, kb, sre_s, sim_s, yvs_ref)


def _ssm_call(uvp, uvs, abd, cac, kc, alr, ali, h0r, h0i):
    c2 = lambda i: (0, 0)
    c4 = lambda i: (0, 0, 0, 0)
    c5 = lambda i: (0, 0, 0, 0, 0)
    one = pl.Buffered(1)
    pblk = lambda i: (jnp.minimum(i, NPB - 1), 0)
    return pl.pallas_call(
        _ssm_kernel,
        grid=(NPB + 1,),
        in_specs=[pl.BlockSpec((CB, UVW), pblk),
                  pl.BlockSpec((CR_S, UVW), c2),
                  pl.BlockSpec((N_SG, CHUNK, 2, LANES, LANES), c5, pipeline_mode=one),
                  pl.BlockSpec((N_SG, CHUNK, 2, N_ST, LANES), c5, pipeline_mode=one),
                  pl.BlockSpec((N_SG, CHUNK, LANES, LANES), c4, pipeline_mode=one),
                  pl.BlockSpec((1, ST_W), c2),
                  pl.BlockSpec((1, ST_W), c2),
                  pl.BlockSpec((N_SEQ_S, ST_W), c2),
                  pl.BlockSpec((N_SEQ_S, ST_W), c2)],
        out_specs=[pl.BlockSpec((CB, UVW), pblk),
                   pl.BlockSpec((CR_S, UVW), c2),
                   pl.BlockSpec((1, ST_W), c2),
                   pl.BlockSpec((1, ST_W), c2),
                   pl.BlockSpec((N_SEQ_S, ST_W), c2),
                   pl.BlockSpec((N_SEQ_S, ST_W), c2)],
        out_shape=[jax.ShapeDtypeStruct((CR_P, UVW), F32),
                   jax.ShapeDtypeStruct((CR_S, UVW), F32),
                   jax.ShapeDtypeStruct((1, ST_W), F32),
                   jax.ShapeDtypeStruct((1, ST_W), F32),
                   jax.ShapeDtypeStruct((N_SEQ_S, ST_W), F32),
                   jax.ShapeDtypeStruct((N_SEQ_S, ST_W), F32)],
        scratch_shapes=[pltpu.VMEM((N_SG, CW, 2 * SG_ST), BF16),
                        pltpu.VMEM((N_SG, 2 * SG_ST, CW), BF16),
                        pltpu.VMEM((N_SG, CW, CW), BF16),
                        pltpu.VMEM((CB, ST_W), F32), pltpu.VMEM((CB, ST_W), F32),
                        pltpu.VMEM((1, ST_W), F32), pltpu.VMEM((1, ST_W), F32)],
        compiler_params=pltpu.CompilerParams(
            dimension_semantics=("arbitrary",), vmem_limit_bytes=VMEM_LIMIT),
        name="ssm",
    )(uvp, uvs, abd, cac, kc, alr, ali, h0r, h0i)


def _post_kernel(xp_ref, xs_ref, yvp_ref, yvs_ref, mixb_ref, wglu_ref, bglu_ref, ga_ref,
                 wout_ref, gffn_ref, wrt_ref, brt_ref,
                 xmid_ref, ridx_ref, rgate_ref, cnt_ref, base_ref, y_scr):
    i = pl.program_id(0)
    is_sample = i == NB - 1

    @pl.when(i == 0)
    def _():
        base_ref[...] = jnp.zeros_like(base_ref)

    yv = jnp.where(is_sample, yvs_ref[...], yvp_ref[...])
    for t in range(CHUNK):
        for k in range(N_SG):
            y_scr[k, pl.ds(t, CRB, stride=CHUNK), :] = (
                yv[:, t * SSM_W + k * LANES:t * SSM_W + (k + 1) * LANES])
    y = jnp.concatenate([y_scr[k] for k in range(N_SG)], axis=1)
    z = 0.5 * y * (1.0 + jnp.tanh(0.7978845608028654 * (y + 0.044715 * (y * y * y))))
    gl = jnp.dot(z.astype(BF16), wglu_ref[...], preferred_element_type=F32) + bglu_ref[...]
    out_a = z * _sigmoid(gl)
    mix = jnp.concatenate([_rms(out_a, ga_ref[...]).astype(BF16), mixb_ref[...]], axis=1)
    x = jnp.where(is_sample, xs_ref[...], xp_ref[...])
    xm = x + jnp.dot(mix, wout_ref[...], preferred_element_type=F32)
    xmid_ref[...] = xm
    hb = _rms(xm, gffn_ref[...]).astype(BF16)

    logit = lax.dot_general(wrt_ref[...], hb, (((1,), (1,)), ((), ())),
                            preferred_element_type=F32) + brt_ref[...]
    fine = logit[0:N_EXP, :]
    coarse = logit[N_EXP:N_EXP + N_EGRP, :]
    cmax = jnp.max(coarse, axis=0, keepdims=True)
    gi = lax.broadcasted_iota(I32, (N_EGRP, TB), 0).astype(F32)
    grp = jnp.min(jnp.where(coarse == cmax, gi, float(N_EGRP)), axis=0, keepdims=True)
    pg = 1.0 / jnp.sum(jnp.exp(coarse - cmax), axis=0, keepdims=True)
    eidx = lax.broadcasted_iota(I32, (N_EXP, TB), 0)
    ei = eidx.astype(F32)
    egrp = (eidx // EPG).astype(F32)
    neg = jnp.float32(-jnp.inf)
    fm = jnp.where(egrp == grp, fine, neg)
    v0 = jnp.max(fm, axis=0, keepdims=True)
    e0 = jnp.min(jnp.where(fm == v0, ei, float(N_EXP)), axis=0, keepdims=True)
    fm2 = jnp.where(ei == e0, neg, fm)
    v1 = jnp.max(fm2, axis=0, keepdims=True)
    e1 = jnp.min(jnp.where(fm2 == v1, ei, float(N_EXP)), axis=0, keepdims=True)
    tt = jnp.exp(v1 - v0)
    g0 = pg / (1.0 + tt)
    g1 = pg * tt / (1.0 + tt)

    sel0 = ei == e0
    sel1 = ei == e1
    cnt = jnp.where(sel0 | sel1, 1.0, 0.0)
    ta = lax.broadcasted_iota(I32, (TB, TB), 0)
    tb = lax.broadcasted_iota(I32, (TB, TB), 1)
    before = jnp.where(ta < tb, 1.0, 0.0).astype(BF16)
    cum = jnp.dot(cnt.astype(BF16), before, preferred_element_type=F32) + base_ref[...]
    rank0 = jnp.sum(jnp.where(sel0, cum, 0.0), axis=0, keepdims=True)
    rank1 = jnp.sum(jnp.where(sel1, cum, 0.0), axis=0, keepdims=True)
    base_new = base_ref[...] + jnp.sum(cnt, axis=1, keepdims=True)
    base_ref[...] = base_new
    cnt_ref[...] = jnp.broadcast_to(base_new, (N_EXP, LANES)).astype(I32)

    zi = jnp.zeros((1, TB), I32)
    ridx_ref[...] = jnp.concatenate(
        [e0.astype(I32), e1.astype(I32), rank0.astype(I32), rank1.astype(I32), zi, zi, zi, zi],
        axis=0)
    zf = jnp.zeros((1, TB), F32)
    rgate_ref[...] = jnp.concatenate([g0, g1, zf, zf, zf, zf, zf, zf], axis=0)


def _post_call(xp, xs, s_block, yvp, yvs, mixb, wglu, bglu, ga, wout, gffn, wrt, brt):
    const = lambda i: (0, 0)
    return pl.pallas_call(
        _post_kernel,
        grid=(NB,),
        in_specs=_stream_specs(s_block) + [
            pl.BlockSpec((CRB, UVW), lambda i: (jnp.minimum(i, NBP - 1), 0)),
            pl.BlockSpec((CRB, UVW), const),
            pl.BlockSpec((TB, CONV_W), lambda i: (i, 0)),
            pl.BlockSpec((SSM_W, SSM_W), const),
            pl.BlockSpec((1, SSM_W), const),
            pl.BlockSpec((1, SSM_W), const),
            pl.BlockSpec((D_MODEL, D_MODEL), const),
            pl.BlockSpec((1, D_MODEL), const),
            pl.BlockSpec((RT_ROWS, D_MODEL), const),
            pl.BlockSpec((RT_ROWS, 1), const),
        ],
        out_specs=[
            pl.BlockSpec((TB, D_MODEL), lambda i: (i, 0)),
            pl.BlockSpec((SUBLANES, TB), lambda i: (0, i)),
            pl.BlockSpec((SUBLANES, TB), lambda i: (0, i)),
            pl.BlockSpec((N_EXP, LANES), const),
        ],
        out_shape=[
            jax.ShapeDtypeStruct((T_ALL, D_MODEL), F32),
            jax.ShapeDtypeStruct((SUBLANES, T_ALL), I32),
            jax.ShapeDtypeStruct((SUBLANES, T_ALL), F32),
            jax.ShapeDtypeStruct((N_EXP, LANES), I32),
        ],
        scratch_shapes=[pltpu.VMEM((N_EXP, 1), F32), pltpu.VMEM((N_SG, TB, LANES), F32)],
        compiler_params=pltpu.CompilerParams(
            dimension_semantics=("arbitrary",), vmem_limit_bytes=VMEM_LIMIT),
        name="post",
    )(xp, xs, yvp, yvs, mixb, wglu, bglu, ga, wout, gffn, wrt, brt)


def _invert_kernel(cpos0_ref, cpos1_ref, csrc_ref):
    i = pl.program_id(0)

    def put(r, _):
        tok = i * TB + r
        csrc_ref[cpos0_ref[r]] = tok
        csrc_ref[cpos1_ref[r]] = tok
        return 0

    lax.fori_loop(0, TB, put, 0, unroll=ISSUE_UNROLL)


def _invert_call(cpos0, cpos1):
    blk = pl.BlockSpec((TB,), lambda i: (i,), memory_space=pltpu.SMEM)
    return pl.pallas_call(
        _invert_kernel,
        grid=(NB,),
        in_specs=[blk, blk],
        out_specs=pl.BlockSpec(memory_space=pltpu.SMEM),
        out_shape=jax.ShapeDtypeStruct((N_PAIRS,), I32),
        compiler_params=pltpu.CompilerParams(dimension_semantics=("arbitrary",)),
        name="invert",
    )(cpos0, cpos1)


def _expert_kernel(te_ref, nu_ref, ts_ref, csrc_ref, xmid_ref, gffn_ref, wg_ref, wu_ref, wd_ref,
                   r_ref, wgb, wub, wdb, hbuf, xbuf, sems):
    i = pl.program_id(0)
    n_used = nu_ref[0]
    prev = te_ref[jnp.maximum(i - 1, 0)]
    fresh = (i == 0) | (te_ref[i] != prev)

    @pl.when(fresh)
    def _():
        wgb[...] = wg_ref[...].astype(BF16)
        wub[...] = wu_ref[...].astype(BF16)
        wdb[...] = wd_ref[...].astype(BF16)

    def gather_rows(tile, slot):
        base = ts_ref[tile]
        for r in range(TM):
            tok = csrc_ref[jnp.minimum(base + r, N_PAIRS - 1)]
            pltpu.make_async_copy(xmid_ref.at[pl.ds(tok, 1)], xbuf.at[slot, pl.ds(r, 1)],
                                  sems.at[slot]).start(priority=r % N_DMA_THREADS)

    def wait_rows(slot):
        pltpu.make_async_copy(xbuf.at[slot], xbuf.at[slot], sems.at[slot]).wait()

    @pl.when(i == 0)
    def _():
        gather_rows(0, 0)

    used = i < n_used
    slot = i % 2

    @pl.when(used)
    def _():
        wait_rows(slot)
        hbuf[...] = _rms(xbuf[slot], gffn_ref[...]).astype(BF16)
        gather_rows(jnp.minimum(i + 1, n_used - 1), 1 - slot)
        hb = hbuf[...]
        g = jnp.dot(hb, wgb[...], preferred_element_type=F32)
        u = jnp.dot(hb, wub[...], preferred_element_type=F32)
        hid = (g * _sigmoid(g) * u).astype(BF16)
        r_ref[...] = jnp.dot(hid, wdb[...], preferred_element_type=F32)

    @pl.when(i == n_used - 1)
    def _():
        wait_rows(1 - slot)

    @pl.when(jnp.logical_not(used))
    def _():
        r_ref[...] = jnp.zeros_like(r_ref)


def _expert_call(layer, te, nu, ts, csrc, xmid, gffn, w_gate, w_up, w_down):
    def wmap(i, te, nu, ts, csrc):
        return (layer, te[i], 0, 0)

    return pl.pallas_call(
        _expert_kernel,
        grid_spec=pltpu.PrefetchScalarGridSpec(
            num_scalar_prefetch=4,
            grid=(NT,),
            in_specs=[
                pl.BlockSpec(memory_space=pl.ANY),
                pl.BlockSpec((1, D_MODEL), lambda i, *_: (0, 0)),
                pl.BlockSpec((None, None, D_MODEL, D_FF), wmap),
                pl.BlockSpec((None, None, D_MODEL, D_FF), wmap),
                pl.BlockSpec((None, None, D_FF, D_MODEL), wmap),
            ],
            out_specs=pl.BlockSpec((TM, D_MODEL), lambda i, *_: (i, 0)),
            scratch_shapes=[pltpu.VMEM((D_MODEL, D_FF), BF16),
                            pltpu.VMEM((D_MODEL, D_FF), BF16),
                            pltpu.VMEM((D_FF, D_MODEL), BF16),
                            pltpu.VMEM((TM, D_MODEL), BF16),
                            pltpu.VMEM((2, TM, D_MODEL), F32),
                            pltpu.SemaphoreType.DMA((2,))],
        ),
        out_shape=jax.ShapeDtypeStruct((P_CAP, D_MODEL), F32),
        compiler_params=pltpu.CompilerParams(
            dimension_semantics=("arbitrary",), vmem_limit_bytes=VMEM_LIMIT),
        name="experts",
    )(te, nu, ts, csrc, xmid, gffn, w_gate, w_up, w_down)


def _combine_kernel(pos0_ref, pos1_ref, xmid_ref, gate_ref, gfin_ref, r_ref, *rest, final):
    if final:
        outp_ref, outs_ref, buf, sem = rest
    else:
        out_ref, buf, sem = rest
    i = pl.program_id(0)

    def issue(r, _):
        for k, pos_ref in enumerate((pos0_ref, pos1_ref)):
            p = pos_ref[r]
            pltpu.make_async_copy(r_ref.at[pl.ds(p, 1)], buf.at[k, pl.ds(r, 1)], sem).start(
                priority=k % N_DMA_THREADS)
        return 0

    lax.fori_loop(0, TB, issue, 0, unroll=ISSUE_UNROLL)
    for k in range(2):
        pltpu.make_async_copy(buf.at[k], buf.at[k], sem).wait()
    gt = gate_ref[...]
    out = xmid_ref[...] + gt[:, 0:1] * buf[0] + gt[:, 1:2] * buf[1]
    if not final:
        out_ref[...] = out
    else:
        out = _rms(out, gfin_ref[...])

        @pl.when(i < NB - 1)
        def _():
            outp_ref[...] = out

        @pl.when(i == NB - 1)
        def _():
            outs_ref[...] = out


def _combine_call(pos0, pos1, xmid, gate_t, gfin, r, final):
    if final:
        out_specs = [pl.BlockSpec((TB, D_MODEL), lambda i: (jnp.minimum(i, NBP - 1), 0)),
                     pl.BlockSpec((TB, D_MODEL), lambda i: (0, 0))]
        out_shape = [jax.ShapeDtypeStruct((T_PROMPT, D_MODEL), F32),
                     jax.ShapeDtypeStruct((T_SAMPLE, D_MODEL), F32)]
    else:
        out_specs = pl.BlockSpec((TB, D_MODEL), lambda i: (i, 0))
        out_shape = jax.ShapeDtypeStruct((T_ALL, D_MODEL), F32)
    return pl.pallas_call(
        functools.partial(_combine_kernel, final=final),
        grid=(NB,),
        in_specs=[
            pl.BlockSpec((TB,), lambda i: (i,), memory_space=pltpu.SMEM),
            pl.BlockSpec((TB,), lambda i: (i,), memory_space=pltpu.SMEM),
            pl.BlockSpec((TB, D_MODEL), lambda i: (i, 0)),
            pl.BlockSpec((TB, 2), lambda i: (i, 0)),
            pl.BlockSpec((1, D_MODEL), lambda i: (0, 0)),
            pl.BlockSpec(memory_space=pl.ANY),
        ],
        out_specs=out_specs,
        out_shape=out_shape,
        scratch_shapes=[pltpu.VMEM((2, TB, D_MODEL), F32), pltpu.SemaphoreType.DMA(())],
        compiler_params=pltpu.CompilerParams(
            dimension_semantics=("arbitrary",), vmem_limit_bytes=VMEM_LIMIT),
        name="combine",
    )(pos0, pos1, xmid, gate_t, gfin, r)


def _ssm_compact(lam_re, lam_im, log_dt, b_re, b_im, c_re, c_im, d_skip):
    nl = lam_re.shape[0]
    dt = jnp.exp(log_dt)[..., None]
    kpow = jnp.arange(CHUNK + 1, dtype=F32).reshape(CHUNK + 1, 1, 1, 1)
    mag = jnp.exp(kpow * (lam_re * dt))
    pw_re = mag * jnp.cos(kpow * (lam_im * dt))
    pw_im = mag * jnp.sin(kpow * (lam_im * dt))
    ab_re, ab_im = pw_re[1], pw_im[1]
    denom = lam_re * lam_re + lam_im * lam_im
    num_re = ab_re - 1.0
    f_re = (num_re * lam_re + ab_im * lam_im) / denom
    f_im = (ab_im * lam_re - num_re * lam_im) / denom
    bb_re = f_re[..., None] * b_re - f_im[..., None] * b_im
    bb_im = f_re[..., None] * b_im + f_im[..., None] * b_re

    krev = (CHUNK - 1.0) - kpow[:CHUNK]
    rmag = jnp.exp(krev * (lam_re * dt))
    rev_re = (rmag * jnp.cos(krev * (lam_im * dt)))[..., None]
    rev_im = (rmag * jnp.sin(krev * (lam_im * dt)))[..., None]
    ab = jnp.stack([rev_re * bb_re - rev_im * bb_im, rev_re * bb_im + rev_im * bb_re])
    ab = ab.reshape(2, CHUNK, nl, N_SG, GPS, N_ST, G_CH)
    ab = ab.transpose(2, 3, 1, 0, 4, 6, 5).reshape(nl, N_SG, CHUNK, 2, LANES, N_ST)
    abd = jnp.concatenate([ab, ab], axis=-1).astype(BF16)

    pr = pw_re[:, :, :, None, :]
    pi = pw_im[:, :, :, None, :]
    ca_re = c_re * pr - c_im * pi
    ca_im = c_re * pi + c_im * pr
    ca = jnp.stack([ca_re[1:], -ca_im[1:]])
    ca = ca.reshape(2, CHUNK, nl, N_SG, GPS, G_CH, N_ST)
    cac = ca.transpose(2, 3, 1, 0, 6, 4, 5).reshape(nl, N_SG, CHUNK, 2, N_ST, LANES).astype(BF16)

    bq_re = jnp.swapaxes(bb_re, -1, -2)[:, :, None]
    bq_im = jnp.swapaxes(bb_im, -1, -2)[:, :, None]
    kk = jnp.sum(ca_re[:CHUNK, :, :, :, None, :] * bq_re
                 - ca_im[:CHUNK, :, :, :, None, :] * bq_im, axis=-1)
    skip = d_skip.reshape(nl, N_G, G_CH)[..., None] * jnp.eye(G_CH, dtype=F32)
    kk = jnp.concatenate([kk[:1] + skip[None], kk[1:]], axis=0)
    kk = kk.reshape(CHUNK, nl, N_SG, GPS, G_CH, G_CH)
    kk = kk.transpose(1, 2, 0, 5, 3, 4)
    eye = jnp.eye(GPS, dtype=F32).reshape(GPS, 1, GPS, 1)
    kc = (kk[:, :, :, None] * eye).reshape(nl, N_SG, CHUNK, LANES, LANES).astype(BF16)

    al_re = pw_re[CHUNK].reshape(nl, 1, ST_W)
    al_im = pw_im[CHUNK].reshape(nl, 1, ST_W)
    return abd, cac, kc, al_re, al_im


def _tile_tables(counts):
    padded = ((counts + TM - 1) // TM) * TM
    ends = jnp.cumsum(padded)
    offs = ends - padded
    end_tile = ends // TM
    n_used = end_tile[-1].astype(I32)
    tile = jnp.arange(NT, dtype=I32)
    live = jnp.minimum(tile, n_used - 1)
    te = jnp.sum(end_tile[None, :] <= live[:, None], axis=1).astype(I32)
    cstart = jnp.cumsum(counts) - counts
    ts = (cstart[te] + (live - (offs // TM)[te]) * TM).astype(I32)
    return offs.astype(I32), cstart.astype(I32), te, n_used.reshape(1), ts


def kernel(x_prompt, x_sample, state_ssm_re, state_ssm_im, cache_conv, norm_mix, w_in, lam_re,
           lam_im, log_dt, ssm_b_re, ssm_b_im, ssm_c_re, ssm_c_im, ssm_d, w_glu, b_glu, w_conv,
           norm_a, norm_b, w_out, norm_ffn, w_router_group, b_router_group, w_router_expert,
           b_router_expert, w_gate, w_up, w_down, norm_final):
    xp = x_prompt.reshape(T_PROMPT, D_MODEL)
    xs = x_sample.reshape(T_SAMPLE, D_MODEL)
    s_block = 0
    gfin = norm_final.reshape(1, D_MODEL)
    abd, cac, kc, al_re, al_im = _ssm_compact(
        lam_re, lam_im, log_dt, ssm_b_re, ssm_b_im, ssm_c_re, ssm_c_im, ssm_d)
    p_re, p_im, p_buf, s_re, s_im, s_buf = [], [], [], [], [], []
    for l in range(DEPTH):
        cache = cache_conv[l]
        cz1 = jnp.pad(cache[:, 1:2], ((0, 0), (0, LEN_S - 1), (0, 0))).reshape(T_SAMPLE, CONV_W)
        cz2 = jnp.pad(cache, ((0, 0), (0, LEN_S - 2), (0, 0))).reshape(T_SAMPLE, CONV_W)
        uvp, uvs, mixb, ptail, zs = _inproj_call(
            xp, xs, s_block, norm_mix[l].reshape(1, D_MODEL), w_in[l].astype(BF16), w_conv[l],
            norm_b[l].reshape(1, CONV_W), cz1, cz2)

        yvp, yvs, pst_r, pst_i, sst_r, sst_i = _ssm_call(
            uvp, uvs, abd[l], cac[l], kc[l], al_re[l], al_im[l],
            state_ssm_re[l].reshape(N_SEQ_S, ST_W), state_ssm_im[l].reshape(N_SEQ_S, ST_W))

        wrt = jnp.concatenate([
            w_router_expert[l].transpose(0, 2, 1).reshape(N_EXP, D_MODEL),
            w_router_group[l].T,
            jnp.zeros((RT_ROWS - N_EXP - N_EGRP, D_MODEL), F32)], axis=0).astype(BF16)
        brt = jnp.concatenate([b_router_expert[l].reshape(N_EXP), b_router_group[l],
                               jnp.zeros((RT_ROWS - N_EXP - N_EGRP,), F32)]).reshape(RT_ROWS, 1)
        gffn = norm_ffn[l].reshape(1, D_MODEL)
        xmid, ridx, rgate, cnt = _post_call(
            xp, xs, s_block, yvp, yvs, mixb, w_glu[l].astype(BF16), b_glu[l].reshape(1, SSM_W),
            norm_a[l].reshape(1, SSM_W), w_out[l].astype(BF16), gffn, wrt, brt)

        offs, cstart, te, n_used, ts = _tile_tables(cnt[:, 0])
        pos0 = offs[ridx[0]] + ridx[2]
        pos1 = offs[ridx[1]] + ridx[3]
        csrc = _invert_call(cstart[ridx[0]] + ridx[2], cstart[ridx[1]] + ridx[3])
        r = _expert_call(l, te, n_used, ts, csrc, xmid, gffn, w_gate, w_up, w_down)
        final = l == DEPTH - 1
        out = _combine_call(pos0, pos1, xmid, rgate[:2].T, gfin, r, final=final)
        if final:
            y_prompt, y_sample = out
        else:
            xp = xs = out
            s_block = NB - 1

        p_re.append(pst_r.reshape(1, N_G, N_ST))
        p_im.append(pst_i.reshape(1, N_G, N_ST))
        p_buf.append(ptail[SUBLANES - 2:].reshape(1, 2, CONV_W))
        s_re.append(sst_r.reshape(N_SEQ_S, N_G, N_ST))
        s_im.append(sst_i.reshape(N_SEQ_S, N_G, N_ST))
        s_buf.append(zs.reshape(N_SEQ_S, LEN_S, CONV_W)[:, LEN_S - 2:])

    return (y_prompt.reshape(1, T_PROMPT, D_MODEL), y_sample.reshape(N_SEQ_S, LEN_S, D_MODEL),
            jnp.stack(p_re), jnp.stack(p_im), jnp.stack(p_buf),
            jnp.stack(s_re), jnp.stack(s_im), jnp.stack(s_buf))
```

```python
import functools

import jax
import jax.numpy as jnp
from jax import lax
from jax.experimental import pallas as pl
from jax.experimental.pallas import tpu as pltpu

F32 = jnp.float32
BF16 = jnp.bfloat16
I32 = jnp.int32

D_MODEL = 1024
DEPTH = 2
T_PROMPT = 16384
N_SEQ_S = 32
LEN_S = 16
T_SAMPLE = N_SEQ_S * LEN_S
T_ALL = T_PROMPT + T_SAMPLE
SSM_W = 512
CONV_W = 512
N_G = 32
G_CH = 16
N_ST = 64
ST_W = N_G * N_ST
N_EGRP = 4
EPG = 8
N_EXP = 32
D_FF = 512
EPS = 1e-6

LANES = 128
SUBLANES = 8
TB = 512
NB = T_ALL // TB
NBP = T_PROMPT // TB
CHUNK = 8
CRB = TB // CHUNK
GPS = LANES // G_CH
N_SG = SSM_W // LANES
SG_ST = GPS * N_ST
CW = CHUNK * LANES
UVW = CHUNK * SSM_W
CR_P = T_PROMPT // CHUNK
CR_S = T_SAMPLE // CHUNK
CPS = LEN_S // CHUNK
CB = 256
NPB = CR_P // CB
TM = 256
P_CAP = 2 * T_ALL + N_EXP * TM
NT = P_CAP // TM
N_PAIRS = 2 * T_ALL
N_DMA_THREADS = 2
ISSUE_UNROLL = 8
RT_ROWS = 40

VMEM_LIMIT = 56 * 1024 * 1024

assert T_SAMPLE == TB and TB % CHUNK == 0 and CR_S <= CB and LEN_S % CHUNK == 0


def _rms(x, g):
    return x * lax.rsqrt(jnp.mean(x * x, axis=-1, keepdims=True) + EPS) * g


def _sigmoid(x):
    return 1.0 / (1.0 + jnp.exp(-x))


def _stream_specs(s_block):
    return [pl.BlockSpec((TB, D_MODEL), lambda i: (jnp.minimum(i, NBP - 1), 0)),
            pl.BlockSpec((TB, D_MODEL), lambda i: (s_block, 0))]


def _inproj_kernel(xp_ref, xs_ref, gmix_ref, win_ref, wconv_ref, gb_ref, cz1_ref, cz2_ref,
                   uvp_ref, uvs_ref, mixb_ref, ptail_ref, zs_ref, carry_ref, u_scr):
    i = pl.program_id(0)
    is_sample = i == NB - 1

    @pl.when(i == 0)
    def _():
        carry_ref[...] = jnp.zeros_like(carry_ref)

    x = jnp.where(is_sample, xs_ref[...], xp_ref[...])
    h = _rms(x, gmix_ref[...]).astype(BF16)
    proj = jnp.dot(h, win_ref[...], preferred_element_type=F32)
    for k in range(N_SG):
        u_scr[k] = proj[:, k * LANES:(k + 1) * LANES]
    gate_b = proj[:, SSM_W:SSM_W + CONV_W]
    gate_c = proj[:, SSM_W + CONV_W:SSM_W + 2 * CONV_W]
    v = proj[:, SSM_W + 2 * CONV_W:]
    z = gate_c * v
    row = lax.broadcasted_iota(I32, (TB, 1), 0)
    r1 = pltpu.roll(z, 1, 0)
    r2 = pltpu.roll(z, 2, 0)
    c6 = carry_ref[6:7, :]
    c7 = carry_ref[7:8, :]
    z1p = jnp.where(row == 0, c7, r1)
    z2p = jnp.where(row == 0, c6, jnp.where(row == 1, c7, r2))
    pos = row % LEN_S
    z1s = jnp.where(pos == 0, cz1_ref[...], r1)
    z2s = jnp.where(pos < 2, cz2_ref[...], r2)
    z1 = jnp.where(is_sample, z1s, z1p)
    z2 = jnp.where(is_sample, z2s, z2p)
    w = wconv_ref[...]
    conv = w[0:1, :] * z2 + w[1:2, :] * z1 + w[2:3, :] * z
    yb = gate_b * conv
    mixb_ref[...] = _rms(yb, gb_ref[...]).astype(BF16)
    carry_ref[...] = z[TB - SUBLANES:, :]

    def put_chunk_rows(uv_ref):
        for t in range(CHUNK):
            for k in range(N_SG):
                uv_ref[:, t * SSM_W + k * LANES:t * SSM_W + (k + 1) * LANES] = (
                    u_scr[k, pl.ds(t, CRB, stride=CHUNK), :])

    @pl.when(jnp.logical_not(is_sample))
    def _():
        put_chunk_rows(uvp_ref)

    @pl.when(i == NB - 2)
    def _():
        ptail_ref[...] = z[TB - SUBLANES:, :]

    @pl.when(is_sample)
    def _():
        put_chunk_rows(uvs_ref)
        zs_ref[...] = z


def _inproj_call(xp, xs, s_block, gmix, win, wconv, gb, cz1, cz2):
    const = lambda i: (0, 0)
    return pl.pallas_call(
        _inproj_kernel,
        grid=(NB,),
        in_specs=_stream_specs(s_block) + [
            pl.BlockSpec((1, D_MODEL), const),
            pl.BlockSpec((D_MODEL, 4 * SSM_W), const),
            pl.BlockSpec((3, CONV_W), const),
            pl.BlockSpec((1, CONV_W), const),
            pl.BlockSpec((TB, CONV_W), const),
            pl.BlockSpec((TB, CONV_W), const),
        ],
        out_specs=[
            pl.BlockSpec((CRB, UVW), lambda i: (jnp.minimum(i, NBP - 1), 0)),
            pl.BlockSpec((CRB, UVW), const),
            pl.BlockSpec((TB, CONV_W), lambda i: (i, 0)),
            pl.BlockSpec((SUBLANES, CONV_W), const),
            pl.BlockSpec((TB, CONV_W), const),
        ],
        out_shape=[
            jax.ShapeDtypeStruct((CR_P, UVW), F32),
            jax.ShapeDtypeStruct((CR_S, UVW), F32),
            jax.ShapeDtypeStruct((T_ALL, CONV_W), BF16),
            jax.ShapeDtypeStruct((SUBLANES, CONV_W), F32),
            jax.ShapeDtypeStruct((TB, CONV_W), F32),
        ],
        scratch_shapes=[pltpu.VMEM((SUBLANES, CONV_W), F32), pltpu.VMEM((N_SG, TB, LANES), F32)],
        compiler_params=pltpu.CompilerParams(
            dimension_semantics=("arbitrary",), vmem_limit_bytes=VMEM_LIMIT),
        name="inproj",
    )(xp, xs, gmix, win, wconv, gb, cz1, cz2)


def _expand_chunk_matrices(abd_ref, cac_ref, kc_ref, ws, wy, kb):
    row_g = lax.broadcasted_iota(I32, (LANES, LANES), 0) // G_CH
    lane_half = lax.broadcasted_iota(I32, (LANES, LANES), 1) // N_ST
    col_g = lax.broadcasted_iota(I32, (N_ST, LANES), 1) // G_CH
    col_g128 = lax.broadcasted_iota(I32, (LANES, LANES), 1) // G_CH
    zero_blk = jnp.zeros((LANES, LANES), BF16)
    for s in range(N_SG):
        for t in range(CHUNK):
            for r in range(2):
                a = abd_ref[s, t, r]
                for m in range(GPS // 2):
                    ws[s, t * LANES:(t + 1) * LANES,
                       r * SG_ST + m * LANES:r * SG_ST + (m + 1) * LANES] = jnp.where(
                           row_g == 2 * m + lane_half, a, jnp.zeros_like(a))
                c = cac_ref[s, t, r]
                for g in range(GPS):
                    wy[s, r * SG_ST + g * N_ST:r * SG_ST + (g + 1) * N_ST,
                       t * LANES:(t + 1) * LANES] = jnp.where(col_g == g, c, jnp.zeros_like(c))
        for k in range(CHUNK):
            kq = kc_ref[s, k]
            blk = jnp.where(row_g == col_g128, jnp.concatenate([kq] * GPS, axis=0),
                            jnp.zeros((LANES, LANES), BF16))
            for tp in range(CHUNK - k):
                t = tp + k
                kb[s, tp * LANES:(tp + 1) * LANES, t * LANES:(t + 1) * LANES] = blk
        for tp in range(CHUNK):
            for t in range(tp):
                kb[s, tp * LANES:(tp + 1) * LANES, t * LANES:(t + 1) * LANES] = zero_blk


def _ucat(uv_ref, sg):
    parts = [uv_ref[:, t * SSM_W + sg * LANES:t * SSM_W + (sg + 1) * LANES] for t in range(CHUNK)]
    return jnp.concatenate(parts, axis=1).astype(BF16)


def _ssm_local(uv_ref, ws, sre, sim):
    for sg in range(N_SG):
        s = jnp.dot(_ucat(uv_ref, sg), ws[sg], preferred_element_type=F32)
        sre[:, sg * SG_ST:(sg + 1) * SG_ST] = s[:, :SG_ST]
        sim[:, sg * SG_ST:(sg + 1) * SG_ST] = s[:, SG_ST:]


def _ssm_out(uv_ref, wy, kb, sre, sim, yv_ref):
    for sg in range(N_SG):
        sp = jnp.concatenate([sre[:, sg * SG_ST:(sg + 1) * SG_ST],
                              sim[:, sg * SG_ST:(sg + 1) * SG_ST]], axis=1).astype(BF16)
        y = (jnp.dot(sp, wy[sg], preferred_element_type=F32)
             + jnp.dot(_ucat(uv_ref, sg), kb[sg], preferred_element_type=F32))
        for t in range(CHUNK):
            yv_ref[:, t * SSM_W + sg * LANES:t * SSM_W + (sg + 1) * LANES] = (
                y[:, t * LANES:(t + 1) * LANES])


def _ssm_kernel(uvp_ref, uvs_ref, abd_ref, cac_ref, kc_ref, alr_ref, ali_ref, h0r_ref, h0i_ref,
                yvp_ref, yvs_ref, pstr_ref, psti_ref, sstr_ref, ssti_ref,
                ws, wy, kb, sre, sim, cre, cim):
    i = pl.program_id(0)

    @pl.when(i == 0)
    def _():
        cre[...] = jnp.zeros_like(cre)
        cim[...] = jnp.zeros_like(cim)
        _expand_chunk_matrices(abd_ref, cac_ref, kc_ref, ws, wy, kb)

    ar = alr_ref[...]
    ai = ali_ref[...]

    @pl.when(i < NPB)
    def _():
        _ssm_local(uvp_ref, ws, sre, sim)

        def step(c, carry):
            sr, si = carry
            lr = sre[pl.ds(c, 1), :]
            li = sim[pl.ds(c, 1), :]
            sre[pl.ds(c, 1), :] = sr
            sim[pl.ds(c, 1), :] = si
            return ar * sr - ai * si + lr, ar * si + ai * sr + li

        sr, si = lax.fori_loop(0, CB, step, (cre[...], cim[...]))
        cre[...] = sr
        cim[...] = si
        pstr_ref[...] = sr
        psti_ref[...] = si
        _ssm_out(uvp_ref, wy, kb, sre, sim, yvp_ref)

    @pl.when(i == NPB)
    def _():
        sre_s = sre.at[0:CR_S]
        sim_s = sim.at[0:CR_S]
        _ssm_local(uvs_ref, ws, sre_s, sim_s)

        def per_stream(q, _):
            sr = h0r_ref[pl.ds(q, 1), :]
            si = h0i_ref[pl.ds(q, 1), :]
            for k in range(CPS):
                c = q * CPS + k
                lr = sre_s[pl.ds(c, 1), :]
                li = sim_s[pl.ds(c, 1), :]
                sre_s[pl.ds(c, 1), :] = sr
                sim_s[pl.ds(c, 1), :] = si
                sr, si = ar * sr - ai * si + lr, ar * si + ai * sr + li
            sstr_ref[pl.ds(q, 1), :] = sr
            ssti_ref[pl.ds(q, 1), :] = si
            return 0

        lax.fori_loop(0, N_SEQ_S, per_stream, 0)
        _ssm_out(uvs_ref, wy, kb, sre_s, sim_s, yvs_ref)


def _ssm_call(uvp, uvs, abd, cac, kc, alr, ali, h0r, h0i):
    c2 = lambda i: (0, 0)
    c4 = lambda i: (0, 0, 0, 0)
    c5 = lambda i: (0, 0, 0, 0, 0)
    one = pl.Buffered(1)
    pblk = lambda i: (jnp.minimum(i, NPB - 1), 0)
    return pl.pallas_call(
        _ssm_kernel,
        grid=(NPB + 1,),
        in_specs=[pl.BlockSpec((CB, UVW), pblk),
                  pl.BlockSpec((CR_S, UVW), c2),
                  pl.BlockSpec((N_SG, CHUNK, 2, LANES, LANES), c5, pipeline_mode=one),
                  pl.BlockSpec((N_SG, CHUNK, 2, N_ST, LANES), c5, pipeline_mode=one),
                  pl.BlockSpec((N_SG, CHUNK, G_CH, LANES), c4, pipeline_mode=one),
                  pl.BlockSpec((1, ST_W), c2),
                  pl.BlockSpec((1, ST_W), c2),
                  pl.BlockSpec((N_SEQ_S, ST_W), c2),
                  pl.BlockSpec((N_SEQ_S, ST_W), c2)],
        out_specs=[pl.BlockSpec((CB, UVW), pblk),
                   pl.BlockSpec((CR_S, UVW), c2),
                   pl.BlockSpec((1, ST_W), c2),
                   pl.BlockSpec((1, ST_W), c2),
                   pl.BlockSpec((N_SEQ_S, ST_W), c2),
                   pl.BlockSpec((N_SEQ_S, ST_W), c2)],
        out_shape=[jax.ShapeDtypeStruct((CR_P, UVW), F32),
                   jax.ShapeDtypeStruct((CR_S, UVW), F32),
                   jax.ShapeDtypeStruct((1, ST_W), F32),
                   jax.ShapeDtypeStruct((1, ST_W), F32),
                   jax.ShapeDtypeStruct((N_SEQ_S, ST_W), F32),
                   jax.ShapeDtypeStruct((N_SEQ_S, ST_W), F32)],
        scratch_shapes=[pltpu.VMEM((N_SG, CW, 2 * SG_ST), BF16),
                        pltpu.VMEM((N_SG, 2 * SG_ST, CW), BF16),
                        pltpu.VMEM((N_SG, CW, CW), BF16),
                        pltpu.VMEM((CB, ST_W), F32), pltpu.VMEM((CB, ST_W), F32),
                        pltpu.VMEM((1, ST_W), F32), pltpu.VMEM((1, ST_W), F32)],
        compiler_params=pltpu.CompilerParams(
            dimension_semantics=("arbitrary",), vmem_limit_bytes=VMEM_LIMIT),
        name="ssm",
    )(uvp, uvs, abd, cac, kc, alr, ali, h0r, h0i)


def _post_kernel(xp_ref, xs_ref, yvp_ref, yvs_ref, mixb_ref, wglu_ref, bglu_ref, ga_ref,
                 wout_ref, gffn_ref, wrt_ref, brt_ref,
                 xmid_ref, ridx_ref, rgate_ref, cnt_ref, base_ref, y_scr):
    i = pl.program_id(0)
    is_sample = i == NB - 1

    @pl.when(i == 0)
    def _():
        base_ref[...] = jnp.zeros_like(base_ref)

    yv = jnp.where(is_sample, yvs_ref[...], yvp_ref[...])
    for t in range(CHUNK):
        for k in range(N_SG):
            y_scr[k, pl.ds(t, CRB, stride=CHUNK), :] = (
                yv[:, t * SSM_W + k * LANES:t * SSM_W + (k + 1) * LANES])
    y = jnp.concatenate([y_scr[k] for k in range(N_SG)], axis=1)
    z = 0.5 * y * (1.0 + jnp.tanh(0.7978845608028654 * (y + 0.044715 * (y * y * y))))
    gl = jnp.dot(z.astype(BF16), wglu_ref[...], preferred_element_type=F32) + bglu_ref[...]
    out_a = z * _sigmoid(gl)
    mix = jnp.concatenate([_rms(out_a, ga_ref[...]).astype(BF16), mixb_ref[...]], axis=1)
    x = jnp.where(is_sample, xs_ref[...], xp_ref[...])
    xm = x + jnp.dot(mix, wout_ref[...], preferred_element_type=F32)
    xmid_ref[...] = xm
    hb = _rms(xm, gffn_ref[...]).astype(BF16)

    logit = lax.dot_general(wrt_ref[...], hb, (((1,), (1,)), ((), ())),
                            preferred_element_type=F32) + brt_ref[...]
    fine = logit[0:N_EXP, :]
    coarse = logit[N_EXP:N_EXP + N_EGRP, :]
    cmax = jnp.max(coarse, axis=0, keepdims=True)
    gi = lax.broadcasted_iota(I32, (N_EGRP, TB), 0).astype(F32)
    grp = jnp.min(jnp.where(coarse == cmax, gi, float(N_EGRP)), axis=0, keepdims=True)
    pg = 1.0 / jnp.sum(jnp.exp(coarse - cmax), axis=0, keepdims=True)
    eidx = lax.broadcasted_iota(I32, (N_EXP, TB), 0)
    ei = eidx.astype(F32)
    egrp = (eidx // EPG).astype(F32)
    neg = jnp.float32(-jnp.inf)
    fm = jnp.where(egrp == grp, fine, neg)
    v0 = jnp.max(fm, axis=0, keepdims=True)
    e0 = jnp.min(jnp.where(fm == v0, ei, float(N_EXP)), axis=0, keepdims=True)
    fm2 = jnp.where(ei == e0, neg, fm)
    v1 = jnp.max(fm2, axis=0, keepdims=True)
    e1 = jnp.min(jnp.where(fm2 == v1, ei, float(N_EXP)), axis=0, keepdims=True)
    tt = jnp.exp(v1 - v0)
    g0 = pg / (1.0 + tt)
    g1 = pg * tt / (1.0 + tt)

    sel0 = ei == e0
    sel1 = ei == e1
    cnt = jnp.where(sel0 | sel1, 1.0, 0.0)
    ta = lax.broadcasted_iota(I32, (TB, TB), 0)
    tb = lax.broadcasted_iota(I32, (TB, TB), 1)
    before = jnp.where(ta < tb, 1.0, 0.0).astype(BF16)
    cum = jnp.dot(cnt.astype(BF16), before, preferred_element_type=F32) + base_ref[...]
    rank0 = jnp.sum(jnp.where(sel0, cum, 0.0), axis=0, keepdims=True)
    rank1 = jnp.sum(jnp.where(sel1, cum, 0.0), axis=0, keepdims=True)
    base_new = base_ref[...] + jnp.sum(cnt, axis=1, keepdims=True)
    base_ref[...] = base_new
    cnt_ref[...] = jnp.broadcast_to(base_new, (N_EXP, LANES)).astype(I32)

    zi = jnp.zeros((1, TB), I32)
    ridx_ref[...] = jnp.concatenate(
        [e0.astype(I32), e1.astype(I32), rank0.astype(I32), rank1.astype(I32), zi, zi, zi, zi],
        axis=0)
    zf = jnp.zeros((1, TB), F32)
    rgate_ref[...] = jnp.concatenate([g0, g1, zf, zf, zf, zf, zf, zf], axis=0)


def _post_call(xp, xs, s_block, yvp, yvs, mixb, wglu, bglu, ga, wout, gffn, wrt, brt):
    const = lambda i: (0, 0)
    return pl.pallas_call(
        _post_kernel,
        grid=(NB,),
        in_specs=_stream_specs(s_block) + [
            pl.BlockSpec((CRB, UVW), lambda i: (jnp.minimum(i, NBP - 1), 0)),
            pl.BlockSpec((CRB, UVW), const),
            pl.BlockSpec((TB, CONV_W), lambda i: (i, 0)),
            pl.BlockSpec((SSM_W, SSM_W), const),
            pl.BlockSpec((1, SSM_W), const),
            pl.BlockSpec((1, SSM_W), const),
            pl.BlockSpec((D_MODEL, D_MODEL), const),
            pl.BlockSpec((1, D_MODEL), const),
            pl.BlockSpec((RT_ROWS, D_MODEL), const),
            pl.BlockSpec((RT_ROWS, 1), const),
        ],
        out_specs=[
            pl.BlockSpec((TB, D_MODEL), lambda i: (i, 0)),
            pl.BlockSpec((SUBLANES, TB), lambda i: (0, i)),
            pl.BlockSpec((SUBLANES, TB), lambda i: (0, i)),
            pl.BlockSpec((N_EXP, LANES), const),
        ],
        out_shape=[
            jax.ShapeDtypeStruct((T_ALL, D_MODEL), F32),
            jax.ShapeDtypeStruct((SUBLANES, T_ALL), I32),
            jax.ShapeDtypeStruct((SUBLANES, T_ALL), F32),
            jax.ShapeDtypeStruct((N_EXP, LANES), I32),
        ],
        scratch_shapes=[pltpu.VMEM((N_EXP, 1), F32), pltpu.VMEM((N_SG, TB, LANES), F32)],
        compiler_params=pltpu.CompilerParams(
            dimension_semantics=("arbitrary",), vmem_limit_bytes=VMEM_LIMIT),
        name="post",
    )(xp, xs, yvp, yvs, mixb, wglu, bglu, ga, wout, gffn, wrt, brt)


def _invert_kernel(cpos0_ref, cpos1_ref, csrc_ref):
    i = pl.program_id(0)

    def put(r, _):
        tok = i * TB + r
        csrc_ref[cpos0_ref[r]] = tok
        csrc_ref[cpos1_ref[r]] = tok
        return 0

    lax.fori_loop(0, TB, put, 0, unroll=ISSUE_UNROLL)


def _invert_call(cpos0, cpos1):
    blk = pl.BlockSpec((TB,), lambda i: (i,), memory_space=pltpu.SMEM)
    return pl.pallas_call(
        _invert_kernel,
        grid=(NB,),
        in_specs=[blk, blk],
        out_specs=pl.BlockSpec(memory_space=pltpu.SMEM),
        out_shape=jax.ShapeDtypeStruct((N_PAIRS,), I32),
        compiler_params=pltpu.CompilerParams(dimension_semantics=("arbitrary",)),
        name="invert",
    )(cpos0, cpos1)


def _expert_kernel(te_ref, nu_ref, ts_ref, csrc_ref, xmid_ref, gffn_ref, wg_ref, wu_ref, wd_ref,
                   r_ref, wgb, wub, wdb, hbuf, xbuf, sems):
    i = pl.program_id(0)
    n_used = nu_ref[0]
    prev = te_ref[jnp.maximum(i - 1, 0)]
    fresh = (i == 0) | (te_ref[i] != prev)

    @pl.when(fresh)
    def _():
        wgb[...] = wg_ref[...].astype(BF16)
        wub[...] = wu_ref[...].astype(BF16)
        wdb[...] = wd_ref[...].astype(BF16)

    def gather_rows(tile, slot):
        base = ts_ref[tile]
        for r in range(TM):
            tok = csrc_ref[jnp.minimum(base + r, N_PAIRS - 1)]
            pltpu.make_async_copy(xmid_ref.at[pl.ds(tok, 1)], xbuf.at[slot, pl.ds(r, 1)],
                                  sems.at[slot]).start(priority=r % N_DMA_THREADS)

    def wait_rows(slot):
        pltpu.make_async_copy(xbuf.at[slot], xbuf.at[slot], sems.at[slot]).wait()

    @pl.when(i == 0)
    def _():
        gather_rows(0, 0)

    used = i < n_used
    slot = i % 2

    @pl.when(used)
    def _():
        wait_rows(slot)
        hbuf[...] = _rms(xbuf[slot], gffn_ref[...]).astype(BF16)
        gather_rows(jnp.minimum(i + 1, n_used - 1), 1 - slot)
        hb = hbuf[...]
        g = jnp.dot(hb, wgb[...], preferred_element_type=F32)
        u = jnp.dot(hb, wub[...], preferred_element_type=F32)
        hid = (g * _sigmoid(g) * u).astype(BF16)
        r_ref[...] = jnp.dot(hid, wdb[...], preferred_element_type=F32)

    @pl.when(i == n_used - 1)
    def _():
        wait_rows(1 - slot)

    @pl.when(jnp.logical_not(used))
    def _():
        r_ref[...] = jnp.zeros_like(r_ref)


def _expert_call(layer, te, nu, ts, csrc, xmid, gffn, w_gate, w_up, w_down):
    def wmap(i, te, nu, ts, csrc):
        return (layer, te[i], 0, 0)

    return pl.pallas_call(
        _expert_kernel,
        grid_spec=pltpu.PrefetchScalarGridSpec(
            num_scalar_prefetch=4,
            grid=(NT,),
            in_specs=[
                pl.BlockSpec(memory_space=pl.ANY),
                pl.BlockSpec((1, D_MODEL), lambda i, *_: (0, 0)),
                pl.BlockSpec((None, None, D_MODEL, D_FF), wmap),
                pl.BlockSpec((None, None, D_MODEL, D_FF), wmap),
                pl.BlockSpec((None, None, D_FF, D_MODEL), wmap),
            ],
            out_specs=pl.BlockSpec((TM, D_MODEL), lambda i, *_: (i, 0)),
            scratch_shapes=[pltpu.VMEM((D_MODEL, D_FF), BF16),
                            pltpu.VMEM((D_MODEL, D_FF), BF16),
                            pltpu.VMEM((D_FF, D_MODEL), BF16),
                            pltpu.VMEM((TM, D_MODEL), BF16),
                            pltpu.VMEM((2, TM, D_MODEL), F32),
                            pltpu.SemaphoreType.DMA((2,))],
        ),
        out_shape=jax.ShapeDtypeStruct((P_CAP, D_MODEL), F32),
        compiler_params=pltpu.CompilerParams(
            dimension_semantics=("arbitrary",), vmem_limit_bytes=VMEM_LIMIT),
        name="experts",
    )(te, nu, ts, csrc, xmid, gffn, w_gate, w_up, w_down)


def _combine_kernel(pos0_ref, pos1_ref, xmid_ref, gate_ref, gfin_ref, r_ref, *rest, final):
    if final:
        outp_ref, outs_ref, buf, sem = rest
    else:
        out_ref, buf, sem = rest
    i = pl.program_id(0)

    def issue(r, _):
        for k, pos_ref in enumerate((pos0_ref, pos1_ref)):
            p = pos_ref[r]
            pltpu.make_async_copy(r_ref.at[pl.ds(p, 1)], buf.at[k, pl.ds(r, 1)], sem).start(
                priority=k % N_DMA_THREADS)
        return 0

    lax.fori_loop(0, TB, issue, 0, unroll=ISSUE_UNROLL)
    for k in range(2):
        pltpu.make_async_copy(buf.at[k], buf.at[k], sem).wait()
    gt = gate_ref[...]
    out = xmid_ref[...] + gt[:, 0:1] * buf[0] + gt[:, 1:2] * buf[1]
    if not final:
        out_ref[...] = out
    else:
        out = _rms(out, gfin_ref[...])

        @pl.when(i < NB - 1)
        def _():
            outp_ref[...] = out

        @pl.when(i == NB - 1)
        def _():
            outs_ref[...] = out


def _combine_call(pos0, pos1, xmid, gate_t, gfin, r, final):
    if final:
        out_specs = [pl.BlockSpec((TB, D_MODEL), lambda i: (jnp.minimum(i, NBP - 1), 0)),
                     pl.BlockSpec((TB, D_MODEL), lambda i: (0, 0))]
        out_shape = [jax.ShapeDtypeStruct((T_PROMPT, D_MODEL), F32),
                     jax.ShapeDtypeStruct((T_SAMPLE, D_MODEL), F32)]
    else:
        out_specs = pl.BlockSpec((TB, D_MODEL), lambda i: (i, 0))
        out_shape = jax.ShapeDtypeStruct((T_ALL, D_MODEL), F32)
    return pl.pallas_call(
        functools.partial(_combine_kernel, final=final),
        grid=(NB,),
        in_specs=[
            pl.BlockSpec((TB,), lambda i: (i,), memory_space=pltpu.SMEM),
            pl.BlockSpec((TB,), lambda i: (i,), memory_space=pltpu.SMEM),
            pl.BlockSpec((TB, D_MODEL), lambda i: (i, 0)),
            pl.BlockSpec((TB, 2), lambda i: (i, 0)),
            pl.BlockSpec((1, D_MODEL), lambda i: (0, 0)),
            pl.BlockSpec(memory_space=pl.ANY),
        ],
        out_specs=out_specs,
        out_shape=out_shape,
        scratch_shapes=[pltpu.VMEM((2, TB, D_MODEL), F32), pltpu.SemaphoreType.DMA(())],
        compiler_params=pltpu.CompilerParams(
            dimension_semantics=("arbitrary",), vmem_limit_bytes=VMEM_LIMIT),
        name="combine",
    )(pos0, pos1, xmid, gate_t, gfin, r)


def _ssm_compact(lam_re, lam_im, log_dt, b_re, b_im, c_re, c_im, d_skip):
    nl = lam_re.shape[0]
    dt = jnp.exp(log_dt)[..., None]
    kpow = jnp.arange(CHUNK + 1, dtype=F32).reshape(CHUNK + 1, 1, 1, 1)
    mag = jnp.exp(kpow * (lam_re * dt))
    pw_re = mag * jnp.cos(kpow * (lam_im * dt))
    pw_im = mag * jnp.sin(kpow * (lam_im * dt))
    ab_re, ab_im = pw_re[1], pw_im[1]
    denom = lam_re * lam_re + lam_im * lam_im
    num_re = ab_re - 1.0
    f_re = (num_re * lam_re + ab_im * lam_im) / denom
    f_im = (ab_im * lam_re - num_re * lam_im) / denom
    bb_re = f_re[..., None] * b_re - f_im[..., None] * b_im
    bb_im = f_re[..., None] * b_im + f_im[..., None] * b_re

    krev = (CHUNK - 1.0) - kpow[:CHUNK]
    rmag = jnp.exp(krev * (lam_re * dt))
    rev_re = (rmag * jnp.cos(krev * (lam_im * dt)))[..., None]
    rev_im = (rmag * jnp.sin(krev * (lam_im * dt)))[..., None]
    ab = jnp.stack([rev_re * bb_re - rev_im * bb_im, rev_re * bb_im + rev_im * bb_re])
    ab = ab.reshape(2, CHUNK, nl, N_SG, GPS, N_ST, G_CH)
    ab = ab.transpose(2, 3, 1, 0, 4, 6, 5).reshape(nl, N_SG, CHUNK, 2, LANES, N_ST)
    abd = jnp.concatenate([ab, ab], axis=-1).astype(BF16)

    pr = pw_re[:, :, :, None, :]
    pi = pw_im[:, :, :, None, :]
    ca_re = c_re * pr - c_im * pi
    ca_im = c_re * pi + c_im * pr
    ca = jnp.stack([ca_re[1:], -ca_im[1:]])
    ca = ca.reshape(2, CHUNK, nl, N_SG, GPS, G_CH, N_ST)
    cac = ca.transpose(2, 3, 1, 0, 6, 4, 5).reshape(nl, N_SG, CHUNK, 2, N_ST, LANES).astype(BF16)

    bq_re = jnp.swapaxes(bb_re, -1, -2)[:, :, None]
    bq_im = jnp.swapaxes(bb_im, -1, -2)[:, :, None]
    kk = jnp.sum(ca_re[:CHUNK, :, :, :, None, :] * bq_re
                 - ca_im[:CHUNK, :, :, :, None, :] * bq_im, axis=-1)
    skip = d_skip.reshape(nl, N_G, G_CH)[..., None] * jnp.eye(G_CH, dtype=F32)
    kk = jnp.concatenate([kk[:1] + skip[None], kk[1:]], axis=0)
    kk = kk.reshape(CHUNK, nl, N_SG, GPS, G_CH, G_CH)
    kk = kk.transpose(1, 2, 0, 5, 3, 4)
    kc = kk.reshape(nl, N_SG, CHUNK, G_CH, LANES).astype(BF16)

    al_re = pw_re[CHUNK].reshape(nl, 1, ST_W)
    al_im = pw_im[CHUNK].reshape(nl, 1, ST_W)
    return abd, cac, kc, al_re, al_im


def _tile_tables(counts):
    padded = ((counts + TM - 1) // TM) * TM
    ends = jnp.cumsum(padded)
    offs = ends - padded
    end_tile = ends // TM
    n_used = end_tile[-1].astype(I32)
    tile = jnp.arange(NT, dtype=I32)
    live = jnp.minimum(tile, n_used - 1)
    te = jnp.sum(end_tile[None, :] <= live[:, None], axis=1).astype(I32)
    cstart = jnp.cumsum(counts) - counts
    mine = te[:, None] == jnp.arange(N_EXP, dtype=I32)[None, :]
    first_row = jnp.sum(jnp.where(mine, (cstart - offs)[None, :], 0), axis=1)
    ts = (first_row + live * TM).astype(I32)
    return offs.astype(I32), cstart.astype(I32), te, n_used.reshape(1), ts


def _sorted_positions(ridx, offs, cstart):
    experts = jnp.arange(N_EXP, dtype=I32)[:, None]
    out = []
    for k in range(2):
        hit = ridx[k][None, :] == experts
        for table in (offs, cstart):
            out.append(jnp.sum(jnp.where(hit, table[:, None], 0), axis=0) + ridx[2 + k])
    pos0, cpos0, pos1, cpos1 = out
    return pos0, pos1, cpos0, cpos1


def kernel(x_prompt, x_sample, state_ssm_re, state_ssm_im, cache_conv, norm_mix, w_in, lam_re,
           lam_im, log_dt, ssm_b_re, ssm_b_im, ssm_c_re, ssm_c_im, ssm_d, w_glu, b_glu, w_conv,
           norm_a, norm_b, w_out, norm_ffn, w_router_group, b_router_group, w_router_expert,
           b_router_expert, w_gate, w_up, w_down, norm_final):
    xp = x_prompt.reshape(T_PROMPT, D_MODEL)
    xs = x_sample.reshape(T_SAMPLE, D_MODEL)
    s_block = 0
    gfin = norm_final.reshape(1, D_MODEL)
    abd, cac, kc, al_re, al_im = _ssm_compact(
        lam_re, lam_im, log_dt, ssm_b_re, ssm_b_im, ssm_c_re, ssm_c_im, ssm_d)
    p_re, p_im, p_buf, s_re, s_im, s_buf = [], [], [], [], [], []
    for l in range(DEPTH):
        cache = cache_conv[l]
        cz1 = jnp.pad(cache[:, 1:2], ((0, 0), (0, LEN_S - 1), (0, 0))).reshape(T_SAMPLE, CONV_W)
        cz2 = jnp.pad(cache, ((0, 0), (0, LEN_S - 2), (0, 0))).reshape(T_SAMPLE, CONV_W)
        uvp, uvs, mixb, ptail, zs = _inproj_call(
            xp, xs, s_block, norm_mix[l].reshape(1, D_MODEL), w_in[l].astype(BF16), w_conv[l],
            norm_b[l].reshape(1, CONV_W), cz1, cz2)

        yvp, yvs, pst_r, pst_i, sst_r, sst_i = _ssm_call(
            uvp, uvs, abd[l], cac[l], kc[l], al_re[l], al_im[l],
            state_ssm_re[l].reshape(N_SEQ_S, ST_W), state_ssm_im[l].reshape(N_SEQ_S, ST_W))

        wrt = jnp.concatenate([
            w_router_expert[l].transpose(0, 2, 1).reshape(N_EXP, D_MODEL),
            w_router_group[l].T,
            jnp.zeros((RT_ROWS - N_EXP - N_EGRP, D_MODEL), F32)], axis=0).astype(BF16)
        brt = jnp.concatenate([b_router_expert[l].reshape(N_EXP), b_router_group[l],
                               jnp.zeros((RT_ROWS - N_EXP - N_EGRP,), F32)]).reshape(RT_ROWS, 1)
        gffn = norm_ffn[l].reshape(1, D_MODEL)
        xmid, ridx, rgate, cnt = _post_call(
            xp, xs, s_block, yvp, yvs, mixb, w_glu[l].astype(BF16), b_glu[l].reshape(1, SSM_W),
            norm_a[l].reshape(1, SSM_W), w_out[l].astype(BF16), gffn, wrt, brt)

        offs, cstart, te, n_used, ts = _tile_tables(cnt[:, 0])
        pos0, pos1, cpos0, cpos1 = _sorted_positions(ridx, offs, cstart)
        csrc = _invert_call(cpos0, cpos1)
        r = _expert_call(l, te, n_used, ts, csrc, xmid, gffn, w_gate, w_up, w_down)
        final = l == DEPTH - 1
        out = _combine_call(pos0, pos1, xmid, rgate[:2].T, gfin, r, final=final)
        if final:
            y_prompt, y_sample = out
        else:
            xp = xs = out
            s_block = NB - 1

        p_re.append(pst_r.reshape(1, N_G, N_ST))
        p_im.append(pst_i.reshape(1, N_G, N_ST))
        p_buf.append(ptail[SUBLANES - 2:].reshape(1, 2, CONV_W))
        s_re.append(sst_r.reshape(N_SEQ_S, N_G, N_ST))
        s_im.append(sst_i.reshape(N_SEQ_S, N_G, N_ST))
        s_buf.append(zs.reshape(N_SEQ_S, LEN_S, CONV_W)[:, LEN_S - 2:])

    return (y_prompt.reshape(1, T_PROMPT, D_MODEL), y_sample.reshape(N_SEQ_S, LEN_S, D_MODEL),
            jnp.stack(p_re), jnp.stack(p_im), jnp.stack(p_buf),
            jnp.stack(s_re), jnp.stack(s_im), jnp.stack(s_buf))
```

```python
import functools

import jax
import jax.numpy as jnp
from jax import lax
from jax.experimental import pallas as pl
from jax.experimental.pallas import tpu as pltpu

F32 = jnp.float32
BF16 = jnp.bfloat16
I32 = jnp.int32

D_MODEL = 1024
DEPTH = 2
T_PROMPT = 16384
N_SEQ_S = 32
LEN_S = 16
T_SAMPLE = N_SEQ_S * LEN_S
T_ALL = T_PROMPT + T_SAMPLE
SSM_W = 512
CONV_W = 512
N_G = 32
G_CH = 16
N_ST = 64
ST_W = N_G * N_ST
N_EGRP = 4
EPG = 8
N_EXP = 32
D_FF = 512
EPS = 1e-6

LANES = 128
SUBLANES = 8
TB = 512
NB = T_ALL // TB
NBP = T_PROMPT // TB
CHUNK = 8
CRB = TB // CHUNK
GPS = LANES // G_CH
N_SG = SSM_W // LANES
SG_ST = GPS * N_ST
CW = CHUNK * LANES
UVW = CHUNK * SSM_W
CR_P = T_PROMPT // CHUNK
CR_S = T_SAMPLE // CHUNK
CPS = LEN_S // CHUNK
CB = 256
NPB = CR_P // CB
TM = 256
P_CAP = 2 * T_ALL + N_EXP * TM
NT = P_CAP // TM
N_PAIRS = 2 * T_ALL
N_DMA_THREADS = 2
ISSUE_UNROLL = 8
RT_ROWS = 40

VMEM_LIMIT = 56 * 1024 * 1024

assert T_SAMPLE == TB and TB % CHUNK == 0 and CR_S <= CB and LEN_S % CHUNK == 0


def _rms(x, g):
    return x * lax.rsqrt(jnp.mean(x * x, axis=-1, keepdims=True) + EPS) * g


def _sigmoid(x):
    return 1.0 / (1.0 + jnp.exp(-x))


def _put_token_tiles(ref, x):
    rows = x.shape[0]
    for j in range(D_MODEL // LANES):
        ref[pl.ds(j, rows, stride=SUBLANES), :] = x[:, j * LANES:(j + 1) * LANES]


def _get_token_tiles(ref, rows):
    return jnp.concatenate(
        [ref[pl.ds(j, rows, stride=SUBLANES), :] for j in range(D_MODEL // LANES)], axis=1)


def _stream_specs(s_block):
    return [pl.BlockSpec((TB, D_MODEL), lambda i: (jnp.minimum(i, NBP - 1), 0)),
            pl.BlockSpec((TB, D_MODEL), lambda i: (s_block, 0))]


def _inproj_kernel(xp_ref, xs_ref, gmix_ref, win_ref, wconv_ref, gb_ref, cz1_ref, cz2_ref,
                   uvp_ref, uvs_ref, mixb_ref, ptail_ref, zs_ref, carry_ref, u_scr):
    i = pl.program_id(0)
    is_sample = i == NB - 1

    @pl.when(i == 0)
    def _():
        carry_ref[...] = jnp.zeros_like(carry_ref)

    x = jnp.where(is_sample, xs_ref[...], xp_ref[...])
    h = _rms(x, gmix_ref[...]).astype(BF16)
    proj = jnp.dot(h, win_ref[...], preferred_element_type=F32)
    for k in range(N_SG):
        u_scr[k] = proj[:, k * LANES:(k + 1) * LANES]
    gate_b = proj[:, SSM_W:SSM_W + CONV_W]
    gate_c = proj[:, SSM_W + CONV_W:SSM_W + 2 * CONV_W]
    v = proj[:, SSM_W + 2 * CONV_W:]
    z = gate_c * v
    row = lax.broadcasted_iota(I32, (TB, 1), 0)
    r1 = pltpu.roll(z, 1, 0)
    r2 = pltpu.roll(z, 2, 0)
    c6 = carry_ref[6:7, :]
    c7 = carry_ref[7:8, :]
    z1p = jnp.where(row == 0, c7, r1)
    z2p = jnp.where(row == 0, c6, jnp.where(row == 1, c7, r2))
    pos = row % LEN_S
    z1s = jnp.where(pos == 0, cz1_ref[...], r1)
    z2s = jnp.where(pos < 2, cz2_ref[...], r2)
    z1 = jnp.where(is_sample, z1s, z1p)
    z2 = jnp.where(is_sample, z2s, z2p)
    w = wconv_ref[...]
    conv = w[0:1, :] * z2 + w[1:2, :] * z1 + w[2:3, :] * z
    yb = gate_b * conv
    mixb_ref[...] = _rms(yb, gb_ref[...]).astype(BF16)
    carry_ref[...] = z[TB - SUBLANES:, :]

    def put_chunk_rows(uv_ref):
        for t in range(CHUNK):
            for k in range(N_SG):
                uv_ref[:, t * SSM_W + k * LANES:t * SSM_W + (k + 1) * LANES] = (
                    u_scr[k, pl.ds(t, CRB, stride=CHUNK), :])

    @pl.when(jnp.logical_not(is_sample))
    def _():
        put_chunk_rows(uvp_ref)

    @pl.when(i == NB - 2)
    def _():
        ptail_ref[...] = z[TB - SUBLANES:, :]

    @pl.when(is_sample)
    def _():
        put_chunk_rows(uvs_ref)
        zs_ref[...] = z


def _inproj_call(xp, xs, s_block, gmix, win, wconv, gb, cz1, cz2):
    const = lambda i: (0, 0)
    return pl.pallas_call(
        _inproj_kernel,
        grid=(NB,),
        in_specs=_stream_specs(s_block) + [
            pl.BlockSpec((1, D_MODEL), const),
            pl.BlockSpec((D_MODEL, 4 * SSM_W), const),
            pl.BlockSpec((3, CONV_W), const),
            pl.BlockSpec((1, CONV_W), const),
            pl.BlockSpec((TB, CONV_W), const),
            pl.BlockSpec((TB, CONV_W), const),
        ],
        out_specs=[
            pl.BlockSpec((CRB, UVW), lambda i: (jnp.minimum(i, NBP - 1), 0)),
            pl.BlockSpec((CRB, UVW), const),
            pl.BlockSpec((TB, CONV_W), lambda i: (i, 0)),
            pl.BlockSpec((SUBLANES, CONV_W), const),
            pl.BlockSpec((TB, CONV_W), const),
        ],
        out_shape=[
            jax.ShapeDtypeStruct((CR_P, UVW), F32),
            jax.ShapeDtypeStruct((CR_S, UVW), F32),
            jax.ShapeDtypeStruct((T_ALL, CONV_W), BF16),
            jax.ShapeDtypeStruct((SUBLANES, CONV_W), F32),
            jax.ShapeDtypeStruct((TB, CONV_W), F32),
        ],
        scratch_shapes=[pltpu.VMEM((SUBLANES, CONV_W), F32), pltpu.VMEM((N_SG, TB, LANES), F32)],
        compiler_params=pltpu.CompilerParams(
            dimension_semantics=("arbitrary",), vmem_limit_bytes=VMEM_LIMIT),
        name="inproj",
    )(xp, xs, gmix, win, wconv, gb, cz1, cz2)


def _expand_chunk_matrices(abd_ref, cac_ref, kc_ref, ws, wy, kb):
    row_g = lax.broadcasted_iota(I32, (LANES, LANES), 0) // G_CH
    lane_half = lax.broadcasted_iota(I32, (LANES, LANES), 1) // N_ST
    col_g = lax.broadcasted_iota(I32, (N_ST, LANES), 1) // G_CH
    col_g128 = lax.broadcasted_iota(I32, (LANES, LANES), 1) // G_CH
    zero_blk = jnp.zeros((LANES, LANES), BF16)
    for s in range(N_SG):
        for t in range(CHUNK):
            for r in range(2):
                a = abd_ref[s, t, r]
                for m in range(GPS // 2):
                    ws[s, t * LANES:(t + 1) * LANES,
                       r * SG_ST + m * LANES:r * SG_ST + (m + 1) * LANES] = jnp.where(
                           row_g == 2 * m + lane_half, a, jnp.zeros_like(a))
                c = cac_ref[s, t, r]
                for g in range(GPS):
                    wy[s, r * SG_ST + g * N_ST:r * SG_ST + (g + 1) * N_ST,
                       t * LANES:(t + 1) * LANES] = jnp.where(col_g == g, c, jnp.zeros_like(c))
        for k in range(CHUNK):
            kq = kc_ref[s, k]
            blk = jnp.where(row_g == col_g128, jnp.concatenate([kq] * GPS, axis=0),
                            jnp.zeros((LANES, LANES), BF16))
            for tp in range(CHUNK - k):
                t = tp + k
                kb[s, tp * LANES:(tp + 1) * LANES, t * LANES:(t + 1) * LANES] = blk
        for tp in range(CHUNK):
            for t in range(tp):
                kb[s, tp * LANES:(tp + 1) * LANES, t * LANES:(t + 1) * LANES] = zero_blk


def _ucat(uv_ref, sg):
    parts = [uv_ref[:, t * SSM_W + sg * LANES:t * SSM_W + (sg + 1) * LANES] for t in range(CHUNK)]
    return jnp.concatenate(parts, axis=1).astype(BF16)


def _ssm_local(uv_ref, ws, sre, sim):
    for sg in range(N_SG):
        s = jnp.dot(_ucat(uv_ref, sg), ws[sg], preferred_element_type=F32)
        sre[:, sg * SG_ST:(sg + 1) * SG_ST] = s[:, :SG_ST]
        sim[:, sg * SG_ST:(sg + 1) * SG_ST] = s[:, SG_ST:]


def _ssm_out(uv_ref, wy, kb, sre, sim, yv_ref):
    for sg in range(N_SG):
        sp = jnp.concatenate([sre[:, sg * SG_ST:(sg + 1) * SG_ST],
                              sim[:, sg * SG_ST:(sg + 1) * SG_ST]], axis=1).astype(BF16)
        y = (jnp.dot(sp, wy[sg], preferred_element_type=F32)
             + jnp.dot(_ucat(uv_ref, sg), kb[sg], preferred_element_type=F32))
        for t in range(CHUNK):
            yv_ref[:, t * SSM_W + sg * LANES:t * SSM_W + (sg + 1) * LANES] = (
                y[:, t * LANES:(t + 1) * LANES])


def _ssm_kernel(uvp_ref, uvs_ref, abd_ref, cac_ref, kc_ref, alr_ref, ali_ref, h0r_ref, h0i_ref,
                yvp_ref, yvs_ref, pstr_ref, psti_ref, sstr_ref, ssti_ref,
                ws, wy, kb, sre, sim, cre, cim):
    i = pl.program_id(0)

    @pl.when(i == 0)
    def _():
        cre[...] = jnp.zeros_like(cre)
        cim[...] = jnp.zeros_like(cim)
        _expand_chunk_matrices(abd_ref, cac_ref, kc_ref, ws, wy, kb)

    ar = alr_ref[...]
    ai = ali_ref[...]

    @pl.when(i < NPB)
    def _():
        _ssm_local(uvp_ref, ws, sre, sim)

        def step(c, carry):
            sr, si = carry
            lr = sre[pl.ds(c, 1), :]
            li = sim[pl.ds(c, 1), :]
            sre[pl.ds(c, 1), :] = sr
            sim[pl.ds(c, 1), :] = si
            return ar * sr - ai * si + lr, ar * si + ai * sr + li

        sr, si = lax.fori_loop(0, CB, step, (cre[...], cim[...]))
        cre[...] = sr
        cim[...] = si
        pstr_ref[...] = sr
        psti_ref[...] = si
        _ssm_out(uvp_ref, wy, kb, sre, sim, yvp_ref)

    @pl.when(i == NPB)
    def _():
        sre_s = sre.at[0:CR_S]
        sim_s = sim.at[0:CR_S]
        _ssm_local(uvs_ref, ws, sre_s, sim_s)

        def per_stream(q, _):
            sr = h0r_ref[pl.ds(q, 1), :]
            si = h0i_ref[pl.ds(q, 1), :]
            for k in range(CPS):
                c = q * CPS + k
                lr = sre_s[pl.ds(c, 1), :]
                li = sim_s[pl.ds(c, 1), :]
                sre_s[pl.ds(c, 1), :] = sr
                sim_s[pl.ds(c, 1), :] = si
                sr, si = ar * sr - ai * si + lr, ar * si + ai * sr + li
            sstr_ref[pl.ds(q, 1), :] = sr
            ssti_ref[pl.ds(q, 1), :] = si
            return 0

        lax.fori_loop(0, N_SEQ_S, per_stream, 0)
        _ssm_out(uvs_ref, wy, kb, sre_s, sim_s, yvs_ref)


def _ssm_call(uvp, uvs, abd, cac, kc, alr, ali, h0r, h0i):
    c2 = lambda i: (0, 0)
    c4 = lambda i: (0, 0, 0, 0)
    c5 = lambda i: (0, 0, 0, 0, 0)
    one = pl.Buffered(1)
    pblk = lambda i: (jnp.minimum(i, NPB - 1), 0)
    return pl.pallas_call(
        _ssm_kernel,
        grid=(NPB + 1,),
        in_specs=[pl.BlockSpec((CB, UVW), pblk),
                  pl.BlockSpec((CR_S, UVW), c2),
                  pl.BlockSpec((N_SG, CHUNK, 2, LANES, LANES), c5, pipeline_mode=one),
                  pl.BlockSpec((N_SG, CHUNK, 2, N_ST, LANES), c5, pipeline_mode=one),
                  pl.BlockSpec((N_SG, CHUNK, G_CH, LANES), c4, pipeline_mode=one),
                  pl.BlockSpec((1, ST_W), c2),
                  pl.BlockSpec((1, ST_W), c2),
                  pl.BlockSpec((N_SEQ_S, ST_W), c2),
                  pl.BlockSpec((N_SEQ_S, ST_W), c2)],
        out_specs=[pl.BlockSpec((CB, UVW), pblk),
                   pl.BlockSpec((CR_S, UVW), c2),
                   pl.BlockSpec((1, ST_W), c2),
                   pl.BlockSpec((1, ST_W), c2),
                   pl.BlockSpec((N_SEQ_S, ST_W), c2),
                   pl.BlockSpec((N_SEQ_S, ST_W), c2)],
        out_shape=[jax.ShapeDtypeStruct((CR_P, UVW), F32),
                   jax.ShapeDtypeStruct((CR_S, UVW), F32),
                   jax.ShapeDtypeStruct((1, ST_W), F32),
                   jax.ShapeDtypeStruct((1, ST_W), F32),
                   jax.ShapeDtypeStruct((N_SEQ_S, ST_W), F32),
                   jax.ShapeDtypeStruct((N_SEQ_S, ST_W), F32)],
        scratch_shapes=[pltpu.VMEM((N_SG, CW, 2 * SG_ST), BF16),
                        pltpu.VMEM((N_SG, 2 * SG_ST, CW), BF16),
                        pltpu.VMEM((N_SG, CW, CW), BF16),
                        pltpu.VMEM((CB, ST_W), F32), pltpu.VMEM((CB, ST_W), F32),
                        pltpu.VMEM((1, ST_W), F32), pltpu.VMEM((1, ST_W), F32)],
        compiler_params=pltpu.CompilerParams(
            dimension_semantics=("arbitrary",), vmem_limit_bytes=VMEM_LIMIT),
        name="ssm",
    )(uvp, uvs, abd, cac, kc, alr, ali, h0r, h0i)


def _post_kernel(xp_ref, xs_ref, yvp_ref, yvs_ref, mixb_ref, wglu_ref, bglu_ref, ga_ref,
                 wout_ref, gffn_ref, wrt_ref, brt_ref,
                 xmid_ref, ridx_ref, rgate_ref, cnt_ref, base_ref, y_scr):
    i = pl.program_id(0)
    is_sample = i == NB - 1

    @pl.when(i == 0)
    def _():
        base_ref[...] = jnp.zeros_like(base_ref)

    yv = jnp.where(is_sample, yvs_ref[...], yvp_ref[...])
    for t in range(CHUNK):
        for k in range(N_SG):
            y_scr[k, pl.ds(t, CRB, stride=CHUNK), :] = (
                yv[:, t * SSM_W + k * LANES:t * SSM_W + (k + 1) * LANES])
    y = jnp.concatenate([y_scr[k] for k in range(N_SG)], axis=1)
    z = 0.5 * y * (1.0 + jnp.tanh(0.7978845608028654 * (y + 0.044715 * (y * y * y))))
    gl = jnp.dot(z.astype(BF16), wglu_ref[...], preferred_element_type=F32) + bglu_ref[...]
    out_a = z * _sigmoid(gl)
    mix = jnp.concatenate([_rms(out_a, ga_ref[...]).astype(BF16), mixb_ref[...]], axis=1)
    x = jnp.where(is_sample, xs_ref[...], xp_ref[...])
    xm = x + jnp.dot(mix, wout_ref[...], preferred_element_type=F32)
    _put_token_tiles(xmid_ref, xm)
    hn = _rms(xm, gffn_ref[...])

    def split(a):
        hi = a.astype(BF16)
        return hi, (a - hi.astype(F32)).astype(BF16)

    def dot_nt(a, b):
        return lax.dot_general(a, b, (((1,), (1,)), ((), ())), preferred_element_type=F32)

    h_hi, h_lo = split(hn)
    w_hi, w_lo = split(wrt_ref[...])
    logit = (dot_nt(w_hi, h_hi) + dot_nt(w_hi, h_lo) + dot_nt(w_lo, h_hi)
             + brt_ref[...])
    fine = logit[0:N_EXP, :]
    coarse = logit[N_EXP:N_EXP + N_EGRP, :]
    cmax = jnp.max(coarse, axis=0, keepdims=True)
    gi = lax.broadcasted_iota(I32, (N_EGRP, TB), 0).astype(F32)
    grp = jnp.min(jnp.where(coarse == cmax, gi, float(N_EGRP)), axis=0, keepdims=True)
    pg = 1.0 / jnp.sum(jnp.exp(coarse - cmax), axis=0, keepdims=True)
    eidx = lax.broadcasted_iota(I32, (N_EXP, TB), 0)
    ei = eidx.astype(F32)
    egrp = (eidx // EPG).astype(F32)
    neg = jnp.float32(-jnp.inf)
    fm = jnp.where(egrp == grp, fine, neg)
    v0 = jnp.max(fm, axis=0, keepdims=True)
    e0 = jnp.min(jnp.where(fm == v0, ei, float(N_EXP)), axis=0, keepdims=True)
    fm2 = jnp.where(ei == e0, neg, fm)
    v1 = jnp.max(fm2, axis=0, keepdims=True)
    e1 = jnp.min(jnp.where(fm2 == v1, ei, float(N_EXP)), axis=0, keepdims=True)
    tt = jnp.exp(v1 - v0)
    g0 = pg / (1.0 + tt)
    g1 = pg * tt / (1.0 + tt)

    sel0 = ei == e0
    sel1 = ei == e1
    cnt = jnp.where(sel0 | sel1, 1.0, 0.0)
    ta = lax.broadcasted_iota(I32, (TB, TB), 0)
    tb = lax.broadcasted_iota(I32, (TB, TB), 1)
    before = jnp.where(ta < tb, 1.0, 0.0).astype(BF16)
    cum = jnp.dot(cnt.astype(BF16), before, preferred_element_type=F32) + base_ref[...]
    rank0 = jnp.sum(jnp.where(sel0, cum, 0.0), axis=0, keepdims=True)
    rank1 = jnp.sum(jnp.where(sel1, cum, 0.0), axis=0, keepdims=True)
    base_new = base_ref[...] + jnp.sum(cnt, axis=1, keepdims=True)
    base_ref[...] = base_new
    cnt_ref[...] = jnp.broadcast_to(base_new, (N_EXP, LANES)).astype(I32)

    zi = jnp.zeros((1, TB), I32)
    ridx_ref[...] = jnp.concatenate(
        [e0.astype(I32), e1.astype(I32), rank0.astype(I32), rank1.astype(I32), zi, zi, zi, zi],
        axis=0)
    zf = jnp.zeros((1, TB), F32)
    rgate_ref[...] = jnp.concatenate([g0, g1, zf, zf, zf, zf, zf, zf], axis=0)


def _post_call(xp, xs, s_block, yvp, yvs, mixb, wglu, bglu, ga, wout, gffn, wrt, brt):
    const = lambda i: (0, 0)
    return pl.pallas_call(
        _post_kernel,
        grid=(NB,),
        in_specs=_stream_specs(s_block) + [
            pl.BlockSpec((CRB, UVW), lambda i: (jnp.minimum(i, NBP - 1), 0)),
            pl.BlockSpec((CRB, UVW), const),
            pl.BlockSpec((TB, CONV_W), lambda i: (i, 0)),
            pl.BlockSpec((SSM_W, SSM_W), const),
            pl.BlockSpec((1, SSM_W), const),
            pl.BlockSpec((1, SSM_W), const),
            pl.BlockSpec((D_MODEL, D_MODEL), const),
            pl.BlockSpec((1, D_MODEL), const),
            pl.BlockSpec((RT_ROWS, D_MODEL), const),
            pl.BlockSpec((RT_ROWS, 1), const),
        ],
        out_specs=[
            pl.BlockSpec((TB * SUBLANES, LANES), lambda i: (i, 0)),
            pl.BlockSpec((SUBLANES, TB), lambda i: (0, i)),
            pl.BlockSpec((SUBLANES, TB), lambda i: (0, i)),
            pl.BlockSpec((N_EXP, LANES), const),
        ],
        out_shape=[
            jax.ShapeDtypeStruct((T_ALL * SUBLANES, LANES), F32),
            jax.ShapeDtypeStruct((SUBLANES, T_ALL), I32),
            jax.ShapeDtypeStruct((SUBLANES, T_ALL), F32),
            jax.ShapeDtypeStruct((N_EXP, LANES), I32),
        ],
        scratch_shapes=[pltpu.VMEM((N_EXP, 1), F32), pltpu.VMEM((N_SG, TB, LANES), F32)],
        compiler_params=pltpu.CompilerParams(
            dimension_semantics=("arbitrary",), vmem_limit_bytes=VMEM_LIMIT),
        name="post",
    )(xp, xs, yvp, yvs, mixb, wglu, bglu, ga, wout, gffn, wrt, brt)


def _invert_kernel(cpos0_ref, cpos1_ref, csrc_ref):
    i = pl.program_id(0)

    def put(r, _):
        tok = i * TB + r
        csrc_ref[cpos0_ref[r]] = tok
        csrc_ref[cpos1_ref[r]] = tok
        return 0

    lax.fori_loop(0, TB, put, 0, unroll=ISSUE_UNROLL)


def _invert_call(cpos0, cpos1):
    blk = pl.BlockSpec((TB,), lambda i: (i,), memory_space=pltpu.SMEM)
    return pl.pallas_call(
        _invert_kernel,
        grid=(NB,),
        in_specs=[blk, blk],
        out_specs=pl.BlockSpec(memory_space=pltpu.SMEM),
        out_shape=jax.ShapeDtypeStruct((N_PAIRS,), I32),
        compiler_params=pltpu.CompilerParams(dimension_semantics=("arbitrary",)),
        name="invert",
    )(cpos0, cpos1)


def _expert_kernel(te_ref, nu_ref, ts_ref, csrc_ref, xmid_ref, gffn_ref, wg_ref, wu_ref, wd_ref,
                   r_ref, wgb, wub, wdb, hbuf, xbuf, sems):
    i = pl.program_id(0)
    n_used = nu_ref[0]
    prev = te_ref[jnp.maximum(i - 1, 0)]
    fresh = (i == 0) | (te_ref[i] != prev)

    @pl.when(fresh)
    def _():
        wgb[...] = wg_ref[...].astype(BF16)
        wub[...] = wu_ref[...].astype(BF16)
        wdb[...] = wd_ref[...].astype(BF16)

    def gather_rows(tile, slot):
        base = ts_ref[tile]
        for r in range(TM):
            tok = csrc_ref[jnp.minimum(base + r, N_PAIRS - 1)]
            src = xmid_ref.at[pl.ds(pl.multiple_of(tok * SUBLANES, SUBLANES), SUBLANES)]
            dst = xbuf.at[slot, pl.ds(r * SUBLANES, SUBLANES)]
            pltpu.make_async_copy(src, dst, sems.at[slot]).start(priority=r % N_DMA_THREADS)

    def wait_rows(slot):
        pltpu.make_async_copy(xbuf.at[slot], xbuf.at[slot], sems.at[slot]).wait()

    @pl.when(i == 0)
    def _():
        gather_rows(0, 0)

    used = i < n_used
    slot = i % 2

    @pl.when(used)
    def _():
        wait_rows(slot)
        hbuf[...] = _rms(_get_token_tiles(xbuf.at[slot], TM), gffn_ref[...]).astype(BF16)
        gather_rows(jnp.minimum(i + 1, n_used - 1), 1 - slot)
        hb = hbuf[...]
        g = jnp.dot(hb, wgb[...], preferred_element_type=F32)
        u = jnp.dot(hb, wub[...], preferred_element_type=F32)
        hid = (g * _sigmoid(g) * u).astype(BF16)
        _put_token_tiles(r_ref, jnp.dot(hid, wdb[...], preferred_element_type=F32))

    @pl.when(i == n_used - 1)
    def _():
        wait_rows(1 - slot)

    @pl.when(jnp.logical_not(used))
    def _():
        r_ref[...] = jnp.zeros_like(r_ref)


def _expert_call(layer, te, nu, ts, csrc, xmid, gffn, w_gate, w_up, w_down):
    def wmap(i, te, nu, ts, csrc):
        return (layer, te[i], 0, 0)

    return pl.pallas_call(
        _expert_kernel,
        grid_spec=pltpu.PrefetchScalarGridSpec(
            num_scalar_prefetch=4,
            grid=(NT,),
            in_specs=[
                pl.BlockSpec(memory_space=pl.ANY),
                pl.BlockSpec((1, D_MODEL), lambda i, *_: (0, 0)),
                pl.BlockSpec((None, None, D_MODEL, D_FF), wmap),
                pl.BlockSpec((None, None, D_MODEL, D_FF), wmap),
                pl.BlockSpec((None, None, D_FF, D_MODEL), wmap),
            ],
            out_specs=pl.BlockSpec((TM * SUBLANES, LANES), lambda i, *_: (i, 0)),
            scratch_shapes=[pltpu.VMEM((D_MODEL, D_FF), BF16),
                            pltpu.VMEM((D_MODEL, D_FF), BF16),
                            pltpu.VMEM((D_FF, D_MODEL), BF16),
                            pltpu.VMEM((TM, D_MODEL), BF16),
                            pltpu.VMEM((2, TM * SUBLANES, LANES), F32),
                            pltpu.SemaphoreType.DMA((2,))],
        ),
        out_shape=jax.ShapeDtypeStruct((P_CAP * SUBLANES, LANES), F32),
        compiler_params=pltpu.CompilerParams(
            dimension_semantics=("arbitrary",), vmem_limit_bytes=VMEM_LIMIT),
        name="experts",
    )(te, nu, ts, csrc, xmid, gffn, w_gate, w_up, w_down)


def _combine_kernel(pos0_ref, pos1_ref, xmid_ref, gate_ref, gfin_ref, r_ref, *rest, final):
    if final:
        outp_ref, outs_ref, buf, sem = rest
    else:
        out_ref, buf, sem = rest
    i = pl.program_id(0)

    def issue(r, _):
        for k, pos_ref in enumerate((pos0_ref, pos1_ref)):
            p = pos_ref[r]
            src = r_ref.at[pl.ds(pl.multiple_of(p * SUBLANES, SUBLANES), SUBLANES)]
            dst = buf.at[k, pl.ds(pl.multiple_of(r * SUBLANES, SUBLANES), SUBLANES)]
            pltpu.make_async_copy(src, dst, sem).start(priority=k % N_DMA_THREADS)
        return 0

    lax.fori_loop(0, TB, issue, 0, unroll=ISSUE_UNROLL)
    for k in range(2):
        pltpu.make_async_copy(buf.at[k], buf.at[k], sem).wait()
    gt = gate_ref[...]
    out = (_get_token_tiles(xmid_ref, TB) + gt[:, 0:1] * _get_token_tiles(buf.at[0], TB)
           + gt[:, 1:2] * _get_token_tiles(buf.at[1], TB))
    if not final:
        out_ref[...] = out
    else:
        out = _rms(out, gfin_ref[...])

        @pl.when(i < NB - 1)
        def _():
            outp_ref[...] = out

        @pl.when(i == NB - 1)
        def _():
            outs_ref[...] = out


def _combine_call(pos0, pos1, xmid, gate_t, gfin, r, final):
    if final:
        out_specs = [pl.BlockSpec((TB, D_MODEL), lambda i: (jnp.minimum(i, NBP - 1), 0)),
                     pl.BlockSpec((TB, D_MODEL), lambda i: (0, 0))]
        out_shape = [jax.ShapeDtypeStruct((T_PROMPT, D_MODEL), F32),
                     jax.ShapeDtypeStruct((T_SAMPLE, D_MODEL), F32)]
    else:
        out_specs = pl.BlockSpec((TB, D_MODEL), lambda i: (i, 0))
        out_shape = jax.ShapeDtypeStruct((T_ALL, D_MODEL), F32)
    return pl.pallas_call(
        functools.partial(_combine_kernel, final=final),
        grid=(NB,),
        in_specs=[
            pl.BlockSpec((TB,), lambda i: (i,), memory_space=pltpu.SMEM),
            pl.BlockSpec((TB,), lambda i: (i,), memory_space=pltpu.SMEM),
            pl.BlockSpec((TB * SUBLANES, LANES), lambda i: (i, 0)),
            pl.BlockSpec((TB, 2), lambda i: (i, 0)),
            pl.BlockSpec((1, D_MODEL), lambda i: (0, 0)),
            pl.BlockSpec(memory_space=pl.ANY),
        ],
        out_specs=out_specs,
        out_shape=out_shape,
        scratch_shapes=[pltpu.VMEM((2, TB * SUBLANES, LANES), F32), pltpu.SemaphoreType.DMA(())],
        compiler_params=pltpu.CompilerParams(
            dimension_semantics=("arbitrary",), vmem_limit_bytes=VMEM_LIMIT),
        name="combine",
    )(pos0, pos1, xmid, gate_t, gfin, r)


def _ssm_compact(lam_re, lam_im, log_dt, b_re, b_im, c_re, c_im, d_skip):
    nl = lam_re.shape[0]
    dt = jnp.exp(log_dt)[..., None]
    kpow = jnp.arange(CHUNK + 1, dtype=F32).reshape(CHUNK + 1, 1, 1, 1)
    mag = jnp.exp(kpow * (lam_re * dt))
    pw_re = mag * jnp.cos(kpow * (lam_im * dt))
    pw_im = mag * jnp.sin(kpow * (lam_im * dt))
    ab_re, ab_im = pw_re[1], pw_im[1]
    denom = lam_re * lam_re + lam_im * lam_im
    num_re = ab_re - 1.0
    f_re = (num_re * lam_re + ab_im * lam_im) / denom
    f_im = (ab_im * lam_re - num_re * lam_im) / denom
    bb_re = f_re[..., None] * b_re - f_im[..., None] * b_im
    bb_im = f_re[..., None] * b_im + f_im[..., None] * b_re

    krev = (CHUNK - 1.0) - kpow[:CHUNK]
    rmag = jnp.exp(krev * (lam_re * dt))
    rev_re = (rmag * jnp.cos(krev * (lam_im * dt)))[..., None]
    rev_im = (rmag * jnp.sin(krev * (lam_im * dt)))[..., None]
    ab = jnp.stack([rev_re * bb_re - rev_im * bb_im, rev_re * bb_im + rev_im * bb_re])
    ab = ab.reshape(2, CHUNK, nl, N_SG, GPS, N_ST, G_CH)
    ab = ab.transpose(2, 3, 1, 0, 4, 6, 5).reshape(nl, N_SG, CHUNK, 2, LANES, N_ST)
    abd = jnp.concatenate([ab, ab], axis=-1).astype(BF16)

    pr = pw_re[:, :, :, None, :]
    pi = pw_im[:, :, :, None, :]
    ca_re = c_re * pr - c_im * pi
    ca_im = c_re * pi + c_im * pr
    ca = jnp.stack([ca_re[1:], -ca_im[1:]])
    ca = ca.reshape(2, CHUNK, nl, N_SG, GPS, G_CH, N_ST)
    cac = ca.transpose(2, 3, 1, 0, 6, 4, 5).reshape(nl, N_SG, CHUNK, 2, N_ST, LANES).astype(BF16)

    bq_re = jnp.swapaxes(bb_re, -1, -2)[:, :, None]
    bq_im = jnp.swapaxes(bb_im, -1, -2)[:, :, None]
    kk = jnp.sum(ca_re[:CHUNK, :, :, :, None, :] * bq_re
                 - ca_im[:CHUNK, :, :, :, None, :] * bq_im, axis=-1)
    skip = d_skip.reshape(nl, N_G, G_CH)[..., None] * jnp.eye(G_CH, dtype=F32)
    kk = jnp.concatenate([kk[:1] + skip[None], kk[1:]], axis=0)
    kk = kk.reshape(CHUNK, nl, N_SG, GPS, G_CH, G_CH)
    kk = kk.transpose(1, 2, 0, 5, 3, 4)
    kc = kk.reshape(nl, N_SG, CHUNK, G_CH, LANES).astype(BF16)

    al_re = pw_re[CHUNK].reshape(nl, 1, ST_W)
    al_im = pw_im[CHUNK].reshape(nl, 1, ST_W)
    return abd, cac, kc, al_re, al_im


def _tile_tables(counts):
    padded = ((counts + TM - 1) // TM) * TM
    ends = jnp.cumsum(padded)
    offs = ends - padded
    end_tile = ends // TM
    n_used = end_tile[-1].astype(I32)
    tile = jnp.arange(NT, dtype=I32)
    live = jnp.minimum(tile, n_used - 1)
    te = jnp.sum(end_tile[None, :] <= live[:, None], axis=1).astype(I32)
    cstart = jnp.cumsum(counts) - counts
    mine = te[:, None] == jnp.arange(N_EXP, dtype=I32)[None, :]
    first_row = jnp.sum(jnp.where(mine, (cstart - offs)[None, :], 0), axis=1)
    ts = (first_row + live * TM).astype(I32)
    return offs.astype(I32), cstart.astype(I32), te, n_used.reshape(1), ts


def _sorted_positions(ridx, offs, cstart):
    experts = jnp.arange(N_EXP, dtype=I32)[:, None]
    out = []
    for k in range(2):
        hit = ridx[k][None, :] == experts
        for table in (offs, cstart):
            out.append(jnp.sum(jnp.where(hit, table[:, None], 0), axis=0) + ridx[2 + k])
    pos0, cpos0, pos1, cpos1 = out
    return pos0, pos1, cpos0, cpos1


def kernel(x_prompt, x_sample, state_ssm_re, state_ssm_im, cache_conv, norm_mix, w_in, lam_re,
           lam_im, log_dt, ssm_b_re, ssm_b_im, ssm_c_re, ssm_c_im, ssm_d, w_glu, b_glu, w_conv,
           norm_a, norm_b, w_out, norm_ffn, w_router_group, b_router_group, w_router_expert,
           b_router_expert, w_gate, w_up, w_down, norm_final):
    xp = x_prompt.reshape(T_PROMPT, D_MODEL)
    xs = x_sample.reshape(T_SAMPLE, D_MODEL)
    s_block = 0
    gfin = norm_final.reshape(1, D_MODEL)
    abd, cac, kc, al_re, al_im = _ssm_compact(
        lam_re, lam_im, log_dt, ssm_b_re, ssm_b_im, ssm_c_re, ssm_c_im, ssm_d)
    p_re, p_im, p_buf, s_re, s_im, s_buf = [], [], [], [], [], []
    for l in range(DEPTH):
        cache = cache_conv[l]
        cz1 = jnp.pad(cache[:, 1:2], ((0, 0), (0, LEN_S - 1), (0, 0))).reshape(T_SAMPLE, CONV_W)
        cz2 = jnp.pad(cache, ((0, 0), (0, LEN_S - 2), (0, 0))).reshape(T_SAMPLE, CONV_W)
        uvp, uvs, mixb, ptail, zs = _inproj_call(
            xp, xs, s_block, norm_mix[l].reshape(1, D_MODEL), w_in[l].astype(BF16), w_conv[l],
            norm_b[l].reshape(1, CONV_W), cz1, cz2)

        yvp, yvs, pst_r, pst_i, sst_r, sst_i = _ssm_call(
            uvp, uvs, abd[l], cac[l], kc[l], al_re[l], al_im[l],
            state_ssm_re[l].reshape(N_SEQ_S, ST_W), state_ssm_im[l].reshape(N_SEQ_S, ST_W))

        wrt = jnp.concatenate([
            w_router_expert[l].transpose(0, 2, 1).reshape(N_EXP, D_MODEL),
            w_router_group[l].T,
            jnp.zeros((RT_ROWS - N_EXP - N_EGRP, D_MODEL), F32)], axis=0)
        brt = jnp.concatenate([b_router_expert[l].reshape(N_EXP), b_router_group[l],
                               jnp.zeros((RT_ROWS - N_EXP - N_EGRP,), F32)]).reshape(RT_ROWS, 1)
        gffn = norm_ffn[l].reshape(1, D_MODEL)
        xmid, ridx, rgate, cnt = _post_call(
            xp, xs, s_block, yvp, yvs, mixb, w_glu[l].astype(BF16), b_glu[l].reshape(1, SSM_W),
            norm_a[l].reshape(1, SSM_W), w_out[l].astype(BF16), gffn, wrt, brt)

        offs, cstart, te, n_used, ts = _tile_tables(cnt[:, 0])
        pos0, pos1, cpos0, cpos1 = _sorted_positions(ridx, offs, cstart)
        csrc = _invert_call(cpos0, cpos1)
        r = _expert_call(l, te, n_used, ts, csrc, xmid, gffn, w_gate, w_up, w_down)
        final = l == DEPTH - 1
        out = _combine_call(pos0, pos1, xmid, rgate[:2].T, gfin, r, final=final)
        if final:
            y_prompt, y_sample = out
        else:
            xp = xs = out
            s_block = NB - 1

        p_re.append(pst_r.reshape(1, N_G, N_ST))
        p_im.append(pst_i.reshape(1, N_G, N_ST))
        p_buf.append(ptail[SUBLANES - 2:].reshape(1, 2, CONV_W))
        s_re.append(sst_r.reshape(N_SEQ_S, N_G, N_ST))
        s_im.append(sst_i.reshape(N_SEQ_S, N_G, N_ST))
        s_buf.append(zs.reshape(N_SEQ_S, LEN_S, CONV_W)[:, LEN_S - 2:])

    return (y_prompt.reshape(1, T_PROMPT, D_MODEL), y_sample.reshape(N_SEQ_S, LEN_S, D_MODEL),
            jnp.stack(p_re), jnp.stack(p_im), jnp.stack(p_buf),
            jnp.stack(s_re), jnp.stack(s_im), jnp.stack(s_buf))
```

```python
import functools

import jax
import jax.numpy as jnp
from jax import lax
from jax.experimental import pallas as pl
from jax.experimental.pallas import tpu as pltpu

F32 = jnp.float32
BF16 = jnp.bfloat16
I32 = jnp.int32

D_MODEL = 1024
DEPTH = 2
T_PROMPT = 16384
N_SEQ_S = 32
LEN_S = 16
T_SAMPLE = N_SEQ_S * LEN_S
T_ALL = T_PROMPT + T_SAMPLE
SSM_W = 512
CONV_W = 512
N_G = 32
G_CH = 16
N_ST = 64
ST_W = N_G * N_ST
N_EGRP = 4
EPG = 8
N_EXP = 32
D_FF = 512
EPS = 1e-6

LANES = 128
SUBLANES = 8
TB = 512
NB = T_ALL // TB
NBP = T_PROMPT // TB
CHUNK = 8
CRB = TB // CHUNK
GPS = LANES // G_CH
N_SG = SSM_W // LANES
SG_ST = GPS * N_ST
CW = CHUNK * LANES
UVW = CHUNK * SSM_W
CR_P = T_PROMPT // CHUNK
CR_S = T_SAMPLE // CHUNK
CPS = LEN_S // CHUNK
CB = 256
NPB = CR_P // CB
TM = 256
P_CAP = 2 * T_ALL + N_EXP * TM
NT = P_CAP // TM
N_PAIRS = 2 * T_ALL
TAIL = TB
TAIL_X = TAIL + SUBLANES
TAIL_CR = TAIL // CHUNK
N_DMA_THREADS = 2
ISSUE_UNROLL = 8
RT_ROWS = 40

VMEM_LIMIT = 56 * 1024 * 1024

assert T_SAMPLE == TB and TB % CHUNK == 0 and CR_S <= CB and LEN_S % CHUNK == 0


def _rms(x, g):
    return x * lax.rsqrt(jnp.mean(x * x, axis=-1, keepdims=True) + EPS) * g


def _sigmoid(x):
    return 1.0 / (1.0 + jnp.exp(-x))


def _gelu(y):
    return 0.5 * y * (1.0 + jnp.tanh(0.7978845608028654 * (y + 0.044715 * (y * y * y))))


def _put_token_tiles(ref, x):
    rows = x.shape[0]
    for j in range(D_MODEL // LANES):
        ref[pl.ds(j, rows, stride=SUBLANES), :] = x[:, j * LANES:(j + 1) * LANES]


def _get_token_tiles(ref, rows):
    return jnp.concatenate(
        [ref[pl.ds(j, rows, stride=SUBLANES), :] for j in range(D_MODEL // LANES)], axis=1)


def _stream_specs(s_block):
    return [pl.BlockSpec((TB, D_MODEL), lambda i: (jnp.minimum(i, NBP - 1), 0)),
            pl.BlockSpec((TB, D_MODEL), lambda i: (s_block, 0))]


def _inproj_kernel(xp_ref, xs_ref, *rest):
    is_sample = pl.program_id(0) == NB - 1
    _inproj_body(jnp.where(is_sample, xs_ref[...], xp_ref[...]), *rest)


def _inproj_body(x, gmix_ref, win_ref, wconv_ref, gb_ref, cz1_ref, cz2_ref,
                 uvp_ref, uvs_ref, mixb_ref, ptail_ref, zs_ref, carry_ref, u_scr):
    i = pl.program_id(0)
    is_sample = i == NB - 1

    @pl.when(i == 0)
    def _():
        carry_ref[...] = jnp.zeros_like(carry_ref)

    h = _rms(x, gmix_ref[...]).astype(BF16)
    proj = jnp.dot(h, win_ref[...], preferred_element_type=F32)
    for k in range(N_SG):
        u_scr[k] = proj[:, k * LANES:(k + 1) * LANES]
    gate_b = proj[:, SSM_W:SSM_W + CONV_W]
    gate_c = proj[:, SSM_W + CONV_W:SSM_W + 2 * CONV_W]
    v = proj[:, SSM_W + 2 * CONV_W:]
    z = gate_c * v
    row = lax.broadcasted_iota(I32, (TB, 1), 0)
    r1 = pltpu.roll(z, 1, 0)
    r2 = pltpu.roll(z, 2, 0)
    c6 = carry_ref[6:7, :]
    c7 = carry_ref[7:8, :]
    z1p = jnp.where(row == 0, c7, r1)
    z2p = jnp.where(row == 0, c6, jnp.where(row == 1, c7, r2))
    pos = row % LEN_S
    z1s = jnp.where(pos == 0, cz1_ref[...], r1)
    z2s = jnp.where(pos < 2, cz2_ref[...], r2)
    z1 = jnp.where(is_sample, z1s, z1p)
    z2 = jnp.where(is_sample, z2s, z2p)
    w = wconv_ref[...]
    conv = w[0:1, :] * z2 + w[1:2, :] * z1 + w[2:3, :] * z
    yb = gate_b * conv
    mixb_ref[...] = _rms(yb, gb_ref[...]).astype(BF16)
    carry_ref[...] = z[TB - SUBLANES:, :]

    def put_chunk_rows(uv_ref):
        for t in range(CHUNK):
            for k in range(N_SG):
                uv_ref[:, t * SSM_W + k * LANES:t * SSM_W + (k + 1) * LANES] = (
                    u_scr[k, pl.ds(t, CRB, stride=CHUNK), :])

    @pl.when(jnp.logical_not(is_sample))
    def _():
        put_chunk_rows(uvp_ref)

    @pl.when(i == NB - 2)
    def _():
        ptail_ref[...] = z[TB - SUBLANES:, :]

    @pl.when(is_sample)
    def _():
        put_chunk_rows(uvs_ref)
        zs_ref[...] = z


def _inproj_specs():
    const = lambda i: (0, 0)
    in_specs = [
        pl.BlockSpec((1, D_MODEL), const),
        pl.BlockSpec((D_MODEL, 4 * SSM_W), const),
        pl.BlockSpec((3, CONV_W), const),
        pl.BlockSpec((1, CONV_W), const),
        pl.BlockSpec((TB, CONV_W), const),
        pl.BlockSpec((TB, CONV_W), const),
    ]
    out_specs = [
        pl.BlockSpec((CRB, UVW), lambda i: (jnp.minimum(i, NBP - 1), 0)),
        pl.BlockSpec((CRB, UVW), const),
        pl.BlockSpec((TB, CONV_W), lambda i: (i, 0)),
        pl.BlockSpec((SUBLANES, CONV_W), const),
        pl.BlockSpec((TB, CONV_W), const),
    ]
    out_shape = [
        jax.ShapeDtypeStruct((CR_P, UVW), F32),
        jax.ShapeDtypeStruct((CR_S, UVW), F32),
        jax.ShapeDtypeStruct((T_ALL, CONV_W), BF16),
        jax.ShapeDtypeStruct((SUBLANES, CONV_W), F32),
        jax.ShapeDtypeStruct((TB, CONV_W), F32),
    ]
    scratch = [pltpu.VMEM((SUBLANES, CONV_W), F32), pltpu.VMEM((N_SG, TB, LANES), F32)]
    return in_specs, out_specs, out_shape, scratch


def _inproj_call(xp, xs, s_block, gmix, win, wconv, gb, cz1, cz2):
    in_specs, out_specs, out_shape, scratch = _inproj_specs()
    return pl.pallas_call(
        _inproj_kernel,
        grid=(NB,),
        in_specs=_stream_specs(s_block) + in_specs,
        out_specs=out_specs,
        out_shape=out_shape,
        scratch_shapes=scratch,
        compiler_params=pltpu.CompilerParams(
            dimension_semantics=("arbitrary",), vmem_limit_bytes=VMEM_LIMIT),
        name="inproj",
    )(xp, xs, gmix, win, wconv, gb, cz1, cz2)


def _issue_row_gather(pos0_ref, pos1_ref, r_ref, dst_of, sem):
    def issue(r, _):
        for k, pos_ref in enumerate((pos0_ref, pos1_ref)):
            p = pos_ref[r]
            src = r_ref.at[pl.ds(pl.multiple_of(p * SUBLANES, SUBLANES), SUBLANES)]
            dst = dst_of(k).at[pl.ds(pl.multiple_of(r * SUBLANES, SUBLANES), SUBLANES)]
            pltpu.make_async_copy(src, dst, sem).start(priority=k % N_DMA_THREADS)
        return 0

    lax.fori_loop(0, TB, issue, 0, unroll=ISSUE_UNROLL)


def _combine_inproj_kernel(pos0_ref, pos1_ref, npos0_ref, npos1_ref, xmid_ref, gate_ref, r_ref,
                           gmix_ref, win_ref, wconv_ref, gb_ref, cz1_ref, cz2_ref,
                           x_ref, uvp_ref, uvs_ref, mixb_ref, ptail_ref, zs_ref,
                           buf, sems, carry_ref, u_scr):
    x = _combine_rows(pos0_ref, pos1_ref, npos0_ref, npos1_ref, xmid_ref, gate_ref, r_ref,
                      buf, sems)
    x_ref[...] = x
    _inproj_body(x, gmix_ref, win_ref, wconv_ref, gb_ref, cz1_ref, cz2_ref,
                 uvp_ref, uvs_ref, mixb_ref, ptail_ref, zs_ref, carry_ref, u_scr)


def _combine_inproj_call(pos0, pos1, xmid, gate_t, r, gmix, win, wconv, gb, cz1, cz2):
    in_specs, out_specs, out_shape, scratch = _inproj_specs()
    c_specs, c_scratch = _combine_specs()
    return pl.pallas_call(
        _combine_inproj_kernel,
        grid=(NB,),
        in_specs=c_specs + in_specs,
        out_specs=[pl.BlockSpec((TB, D_MODEL), lambda i: (i, 0))] + out_specs,
        out_shape=[jax.ShapeDtypeStruct((T_ALL, D_MODEL), F32)] + out_shape,
        scratch_shapes=c_scratch + scratch,
        compiler_params=pltpu.CompilerParams(
            dimension_semantics=("arbitrary",), vmem_limit_bytes=VMEM_LIMIT),
        name="combine_inproj",
    )(pos0, pos1, pos0, pos1, xmid, gate_t, r, gmix, win, wconv, gb, cz1, cz2)


def _expand_chunk_matrices(abd_ref, cac_ref, kc_ref, ws, wy, kb):
    for s in range(N_SG):
        _expand_ws(abd_ref, s, ws.at[s])
        _expand_wy(cac_ref, s, wy.at[s])
        _expand_kb(kc_ref, s, kb.at[s])


def _bf16_part(a, part):
    if part is None:
        return a
    hi = a.astype(BF16)
    return hi if part == "hi" else (a - hi.astype(F32)).astype(BF16)


def _expand_ws(abd_ref, s, dst, part=None):
    row_g = lax.broadcasted_iota(I32, (LANES, LANES), 0) // G_CH
    lane_half = lax.broadcasted_iota(I32, (LANES, LANES), 1) // N_ST
    for t in range(CHUNK):
        for r in range(2):
            a = _bf16_part(abd_ref[s, t, r], part)
            for m in range(GPS // 2):
                dst[t * LANES:(t + 1) * LANES,
                    r * SG_ST + m * LANES:r * SG_ST + (m + 1) * LANES] = jnp.where(
                        row_g == 2 * m + lane_half, a, jnp.zeros_like(a))


def _expand_wy(cac_ref, s, dst, part=None):
    col_g = lax.broadcasted_iota(I32, (N_ST, LANES), 1) // G_CH
    for t in range(CHUNK):
        for r in range(2):
            c = _bf16_part(cac_ref[s, t, r], part)
            for g in range(GPS):
                dst[r * SG_ST + g * N_ST:r * SG_ST + (g + 1) * N_ST,
                    t * LANES:(t + 1) * LANES] = jnp.where(col_g == g, c, jnp.zeros_like(c))


def _expand_kb(kc_ref, s, dst, part=None):
    row_g = lax.broadcasted_iota(I32, (LANES, LANES), 0) // G_CH
    col_g = lax.broadcasted_iota(I32, (LANES, LANES), 1) // G_CH
    zero_blk = jnp.zeros((LANES, LANES), BF16)
    for k in range(CHUNK):
        kq = _bf16_part(kc_ref[s, k], part)
        blk = jnp.where(row_g == col_g, jnp.concatenate([kq] * GPS, axis=0), zero_blk)
        for tp in range(CHUNK - k):
            t = tp + k
            dst[tp * LANES:(tp + 1) * LANES, t * LANES:(t + 1) * LANES] = blk
    for tp in range(CHUNK):
        for t in range(tp):
            dst[tp * LANES:(tp + 1) * LANES, t * LANES:(t + 1) * LANES] = zero_blk


def _ucat(uv_ref, sg):
    parts = [uv_ref[:, t * SSM_W + sg * LANES:t * SSM_W + (sg + 1) * LANES] for t in range(CHUNK)]
    return jnp.concatenate(parts, axis=1).astype(BF16)


def _ssm_local(uv_ref, ws, sre, sim):
    for sg in range(N_SG):
        s = jnp.dot(_ucat(uv_ref, sg), ws[sg], preferred_element_type=F32)
        sre[:, sg * SG_ST:(sg + 1) * SG_ST] = s[:, :SG_ST]
        sim[:, sg * SG_ST:(sg + 1) * SG_ST] = s[:, SG_ST:]


def _ssm_out(uv_ref, wy, kb, sre, sim, yv_ref):
    for sg in range(N_SG):
        sp = jnp.concatenate([sre[:, sg * SG_ST:(sg + 1) * SG_ST],
                              sim[:, sg * SG_ST:(sg + 1) * SG_ST]], axis=1).astype(BF16)
        y = (jnp.dot(sp, wy[sg], preferred_element_type=F32)
             + jnp.dot(_ucat(uv_ref, sg), kb[sg], preferred_element_type=F32))
        for t in range(CHUNK):
            yv_ref[:, t * SSM_W + sg * LANES:t * SSM_W + (sg + 1) * LANES] = (
                y[:, t * LANES:(t + 1) * LANES])


def _ssm_kernel(uvp_ref, uvs_ref, abd_ref, cac_ref, kc_ref, alr_ref, ali_ref, h0r_ref, h0i_ref,
                yvp_ref, yvs_ref, pstr_ref, psti_ref, sstr_ref, ssti_ref, tstr_ref, tsti_ref,
                ws, wy, kb, sre, sim, cre, cim):
    i = pl.program_id(0)

    @pl.when(i == 0)
    def _():
        cre[...] = jnp.zeros_like(cre)
        cim[...] = jnp.zeros_like(cim)
        _expand_chunk_matrices(abd_ref, cac_ref, kc_ref, ws, wy, kb)

    ar = alr_ref[...]
    ai = ali_ref[...]

    @pl.when(i < NPB)
    def _():
        _ssm_local(uvp_ref, ws, sre, sim)

        def step(c, carry):
            sr, si = carry
            lr = sre[pl.ds(c, 1), :]
            li = sim[pl.ds(c, 1), :]
            sre[pl.ds(c, 1), :] = sr
            sim[pl.ds(c, 1), :] = si
            return ar * sr - ai * si + lr, ar * si + ai * sr + li

        sr, si = lax.fori_loop(0, CB, step, (cre[...], cim[...]))
        cre[...] = sr
        cim[...] = si
        pstr_ref[...] = sr
        psti_ref[...] = si
        tstr_ref[...] = sre[CB - TAIL_CR:CB - TAIL_CR + 1, :]
        tsti_ref[...] = sim[CB - TAIL_CR:CB - TAIL_CR + 1, :]
        _ssm_out(uvp_ref, wy, kb, sre, sim, yvp_ref)

    @pl.when(i == NPB)
    def _():
        sre_s = sre.at[0:CR_S]
        sim_s = sim.at[0:CR_S]
        _ssm_local(uvs_ref, ws, sre_s, sim_s)

        def per_stream(q, _):
            sr = h0r_ref[pl.ds(q, 1), :]
            si = h0i_ref[pl.ds(q, 1), :]
            for k in range(CPS):
                c = q * CPS + k
                lr = sre_s[pl.ds(c, 1), :]
                li = sim_s[pl.ds(c, 1), :]
                sre_s[pl.ds(c, 1), :] = sr
                sim_s[pl.ds(c, 1), :] = si
                sr, si = ar * sr - ai * si + lr, ar * si + ai * sr + li
            sstr_ref[pl.ds(q, 1), :] = sr
            ssti_ref[pl.ds(q, 1), :] = si
            return 0

        lax.fori_loop(0, N_SEQ_S, per_stream, 0)
        _ssm_out(uvs_ref, wy, kb, sre_s, sim_s, yvs_ref)


def _ssm_call(uvp, uvs, abd, cac, kc, alr, ali, h0r, h0i):
    c2 = lambda i: (0, 0)
    c4 = lambda i: (0, 0, 0, 0)
    c5 = lambda i: (0, 0, 0, 0, 0)
    one = pl.Buffered(1)
    pblk = lambda i: (jnp.minimum(i, NPB - 1), 0)
    return pl.pallas_call(
        _ssm_kernel,
        grid=(NPB + 1,),
        in_specs=[pl.BlockSpec((CB, UVW), pblk),
                  pl.BlockSpec((CR_S, UVW), c2),
                  pl.BlockSpec((N_SG, CHUNK, 2, LANES, LANES), c5, pipeline_mode=one),
                  pl.BlockSpec((N_SG, CHUNK, 2, N_ST, LANES), c5, pipeline_mode=one),
                  pl.BlockSpec((N_SG, CHUNK, G_CH, LANES), c4, pipeline_mode=one),
                  pl.BlockSpec((1, ST_W), c2),
                  pl.BlockSpec((1, ST_W), c2),
                  pl.BlockSpec((N_SEQ_S, ST_W), c2),
                  pl.BlockSpec((N_SEQ_S, ST_W), c2)],
        out_specs=[pl.BlockSpec((CB, UVW), pblk),
                   pl.BlockSpec((CR_S, UVW), c2),
                   pl.BlockSpec((1, ST_W), c2),
                   pl.BlockSpec((1, ST_W), c2),
                   pl.BlockSpec((N_SEQ_S, ST_W), c2),
                   pl.BlockSpec((N_SEQ_S, ST_W), c2),
                   pl.BlockSpec((1, ST_W), c2),
                   pl.BlockSpec((1, ST_W), c2)],
        out_shape=[jax.ShapeDtypeStruct((CR_P, UVW), F32),
                   jax.ShapeDtypeStruct((CR_S, UVW), F32),
                   jax.ShapeDtypeStruct((1, ST_W), F32),
                   jax.ShapeDtypeStruct((1, ST_W), F32),
                   jax.ShapeDtypeStruct((N_SEQ_S, ST_W), F32),
                   jax.ShapeDtypeStruct((N_SEQ_S, ST_W), F32),
                   jax.ShapeDtypeStruct((1, ST_W), F32),
                   jax.ShapeDtypeStruct((1, ST_W), F32)],
        scratch_shapes=[pltpu.VMEM((N_SG, CW, 2 * SG_ST), BF16),
                        pltpu.VMEM((N_SG, 2 * SG_ST, CW), BF16),
                        pltpu.VMEM((N_SG, CW, CW), BF16),
                        pltpu.VMEM((CB, ST_W), F32), pltpu.VMEM((CB, ST_W), F32),
                        pltpu.VMEM((1, ST_W), F32), pltpu.VMEM((1, ST_W), F32)],
        compiler_params=pltpu.CompilerParams(
            dimension_semantics=("arbitrary",), vmem_limit_bytes=VMEM_LIMIT),
        name="ssm",
    )(uvp, uvs, abd, cac, kc, alr, ali, h0r, h0i)


def _split(a):
    hi = a.astype(BF16)
    return hi, (a - hi.astype(F32)).astype(BF16)


def _dot3(a, b):
    ah, al = _split(a)
    bh, bl = _split(b)
    return (jnp.dot(ah, bh, preferred_element_type=F32)
            + jnp.dot(ah, bl, preferred_element_type=F32)
            + jnp.dot(al, bh, preferred_element_type=F32))


def _tail_kernel(xt_ref, gmix_ref, win_ref, wconv_ref, gb_ref, abd_ref, cac_ref, kc_ref,
                 alr_ref, ali_ref, tsr_ref, tsi_ref, wglu_ref, bglu_ref, ga_ref, wout_ref,
                 xmt_ref, mh, ml, u_scr, sre, sim, y_scr):
    x = xt_ref[...]
    proj = _dot3(_rms(x, gmix_ref[...]), win_ref[...])
    z = proj[:, SSM_W + CONV_W:SSM_W + 2 * CONV_W] * proj[:, SSM_W + 2 * CONV_W:]
    w = wconv_ref[...]
    conv = w[0:1, :] * pltpu.roll(z, 2, 0) + w[1:2, :] * pltpu.roll(z, 1, 0) + w[2:3, :] * z
    nb = _rms(proj[SUBLANES:, SSM_W:SSM_W + CONV_W] * conv[SUBLANES:], gb_ref[...])
    for k in range(N_SG):
        u_scr[k] = proj[SUBLANES:, k * LANES:(k + 1) * LANES]

    def ucat(sg):
        return jnp.concatenate(
            [u_scr[sg, pl.ds(t, TAIL_CR, stride=CHUNK), :] for t in range(CHUNK)], axis=1)

    def dot3m(a):
        ah, al = _split(a)
        return (jnp.dot(ah, mh[...], preferred_element_type=F32)
                + jnp.dot(ah, ml[...], preferred_element_type=F32)
                + jnp.dot(al, mh[...], preferred_element_type=F32))

    for sg in range(N_SG):
        _expand_ws(abd_ref, sg, mh, "hi")
        _expand_ws(abd_ref, sg, ml, "lo")
        s = dot3m(ucat(sg))
        sre[:, sg * SG_ST:(sg + 1) * SG_ST] = s[:, :SG_ST]
        sim[:, sg * SG_ST:(sg + 1) * SG_ST] = s[:, SG_ST:]

    ar = alr_ref[...]
    ai = ali_ref[...]

    def step(c, carry):
        sr, si = carry
        lr = sre[pl.ds(c, 1), :]
        li = sim[pl.ds(c, 1), :]
        sre[pl.ds(c, 1), :] = sr
        sim[pl.ds(c, 1), :] = si
        return ar * sr - ai * si + lr, ar * si + ai * sr + li

    lax.fori_loop(0, TAIL_CR, step, (tsr_ref[...], tsi_ref[...]))

    for sg in range(N_SG):
        sp = jnp.concatenate([sre[:, sg * SG_ST:(sg + 1) * SG_ST],
                              sim[:, sg * SG_ST:(sg + 1) * SG_ST]], axis=1)
        _expand_wy(cac_ref, sg, mh, "hi")
        _expand_wy(cac_ref, sg, ml, "lo")
        y = dot3m(sp)
        _expand_kb(kc_ref, sg, mh, "hi")
        _expand_kb(kc_ref, sg, ml, "lo")
        y = y + dot3m(ucat(sg))
        for t in range(CHUNK):
            y_scr[sg, pl.ds(t, TAIL_CR, stride=CHUNK), :] = y[:, t * LANES:(t + 1) * LANES]

    zg = _gelu(jnp.concatenate([y_scr[k] for k in range(N_SG)], axis=1))
    out_a = zg * _sigmoid(_dot3(zg, wglu_ref[...]) + bglu_ref[...])
    mix = jnp.concatenate([_rms(out_a, ga_ref[...]), nb], axis=1)
    xmt_ref[...] = x[SUBLANES:, :] + _dot3(mix, wout_ref[...])


def _tail_call(xt, gmix, win, wconv, gb, abd, cac, kc, alr, ali, tsr, tsi,
               wglu, bglu, ga, wout):
    def whole(a):
        nd = a.ndim
        return pl.BlockSpec(a.shape, lambda i: (0,) * nd, pipeline_mode=pl.Buffered(1))

    args = (xt, gmix, win, wconv, gb, abd, cac, kc, alr, ali, tsr, tsi, wglu, bglu, ga, wout)
    return pl.pallas_call(
        _tail_kernel,
        grid=(1,),
        in_specs=[whole(a) for a in args],
        out_specs=pl.BlockSpec((TAIL, D_MODEL), lambda i: (0, 0)),
        out_shape=jax.ShapeDtypeStruct((TAIL, D_MODEL), F32),
        scratch_shapes=[pltpu.VMEM((CW, CW), BF16), pltpu.VMEM((CW, CW), BF16),
                        pltpu.VMEM((N_SG, TAIL, LANES), F32),
                        pltpu.VMEM((TAIL_CR, ST_W), F32), pltpu.VMEM((TAIL_CR, ST_W), F32),
                        pltpu.VMEM((N_SG, TAIL, LANES), F32)],
        compiler_params=pltpu.CompilerParams(
            dimension_semantics=("arbitrary",), vmem_limit_bytes=VMEM_LIMIT),
        name="tail",
    )(*args)


def _post_kernel(xp_ref, xs_ref, yvp_ref, yvs_ref, mixb_ref, xmt_ref, wglu_ref, bglu_ref, ga_ref,
                 wout_ref, gffn_ref, wrt_ref, brt_ref,
                 xmid_ref, ridx_ref, rgate_ref, cnt_ref, base_ref, y_scr, *, tail_block):
    i = pl.program_id(0)
    is_sample = i == NB - 1

    @pl.when(i == 0)
    def _():
        base_ref[...] = jnp.zeros_like(base_ref)

    yv = jnp.where(is_sample, yvs_ref[...], yvp_ref[...])
    for t in range(CHUNK):
        for k in range(N_SG):
            y_scr[k, pl.ds(t, CRB, stride=CHUNK), :] = (
                yv[:, t * SSM_W + k * LANES:t * SSM_W + (k + 1) * LANES])
    y = jnp.concatenate([y_scr[k] for k in range(N_SG)], axis=1)
    z = _gelu(y)
    gl = jnp.dot(z.astype(BF16), wglu_ref[...], preferred_element_type=F32) + bglu_ref[...]
    out_a = z * _sigmoid(gl)
    mix = jnp.concatenate([_rms(out_a, ga_ref[...]).astype(BF16), mixb_ref[...]], axis=1)
    x = jnp.where(is_sample, xs_ref[...], xp_ref[...])
    xm = x + jnp.dot(mix, wout_ref[...], preferred_element_type=F32)
    if tail_block is not None:
        xm = jnp.where(i == tail_block, xmt_ref[...], xm)
    _put_token_tiles(xmid_ref, xm)
    hn = _rms(xm, gffn_ref[...])

    def dot_nt(a, b):
        return lax.dot_general(a, b, (((1,), (1,)), ((), ())), preferred_element_type=F32)

    h_hi, h_lo = _split(hn)
    w_hi, w_lo = _split(wrt_ref[...])
    logit = (dot_nt(w_hi, h_hi) + dot_nt(w_hi, h_lo) + dot_nt(w_lo, h_hi)
             + brt_ref[...])
    fine = logit[0:N_EXP, :]
    coarse = logit[N_EXP:N_EXP + N_EGRP, :]
    cmax = jnp.max(coarse, axis=0, keepdims=True)
    gi = lax.broadcasted_iota(I32, (N_EGRP, TB), 0).astype(F32)
    grp = jnp.min(jnp.where(coarse == cmax, gi, float(N_EGRP)), axis=0, keepdims=True)
    pg = 1.0 / jnp.sum(jnp.exp(coarse - cmax), axis=0, keepdims=True)
    eidx = lax.broadcasted_iota(I32, (N_EXP, TB), 0)
    ei = eidx.astype(F32)
    egrp = (eidx // EPG).astype(F32)
    neg = jnp.float32(-jnp.inf)
    fm = jnp.where(egrp == grp, fine, neg)
    v0 = jnp.max(fm, axis=0, keepdims=True)
    e0 = jnp.min(jnp.where(fm == v0, ei, float(N_EXP)), axis=0, keepdims=True)
    fm2 = jnp.where(ei == e0, neg, fm)
    v1 = jnp.max(fm2, axis=0, keepdims=True)
    e1 = jnp.min(jnp.where(fm2 == v1, ei, float(N_EXP)), axis=0, keepdims=True)
    tt = jnp.exp(v1 - v0)
    g0 = pg / (1.0 + tt)
    g1 = pg * tt / (1.0 + tt)

    sel0 = ei == e0
    sel1 = ei == e1
    cnt = jnp.where(sel0 | sel1, 1.0, 0.0)
    ta = lax.broadcasted_iota(I32, (TB, TB), 0)
    tb = lax.broadcasted_iota(I32, (TB, TB), 1)
    before = jnp.where(ta < tb, 1.0, 0.0).astype(BF16)
    cum = jnp.dot(cnt.astype(BF16), before, preferred_element_type=F32) + base_ref[...]
    rank0 = jnp.sum(jnp.where(sel0, cum, 0.0), axis=0, keepdims=True)
    rank1 = jnp.sum(jnp.where(sel1, cum, 0.0), axis=0, keepdims=True)
    base_new = base_ref[...] + jnp.sum(cnt, axis=1, keepdims=True)
    base_ref[...] = base_new
    cnt_ref[...] = jnp.broadcast_to(base_new, (N_EXP, LANES)).astype(I32)

    zi = jnp.zeros((1, TB), I32)
    ridx_ref[...] = jnp.concatenate(
        [e0.astype(I32), e1.astype(I32), rank0.astype(I32), rank1.astype(I32), zi, zi, zi, zi],
        axis=0)
    zf = jnp.zeros((1, TB), F32)
    rgate_ref[...] = jnp.concatenate([g0, g1, zf, zf, zf, zf, zf, zf], axis=0)


def _post_call(xp, xs, s_block, yvp, yvs, mixb, xmt, tail_block, wglu, bglu, ga, wout, gffn,
               wrt, brt):
    const = lambda i: (0, 0)
    return pl.pallas_call(
        functools.partial(_post_kernel, tail_block=tail_block),
        grid=(NB,),
        in_specs=_stream_specs(s_block) + [
            pl.BlockSpec((CRB, UVW), lambda i: (jnp.minimum(i, NBP - 1), 0)),
            pl.BlockSpec((CRB, UVW), const),
            pl.BlockSpec((TB, CONV_W), lambda i: (i, 0)),
            pl.BlockSpec((TB, D_MODEL), const),
            pl.BlockSpec((SSM_W, SSM_W), const),
            pl.BlockSpec((1, SSM_W), const),
            pl.BlockSpec((1, SSM_W), const),
            pl.BlockSpec((D_MODEL, D_MODEL), const),
            pl.BlockSpec((1, D_MODEL), const),
            pl.BlockSpec((RT_ROWS, D_MODEL), const),
            pl.BlockSpec((RT_ROWS, 1), const),
        ],
        out_specs=[
            pl.BlockSpec((TB * SUBLANES, LANES), lambda i: (i, 0)),
            pl.BlockSpec((SUBLANES, TB), lambda i: (0, i)),
            pl.BlockSpec((SUBLANES, TB), lambda i: (0, i)),
            pl.BlockSpec((N_EXP, LANES), const),
        ],
        out_shape=[
            jax.ShapeDtypeStruct((T_ALL * SUBLANES, LANES), F32),
            jax.ShapeDtypeStruct((SUBLANES, T_ALL), I32),
            jax.ShapeDtypeStruct((SUBLANES, T_ALL), F32),
            jax.ShapeDtypeStruct((N_EXP, LANES), I32),
        ],
        scratch_shapes=[pltpu.VMEM((N_EXP, 1), F32), pltpu.VMEM((N_SG, TB, LANES), F32)],
        compiler_params=pltpu.CompilerParams(
            dimension_semantics=("arbitrary",), vmem_limit_bytes=VMEM_LIMIT),
        name="post",
    )(xp, xs, yvp, yvs, mixb, xmt, wglu, bglu, ga, wout, gffn, wrt, brt)


def _invert_kernel(cpos0_ref, cpos1_ref, csrc_ref):
    i = pl.program_id(0)

    def put(r, _):
        tok = i * TB + r
        csrc_ref[cpos0_ref[r]] = tok
        csrc_ref[cpos1_ref[r]] = tok
        return 0

    lax.fori_loop(0, TB, put, 0, unroll=ISSUE_UNROLL)


def _invert_call(cpos0, cpos1):
    blk = pl.BlockSpec((TB,), lambda i: (i,), memory_space=pltpu.SMEM)
    return pl.pallas_call(
        _invert_kernel,
        grid=(NB,),
        in_specs=[blk, blk],
        out_specs=pl.BlockSpec(memory_space=pltpu.SMEM),
        out_shape=jax.ShapeDtypeStruct((N_PAIRS,), I32),
        compiler_params=pltpu.CompilerParams(dimension_semantics=("arbitrary",)),
        name="invert",
    )(cpos0, cpos1)


def _expert_kernel(te_ref, nu_ref, ts_ref, csrc_ref, xmid_ref, gffn_ref, wg_ref, wu_ref, wd_ref,
                   r_ref, wgb, wub, wdb, hbuf, xbuf, sems):
    i = pl.program_id(0)
    n_used = nu_ref[0]
    prev = te_ref[jnp.maximum(i - 1, 0)]
    fresh = (i == 0) | (te_ref[i] != prev)

    @pl.when(fresh)
    def _():
        wgb[...] = wg_ref[...].astype(BF16)
        wub[...] = wu_ref[...].astype(BF16)
        wdb[...] = wd_ref[...].astype(BF16)

    def gather_rows(tile, slot):
        base = ts_ref[tile]
        for r in range(TM):
            tok = csrc_ref[jnp.minimum(base + r, N_PAIRS - 1)]
            src = xmid_ref.at[pl.ds(pl.multiple_of(tok * SUBLANES, SUBLANES), SUBLANES)]
            dst = xbuf.at[slot, pl.ds(r * SUBLANES, SUBLANES)]
            pltpu.make_async_copy(src, dst, sems.at[slot]).start(priority=N_DMA_THREADS - 1)

    def wait_rows(slot):
        pltpu.make_async_copy(xbuf.at[slot], xbuf.at[slot], sems.at[slot]).wait()

    @pl.when(i == 0)
    def _():
        gather_rows(0, 0)

    used = i < n_used
    slot = i % 2

    @pl.when(used)
    def _():
        wait_rows(slot)
        hbuf[...] = _rms(_get_token_tiles(xbuf.at[slot], TM), gffn_ref[...]).astype(BF16)
        gather_rows(jnp.minimum(i + 1, n_used - 1), 1 - slot)
        hb = hbuf[...]
        g = jnp.dot(hb, wgb[...], preferred_element_type=F32)
        u = jnp.dot(hb, wub[...], preferred_element_type=F32)
        hid = (g * _sigmoid(g) * u).astype(BF16)
        _put_token_tiles(r_ref, jnp.dot(hid, wdb[...], preferred_element_type=F32))

    @pl.when(i == n_used - 1)
    def _():
        wait_rows(1 - slot)

    @pl.when(jnp.logical_not(used))
    def _():
        r_ref[...] = jnp.zeros_like(r_ref)


def _expert_call(layer, te, nu, ts, csrc, xmid, gffn, w_gate, w_up, w_down):
    def wmap(i, te, nu, ts, csrc):
        return (layer, te[i], 0, 0)

    return pl.pallas_call(
        _expert_kernel,
        grid_spec=pltpu.PrefetchScalarGridSpec(
            num_scalar_prefetch=4,
            grid=(NT,),
            in_specs=[
                pl.BlockSpec(memory_space=pl.ANY),
                pl.BlockSpec((1, D_MODEL), lambda i, *_: (0, 0)),
                pl.BlockSpec((None, None, D_MODEL, D_FF), wmap),
                pl.BlockSpec((None, None, D_MODEL, D_FF), wmap),
                pl.BlockSpec((None, None, D_FF, D_MODEL), wmap),
            ],
            out_specs=pl.BlockSpec((TM * SUBLANES, LANES), lambda i, *_: (i, 0)),
            scratch_shapes=[pltpu.VMEM((D_MODEL, D_FF), BF16),
                            pltpu.VMEM((D_MODEL, D_FF), BF16),
                            pltpu.VMEM((D_FF, D_MODEL), BF16),
                            pltpu.VMEM((TM, D_MODEL), BF16),
                            pltpu.VMEM((2, TM * SUBLANES, LANES), F32),
                            pltpu.SemaphoreType.DMA((2,))],
        ),
        out_shape=jax.ShapeDtypeStruct((P_CAP * SUBLANES, LANES), F32),
        compiler_params=pltpu.CompilerParams(
            dimension_semantics=("arbitrary",), vmem_limit_bytes=VMEM_LIMIT),
        name="experts",
    )(te, nu, ts, csrc, xmid, gffn, w_gate, w_up, w_down)


def _combine_rows(pos0_ref, pos1_ref, npos0_ref, npos1_ref, xmid_ref, gate_ref, r_ref, buf, sems):
    i = pl.program_id(0)
    slot = i % 2

    @pl.when(i == 0)
    def _():
        _issue_row_gather(pos0_ref, pos1_ref, r_ref, lambda k: buf.at[k], sems.at[0])

    for k in range(2):
        pltpu.make_async_copy(buf.at[k], buf.at[k], sems.at[slot]).wait()

    @pl.when(i < NB - 1)
    def _():
        other = 1 - slot
        _issue_row_gather(npos0_ref, npos1_ref, r_ref, lambda k: buf.at[2 * other + k],
                          sems.at[other])

    gt = gate_ref[...]
    return (_get_token_tiles(xmid_ref, TB)
            + gt[:, 0:1] * _get_token_tiles(buf.at[2 * slot], TB)
            + gt[:, 1:2] * _get_token_tiles(buf.at[2 * slot + 1], TB))


def _combine_specs():
    cur = pl.BlockSpec((TB,), lambda i: (i,), memory_space=pltpu.SMEM)
    nxt = pl.BlockSpec((TB,), lambda i: (jnp.minimum(i + 1, NB - 1),), memory_space=pltpu.SMEM)
    in_specs = [cur, cur, nxt, nxt,
                pl.BlockSpec((TB * SUBLANES, LANES), lambda i: (i, 0)),
                pl.BlockSpec((TB, 2), lambda i: (i, 0)),
                pl.BlockSpec(memory_space=pl.ANY)]
    scratch = [pltpu.VMEM((4, TB * SUBLANES, LANES), F32), pltpu.SemaphoreType.DMA((2,))]
    return in_specs, scratch


def _combine_final_kernel(pos0_ref, pos1_ref, npos0_ref, npos1_ref, xmid_ref, gate_ref, r_ref,
                          gfin_ref, outp_ref, outs_ref, buf, sems):
    i = pl.program_id(0)
    out = _rms(_combine_rows(pos0_ref, pos1_ref, npos0_ref, npos1_ref, xmid_ref, gate_ref, r_ref,
                             buf, sems), gfin_ref[...])

    @pl.when(i < NB - 1)
    def _():
        outp_ref[...] = out

    @pl.when(i == NB - 1)
    def _():
        outs_ref[...] = out


def _combine_final_call(pos0, pos1, xmid, gate_t, r, gfin):
    in_specs, scratch = _combine_specs()
    return pl.pallas_call(
        _combine_final_kernel,
        grid=(NB,),
        in_specs=in_specs + [pl.BlockSpec((1, D_MODEL), lambda i: (0, 0))],
        out_specs=[pl.BlockSpec((TB, D_MODEL), lambda i: (jnp.minimum(i, NBP - 1), 0)),
                   pl.BlockSpec((TB, D_MODEL), lambda i: (0, 0))],
        out_shape=[jax.ShapeDtypeStruct((T_PROMPT, D_MODEL), F32),
                   jax.ShapeDtypeStruct((T_SAMPLE, D_MODEL), F32)],
        scratch_shapes=scratch,
        compiler_params=pltpu.CompilerParams(
            dimension_semantics=("arbitrary",), vmem_limit_bytes=VMEM_LIMIT),
        name="combine_final",
    )(pos0, pos1, pos0, pos1, xmid, gate_t, r, gfin)


def _ssm_compact(lam_re, lam_im, log_dt, b_re, b_im, c_re, c_im, d_skip):
    nl = lam_re.shape[0]
    dt = jnp.exp(log_dt)[..., None]
    kpow = jnp.arange(CHUNK + 1, dtype=F32).reshape(CHUNK + 1, 1, 1, 1)
    mag = jnp.exp(kpow * (lam_re * dt))
    pw_re = mag * jnp.cos(kpow * (lam_im * dt))
    pw_im = mag * jnp.sin(kpow * (lam_im * dt))
    ab_re, ab_im = pw_re[1], pw_im[1]
    denom = lam_re * lam_re + lam_im * lam_im
    num_re = ab_re - 1.0
    f_re = (num_re * lam_re + ab_im * lam_im) / denom
    f_im = (ab_im * lam_re - num_re * lam_im) / denom
    bb_re = f_re[..., None] * b_re - f_im[..., None] * b_im
    bb_im = f_re[..., None] * b_im + f_im[..., None] * b_re

    krev = (CHUNK - 1.0) - kpow[:CHUNK]
    rmag = jnp.exp(krev * (lam_re * dt))
    rev_re = (rmag * jnp.cos(krev * (lam_im * dt)))[..., None]
    rev_im = (rmag * jnp.sin(krev * (lam_im * dt)))[..., None]
    ab = jnp.stack([rev_re * bb_re - rev_im * bb_im, rev_re * bb_im + rev_im * bb_re])
    ab = ab.reshape(2, CHUNK, nl, N_SG, GPS, N_ST, G_CH)
    ab = ab.transpose(2, 3, 1, 0, 4, 6, 5).reshape(nl, N_SG, CHUNK, 2, LANES, N_ST)
    abd = jnp.concatenate([ab, ab], axis=-1)

    pr = pw_re[:, :, :, None, :]
    pi = pw_im[:, :, :, None, :]
    ca_re = c_re * pr - c_im * pi
    ca_im = c_re * pi + c_im * pr
    ca = jnp.stack([ca_re[1:], -ca_im[1:]])
    ca = ca.reshape(2, CHUNK, nl, N_SG, GPS, G_CH, N_ST)
    cac = ca.transpose(2, 3, 1, 0, 6, 4, 5).reshape(nl, N_SG, CHUNK, 2, N_ST, LANES)

    bq_re = jnp.swapaxes(bb_re, -1, -2)[:, :, None]
    bq_im = jnp.swapaxes(bb_im, -1, -2)[:, :, None]
    kk = jnp.sum(ca_re[:CHUNK, :, :, :, None, :] * bq_re
                 - ca_im[:CHUNK, :, :, :, None, :] * bq_im, axis=-1)
    skip = d_skip.reshape(nl, N_G, G_CH)[..., None] * jnp.eye(G_CH, dtype=F32)
    kk = jnp.concatenate([kk[:1] + skip[None], kk[1:]], axis=0)
    kk = kk.reshape(CHUNK, nl, N_SG, GPS, G_CH, G_CH)
    kk = kk.transpose(1, 2, 0, 5, 3, 4)
    kc = kk.reshape(nl, N_SG, CHUNK, G_CH, LANES)

    al_re = pw_re[CHUNK].reshape(nl, 1, ST_W)
    al_im = pw_im[CHUNK].reshape(nl, 1, ST_W)
    return abd, cac, kc, al_re, al_im


def _tile_tables(counts):
    padded = ((counts + TM - 1) // TM) * TM
    ends = jnp.cumsum(padded)
    offs = ends - padded
    end_tile = ends // TM
    n_used = end_tile[-1].astype(I32)
    tile = jnp.arange(NT, dtype=I32)
    live = jnp.minimum(tile, n_used - 1)
    te = jnp.sum(end_tile[None, :] <= live[:, None], axis=1).astype(I32)
    cstart = jnp.cumsum(counts) - counts
    mine = te[:, None] == jnp.arange(N_EXP, dtype=I32)[None, :]
    first_row = jnp.sum(jnp.where(mine, (cstart - offs)[None, :], 0), axis=1)
    ts = (first_row + live * TM).astype(I32)
    return offs.astype(I32), cstart.astype(I32), te, n_used.reshape(1), ts


def _sorted_positions(ridx, offs, cstart):
    experts = jnp.arange(N_EXP, dtype=I32)[:, None]
    out = []
    for k in range(2):
        hit = ridx[k][None, :] == experts
        for table in (offs, cstart):
            out.append(jnp.sum(jnp.where(hit, table[:, None], 0), axis=0) + ridx[2 + k])
    pos0, cpos0, pos1, cpos1 = out
    return pos0, pos1, cpos0, cpos1


def kernel(x_prompt, x_sample, state_ssm_re, state_ssm_im, cache_conv, norm_mix, w_in, lam_re,
           lam_im, log_dt, ssm_b_re, ssm_b_im, ssm_c_re, ssm_c_im, ssm_d, w_glu, b_glu, w_conv,
           norm_a, norm_b, w_out, norm_ffn, w_router_group, b_router_group, w_router_expert,
           b_router_expert, w_gate, w_up, w_down, norm_final):
    xp = x_prompt.reshape(T_PROMPT, D_MODEL)
    xs = x_sample.reshape(T_SAMPLE, D_MODEL)
    s_block = 0
    gfin = norm_final.reshape(1, D_MODEL)
    abd, cac, kc, al_re, al_im = _ssm_compact(
        lam_re, lam_im, log_dt, ssm_b_re, ssm_b_im, ssm_c_re, ssm_c_im, ssm_d)
    p_re, p_im, p_buf, s_re, s_im, s_buf = [], [], [], [], [], []
    moe = None
    for l in range(DEPTH):
        cache = cache_conv[l]
        cz1 = jnp.pad(cache[:, 1:2], ((0, 0), (0, LEN_S - 1), (0, 0))).reshape(T_SAMPLE, CONV_W)
        cz2 = jnp.pad(cache, ((0, 0), (0, LEN_S - 2), (0, 0))).reshape(T_SAMPLE, CONV_W)
        inproj_params = (norm_mix[l].reshape(1, D_MODEL), w_in[l].astype(BF16), w_conv[l],
                         norm_b[l].reshape(1, CONV_W), cz1, cz2)
        if moe is None:
            uvp, uvs, mixb, ptail, zs = _inproj_call(xp, xs, s_block, *inproj_params)
        else:
            x, uvp, uvs, mixb, ptail, zs = _combine_inproj_call(*moe, *inproj_params)
            xp = xs = x
            s_block = NB - 1

        yvp, yvs, pst_r, pst_i, sst_r, sst_i, tst_r, tst_i = _ssm_call(
            uvp, uvs, abd[l].astype(BF16), cac[l].astype(BF16), kc[l].astype(BF16),
            al_re[l], al_im[l],
            state_ssm_re[l].reshape(N_SEQ_S, ST_W), state_ssm_im[l].reshape(N_SEQ_S, ST_W))

        tail_block = NBP - 1 if l < DEPTH - 1 else None
        if tail_block is not None:
            xmt = _tail_call(
                xp[T_PROMPT - TAIL_X:T_PROMPT], norm_mix[l].reshape(1, D_MODEL), w_in[l],
                w_conv[l], norm_b[l].reshape(1, CONV_W), abd[l], cac[l], kc[l],
                al_re[l], al_im[l], tst_r, tst_i, w_glu[l], b_glu[l].reshape(1, SSM_W),
                norm_a[l].reshape(1, SSM_W), w_out[l])

        wrt = jnp.concatenate([
            w_router_expert[l].transpose(0, 2, 1).reshape(N_EXP, D_MODEL),
            w_router_group[l].T,
            jnp.zeros((RT_ROWS - N_EXP - N_EGRP, D_MODEL), F32)], axis=0)
        brt = jnp.concatenate([b_router_expert[l].reshape(N_EXP), b_router_group[l],
                               jnp.zeros((RT_ROWS - N_EXP - N_EGRP,), F32)]).reshape(RT_ROWS, 1)
        gffn = norm_ffn[l].reshape(1, D_MODEL)
        xmid, ridx, rgate, cnt = _post_call(
            xp, xs, s_block, yvp, yvs, mixb, xmt, tail_block,
            w_glu[l].astype(BF16), b_glu[l].reshape(1, SSM_W),
            norm_a[l].reshape(1, SSM_W), w_out[l].astype(BF16), gffn, wrt, brt)

        offs, cstart, te, n_used, ts = _tile_tables(cnt[:, 0])
        pos0, pos1, cpos0, cpos1 = _sorted_positions(ridx, offs, cstart)
        csrc = _invert_call(cpos0, cpos1)
        r = _expert_call(l, te, n_used, ts, csrc, xmid, gffn, w_gate, w_up, w_down)
        moe = (pos0, pos1, xmid, rgate[:2].T, r)

        p_re.append(pst_r.reshape(1, N_G, N_ST))
        p_im.append(pst_i.reshape(1, N_G, N_ST))
        p_buf.append(ptail[SUBLANES - 2:].reshape(1, 2, CONV_W))
        s_re.append(sst_r.reshape(N_SEQ_S, N_G, N_ST))
        s_im.append(sst_i.reshape(N_SEQ_S, N_G, N_ST))
        s_buf.append(zs.reshape(N_SEQ_S, LEN_S, CONV_W)[:, LEN_S - 2:])

    y_prompt, y_sample = _combine_final_call(*moe, gfin)
    return (y_prompt.reshape(1, T_PROMPT, D_MODEL), y_sample.reshape(N_SEQ_S, LEN_S, D_MODEL),
            jnp.stack(p_re), jnp.stack(p_im), jnp.stack(p_buf),
            jnp.stack(s_re), jnp.stack(s_im), jnp.stack(s_buf))
```

```python
import functools

import jax
import jax.numpy as jnp
from jax import lax
from jax.experimental import pallas as pl
from jax.experimental.pallas import tpu as pltpu

F32 = jnp.float32
BF16 = jnp.bfloat16
I32 = jnp.int32

D_MODEL = 1024
DEPTH = 2
T_PROMPT = 16384
N_SEQ_S = 32
LEN_S = 16
T_SAMPLE = N_SEQ_S * LEN_S
T_ALL = T_PROMPT + T_SAMPLE
SSM_W = 512
CONV_W = 512
N_G = 32
G_CH = 16
N_ST = 64
ST_W = N_G * N_ST
N_EGRP = 4
EPG = 8
N_EXP = 32
D_FF = 512
EPS = 1e-6

LANES = 128
SUBLANES = 8
TB = 512
NB = T_ALL // TB
NBP = T_PROMPT // TB
CHUNK = 8
CRB = TB // CHUNK
GPS = LANES // G_CH
N_SG = SSM_W // LANES
SG_ST = GPS * N_ST
CW = CHUNK * LANES
UVW = CHUNK * SSM_W
CR_P = T_PROMPT // CHUNK
CR_S = T_SAMPLE // CHUNK
CPS = LEN_S // CHUNK
CB = 256
NPB = CR_P // CB
TM = 256
P_CAP = 2 * T_ALL + N_EXP * TM
NT = P_CAP // TM
N_PAIRS = 2 * T_ALL
TAIL = TB
TAIL_X = TAIL + SUBLANES
TAIL_CR = TAIL // CHUNK
N_DMA_THREADS = 2
ISSUE_UNROLL = 8
RT_ROWS = 40

VMEM_LIMIT = 56 * 1024 * 1024

assert T_SAMPLE == TB and TB % CHUNK == 0 and CR_S <= CB and LEN_S % CHUNK == 0


def _rms(x, g):
    return x * lax.rsqrt(jnp.mean(x * x, axis=-1, keepdims=True) + EPS) * g


def _sigmoid(x):
    return 1.0 / (1.0 + jnp.exp(-x))


def _gelu(y):
    return 0.5 * y * (1.0 + jnp.tanh(0.7978845608028654 * (y + 0.044715 * (y * y * y))))


def _put_token_tiles(ref, x):
    rows = x.shape[0]
    for j in range(D_MODEL // LANES):
        ref[pl.ds(j, rows, stride=SUBLANES), :] = x[:, j * LANES:(j + 1) * LANES]


def _get_token_tiles(ref, rows):
    return jnp.concatenate(
        [ref[pl.ds(j, rows, stride=SUBLANES), :] for j in range(D_MODEL // LANES)], axis=1)


def _stream_specs(s_block):
    return [pl.BlockSpec((TB, D_MODEL), lambda i: (jnp.minimum(i, NBP - 1), 0)),
            pl.BlockSpec((TB, D_MODEL), lambda i: (s_block, 0))]


def _inproj_kernel(xp_ref, xs_ref, *rest):
    is_sample = pl.program_id(0) == NB - 1
    _inproj_body(jnp.where(is_sample, xs_ref[...], xp_ref[...]), *rest)


def _inproj_body(x, gmix_ref, win_ref, wconv_ref, gb_ref, cz1_ref, cz2_ref,
                 uvp_ref, uvs_ref, mixb_ref, ptail_ref, zs_ref, carry_ref, u_scr):
    i = pl.program_id(0)
    is_sample = i == NB - 1

    @pl.when(i == 0)
    def _():
        carry_ref[...] = jnp.zeros_like(carry_ref)

    h = _rms(x, gmix_ref[...]).astype(BF16)
    proj = jnp.dot(h, win_ref[...], preferred_element_type=F32)
    for k in range(N_SG):
        u_scr[k] = proj[:, k * LANES:(k + 1) * LANES]
    gate_b = proj[:, SSM_W:SSM_W + CONV_W]
    gate_c = proj[:, SSM_W + CONV_W:SSM_W + 2 * CONV_W]
    v = proj[:, SSM_W + 2 * CONV_W:]
    z = gate_c * v
    row = lax.broadcasted_iota(I32, (TB, 1), 0)
    r1 = pltpu.roll(z, 1, 0)
    r2 = pltpu.roll(z, 2, 0)
    c6 = carry_ref[6:7, :]
    c7 = carry_ref[7:8, :]
    z1p = jnp.where(row == 0, c7, r1)
    z2p = jnp.where(row == 0, c6, jnp.where(row == 1, c7, r2))
    pos = row % LEN_S
    z1s = jnp.where(pos == 0, cz1_ref[...], r1)
    z2s = jnp.where(pos < 2, cz2_ref[...], r2)
    z1 = jnp.where(is_sample, z1s, z1p)
    z2 = jnp.where(is_sample, z2s, z2p)
    w = wconv_ref[...]
    conv = w[0:1, :] * z2 + w[1:2, :] * z1 + w[2:3, :] * z
    yb = gate_b * conv
    mixb_ref[...] = _rms(yb, gb_ref[...]).astype(BF16)
    carry_ref[...] = z[TB - SUBLANES:, :]

    def put_chunk_rows(uv_ref):
        for t in range(CHUNK):
            for k in range(N_SG):
                uv_ref[:, t * SSM_W + k * LANES:t * SSM_W + (k + 1) * LANES] = (
                    u_scr[k, pl.ds(t, CRB, stride=CHUNK), :])

    @pl.when(jnp.logical_not(is_sample))
    def _():
        put_chunk_rows(uvp_ref)

    @pl.when(i == NB - 2)
    def _():
        ptail_ref[...] = z[TB - SUBLANES:, :]

    @pl.when(is_sample)
    def _():
        put_chunk_rows(uvs_ref)
        zs_ref[...] = z


def _inproj_specs():
    const = lambda i: (0, 0)
    in_specs = [
        pl.BlockSpec((1, D_MODEL), const),
        pl.BlockSpec((D_MODEL, 4 * SSM_W), const),
        pl.BlockSpec((3, CONV_W), const),
        pl.BlockSpec((1, CONV_W), const),
        pl.BlockSpec((TB, CONV_W), const),
        pl.BlockSpec((TB, CONV_W), const),
    ]
    out_specs = [
        pl.BlockSpec((CRB, UVW), lambda i: (jnp.minimum(i, NBP - 1), 0)),
        pl.BlockSpec((CRB, UVW), const),
        pl.BlockSpec((TB, CONV_W), lambda i: (i, 0)),
        pl.BlockSpec((SUBLANES, CONV_W), const),
        pl.BlockSpec((TB, CONV_W), const),
    ]
    out_shape = [
        jax.ShapeDtypeStruct((CR_P, UVW), F32),
        jax.ShapeDtypeStruct((CR_S, UVW), F32),
        jax.ShapeDtypeStruct((T_ALL, CONV_W), BF16),
        jax.ShapeDtypeStruct((SUBLANES, CONV_W), F32),
        jax.ShapeDtypeStruct((TB, CONV_W), F32),
    ]
    scratch = [pltpu.VMEM((SUBLANES, CONV_W), F32), pltpu.VMEM((N_SG, TB, LANES), F32)]
    return in_specs, out_specs, out_shape, scratch


def _inproj_call(xp, xs, s_block, gmix, win, wconv, gb, cz1, cz2):
    in_specs, out_specs, out_shape, scratch = _inproj_specs()
    return pl.pallas_call(
        _inproj_kernel,
        grid=(NB,),
        in_specs=_stream_specs(s_block) + in_specs,
        out_specs=out_specs,
        out_shape=out_shape,
        scratch_shapes=scratch,
        compiler_params=pltpu.CompilerParams(
            dimension_semantics=("arbitrary",), vmem_limit_bytes=VMEM_LIMIT),
        name="inproj",
    )(xp, xs, gmix, win, wconv, gb, cz1, cz2)


def _issue_row_gather(pos0_ref, pos1_ref, r_ref, dst_of, sem):
    def issue(r, _):
        for k, pos_ref in enumerate((pos0_ref, pos1_ref)):
            p = pos_ref[r]
            src = r_ref.at[pl.ds(pl.multiple_of(p * SUBLANES, SUBLANES), SUBLANES)]
            dst = dst_of(k).at[pl.ds(pl.multiple_of(r * SUBLANES, SUBLANES), SUBLANES)]
            pltpu.make_async_copy(src, dst, sem).start(priority=k % N_DMA_THREADS)
        return 0

    lax.fori_loop(0, TB, issue, 0, unroll=ISSUE_UNROLL)


def _combine_inproj_kernel(pos0_ref, pos1_ref, npos0_ref, npos1_ref, xmid_ref, gate_ref, r_ref,
                           gmix_ref, win_ref, wconv_ref, gb_ref, cz1_ref, cz2_ref,
                           x_ref, uvp_ref, uvs_ref, mixb_ref, ptail_ref, zs_ref,
                           buf, sems, carry_ref, u_scr):
    x = _combine_rows(pos0_ref, pos1_ref, npos0_ref, npos1_ref, xmid_ref, gate_ref, r_ref,
                      buf, sems)
    x_ref[...] = x
    _inproj_body(x, gmix_ref, win_ref, wconv_ref, gb_ref, cz1_ref, cz2_ref,
                 uvp_ref, uvs_ref, mixb_ref, ptail_ref, zs_ref, carry_ref, u_scr)


def _combine_inproj_call(pos0, pos1, xmid, gate_t, r, gmix, win, wconv, gb, cz1, cz2):
    in_specs, out_specs, out_shape, scratch = _inproj_specs()
    c_specs, c_scratch = _combine_specs()
    return pl.pallas_call(
        _combine_inproj_kernel,
        grid=(NB,),
        in_specs=c_specs + in_specs,
        out_specs=[pl.BlockSpec((TB, D_MODEL), lambda i: (i, 0))] + out_specs,
        out_shape=[jax.ShapeDtypeStruct((T_ALL, D_MODEL), F32)] + out_shape,
        scratch_shapes=c_scratch + scratch,
        compiler_params=pltpu.CompilerParams(
            dimension_semantics=("arbitrary",), vmem_limit_bytes=VMEM_LIMIT),
        name="combine_inproj",
    )(pos0, pos1, pos0, pos1, xmid, gate_t, r, gmix, win, wconv, gb, cz1, cz2)


def _expand_chunk_matrices(abd_ref, cac_ref, kc_ref, ws, wy, kb):
    for s in range(N_SG):
        _expand_ws(abd_ref, s, ws.at[s])
        _expand_wy(cac_ref, s, wy.at[s])
        _expand_kb(kc_ref, s, kb.at[s])


def _bf16_part(a, part):
    if part is None:
        return a
    hi = a.astype(BF16)
    return hi if part == "hi" else (a - hi.astype(F32)).astype(BF16)


def _expand_ws(abd_ref, s, dst, part=None):
    row_g = lax.broadcasted_iota(I32, (LANES, LANES), 0) // G_CH
    lane_half = lax.broadcasted_iota(I32, (LANES, LANES), 1) // N_ST
    for t in range(CHUNK):
        for r in range(2):
            a = _bf16_part(abd_ref[s, t, r], part)
            for m in range(GPS // 2):
                dst[t * LANES:(t + 1) * LANES,
                    r * SG_ST + m * LANES:r * SG_ST + (m + 1) * LANES] = jnp.where(
                        row_g == 2 * m + lane_half, a, jnp.zeros_like(a))


def _expand_wy(cac_ref, s, dst, part=None):
    col_g = lax.broadcasted_iota(I32, (N_ST, LANES), 1) // G_CH
    for t in range(CHUNK):
        for r in range(2):
            c = _bf16_part(cac_ref[s, t, r], part)
            for g in range(GPS):
                dst[r * SG_ST + g * N_ST:r * SG_ST + (g + 1) * N_ST,
                    t * LANES:(t + 1) * LANES] = jnp.where(col_g == g, c, jnp.zeros_like(c))


def _expand_kb(kc_ref, s, dst, part=None):
    row_g = lax.broadcasted_iota(I32, (LANES, LANES), 0) // G_CH
    col_g = lax.broadcasted_iota(I32, (LANES, LANES), 1) // G_CH
    zero_blk = jnp.zeros((LANES, LANES), BF16)
    for k in range(CHUNK):
        kq = _bf16_part(kc_ref[s, k], part)
        blk = jnp.where(row_g == col_g, jnp.concatenate([kq] * GPS, axis=0), zero_blk)
        for tp in range(CHUNK - k):
            t = tp + k
            dst[tp * LANES:(tp + 1) * LANES, t * LANES:(t + 1) * LANES] = blk
    for tp in range(CHUNK):
        for t in range(tp):
            dst[tp * LANES:(tp + 1) * LANES, t * LANES:(t + 1) * LANES] = zero_blk


def _ucat(uv_ref, sg):
    parts = [uv_ref[:, t * SSM_W + sg * LANES:t * SSM_W + (sg + 1) * LANES] for t in range(CHUNK)]
    return jnp.concatenate(parts, axis=1).astype(BF16)


def _ssm_local(uv_ref, ws, sre, sim):
    for sg in range(N_SG):
        s = jnp.dot(_ucat(uv_ref, sg), ws[sg], preferred_element_type=F32)
        sre[:, sg * SG_ST:(sg + 1) * SG_ST] = s[:, :SG_ST]
        sim[:, sg * SG_ST:(sg + 1) * SG_ST] = s[:, SG_ST:]


def _ssm_out(uv_ref, wy, kb, sre, sim, yv_ref):
    for sg in range(N_SG):
        sp = jnp.concatenate([sre[:, sg * SG_ST:(sg + 1) * SG_ST],
                              sim[:, sg * SG_ST:(sg + 1) * SG_ST]], axis=1).astype(BF16)
        y = (jnp.dot(sp, wy[sg], preferred_element_type=F32)
             + jnp.dot(_ucat(uv_ref, sg), kb[sg], preferred_element_type=F32))
        for t in range(CHUNK):
            yv_ref[:, t * SSM_W + sg * LANES:t * SSM_W + (sg + 1) * LANES] = (
                y[:, t * LANES:(t + 1) * LANES])


def _ssm_kernel(uvp_ref, uvs_ref, abd_ref, cac_ref, kc_ref, alr_ref, ali_ref, h0r_ref, h0i_ref,
                yvp_ref, yvs_ref, pstr_ref, psti_ref, sstr_ref, ssti_ref, tstr_ref, tsti_ref,
                ws, wy, kb, sre, sim, cre, cim):
    i = pl.program_id(0)

    @pl.when(i == 0)
    def _():
        cre[...] = jnp.zeros_like(cre)
        cim[...] = jnp.zeros_like(cim)
        _expand_chunk_matrices(abd_ref, cac_ref, kc_ref, ws, wy, kb)

    ar = alr_ref[...]
    ai = ali_ref[...]

    @pl.when(i < NPB)
    def _():
        _ssm_local(uvp_ref, ws, sre, sim)

        def step(c, carry):
            sr, si = carry
            lr = sre[pl.ds(c, 1), :]
            li = sim[pl.ds(c, 1), :]
            sre[pl.ds(c, 1), :] = sr
            sim[pl.ds(c, 1), :] = si
            return ar * sr - ai * si + lr, ar * si + ai * sr + li

        sr, si = lax.fori_loop(0, CB, step, (cre[...], cim[...]))
        cre[...] = sr
        cim[...] = si
        pstr_ref[...] = sr
        psti_ref[...] = si
        tstr_ref[...] = sre[CB - TAIL_CR:CB - TAIL_CR + 1, :]
        tsti_ref[...] = sim[CB - TAIL_CR:CB - TAIL_CR + 1, :]
        _ssm_out(uvp_ref, wy, kb, sre, sim, yvp_ref)

    @pl.when(i == NPB)
    def _():
        sre_s = sre.at[0:CR_S]
        sim_s = sim.at[0:CR_S]
        _ssm_local(uvs_ref, ws, sre_s, sim_s)

        def per_stream(q, _):
            sr = h0r_ref[pl.ds(q, 1), :]
            si = h0i_ref[pl.ds(q, 1), :]
            for k in range(CPS):
                c = q * CPS + k
                lr = sre_s[pl.ds(c, 1), :]
                li = sim_s[pl.ds(c, 1), :]
                sre_s[pl.ds(c, 1), :] = sr
                sim_s[pl.ds(c, 1), :] = si
                sr, si = ar * sr - ai * si + lr, ar * si + ai * sr + li
            sstr_ref[pl.ds(q, 1), :] = sr
            ssti_ref[pl.ds(q, 1), :] = si
            return 0

        lax.fori_loop(0, N_SEQ_S, per_stream, 0)
        _ssm_out(uvs_ref, wy, kb, sre_s, sim_s, yvs_ref)


def _ssm_call(uvp, uvs, abd, cac, kc, alr, ali, h0r, h0i):
    c2 = lambda i: (0, 0)
    c4 = lambda i: (0, 0, 0, 0)
    c5 = lambda i: (0, 0, 0, 0, 0)
    one = pl.Buffered(1)
    pblk = lambda i: (jnp.minimum(i, NPB - 1), 0)
    return pl.pallas_call(
        _ssm_kernel,
        grid=(NPB + 1,),
        in_specs=[pl.BlockSpec((CB, UVW), pblk),
                  pl.BlockSpec((CR_S, UVW), c2),
                  pl.BlockSpec((N_SG, CHUNK, 2, LANES, LANES), c5, pipeline_mode=one),
                  pl.BlockSpec((N_SG, CHUNK, 2, N_ST, LANES), c5, pipeline_mode=one),
                  pl.BlockSpec((N_SG, CHUNK, G_CH, LANES), c4, pipeline_mode=one),
                  pl.BlockSpec((1, ST_W), c2),
                  pl.BlockSpec((1, ST_W), c2),
                  pl.BlockSpec((N_SEQ_S, ST_W), c2),
                  pl.BlockSpec((N_SEQ_S, ST_W), c2)],
        out_specs=[pl.BlockSpec((CB, UVW), pblk),
                   pl.BlockSpec((CR_S, UVW), c2),
                   pl.BlockSpec((1, ST_W), c2),
                   pl.BlockSpec((1, ST_W), c2),
                   pl.BlockSpec((N_SEQ_S, ST_W), c2),
                   pl.BlockSpec((N_SEQ_S, ST_W), c2),
                   pl.BlockSpec((1, ST_W), c2),
                   pl.BlockSpec((1, ST_W), c2)],
        out_shape=[jax.ShapeDtypeStruct((CR_P, UVW), F32),
                   jax.ShapeDtypeStruct((CR_S, UVW), F32),
                   jax.ShapeDtypeStruct((1, ST_W), F32),
                   jax.ShapeDtypeStruct((1, ST_W), F32),
                   jax.ShapeDtypeStruct((N_SEQ_S, ST_W), F32),
                   jax.ShapeDtypeStruct((N_SEQ_S, ST_W), F32),
                   jax.ShapeDtypeStruct((1, ST_W), F32),
                   jax.ShapeDtypeStruct((1, ST_W), F32)],
        scratch_shapes=[pltpu.VMEM((N_SG, CW, 2 * SG_ST), BF16),
                        pltpu.VMEM((N_SG, 2 * SG_ST, CW), BF16),
                        pltpu.VMEM((N_SG, CW, CW), BF16),
                        pltpu.VMEM((CB, ST_W), F32), pltpu.VMEM((CB, ST_W), F32),
                        pltpu.VMEM((1, ST_W), F32), pltpu.VMEM((1, ST_W), F32)],
        compiler_params=pltpu.CompilerParams(
            dimension_semantics=("arbitrary",), vmem_limit_bytes=VMEM_LIMIT),
        name="ssm",
    )(uvp, uvs, abd, cac, kc, alr, ali, h0r, h0i)


def _split(a):
    hi = a.astype(BF16)
    return hi, (a - hi.astype(F32)).astype(BF16)


def _dot3(a, b):
    ah, al = _split(a)
    bh, bl = _split(b)
    return (jnp.dot(ah, bh, preferred_element_type=F32)
            + jnp.dot(ah, bl, preferred_element_type=F32)
            + jnp.dot(al, bh, preferred_element_type=F32))


def _tail_kernel(xt_ref, gmix_ref, win_ref, wconv_ref, gb_ref, abd_ref, cac_ref, kc_ref,
                 alr_ref, ali_ref, tsr_ref, tsi_ref, wglu_ref, bglu_ref, ga_ref, wout_ref,
                 xmt_ref, mh, ml, u_scr, sre, sim, y_scr):
    x = xt_ref[...]
    proj = _dot3(_rms(x, gmix_ref[...]), win_ref[...])
    z = proj[:, SSM_W + CONV_W:SSM_W + 2 * CONV_W] * proj[:, SSM_W + 2 * CONV_W:]
    w = wconv_ref[...]
    conv = w[0:1, :] * pltpu.roll(z, 2, 0) + w[1:2, :] * pltpu.roll(z, 1, 0) + w[2:3, :] * z
    nb = _rms(proj[SUBLANES:, SSM_W:SSM_W + CONV_W] * conv[SUBLANES:], gb_ref[...])
    for k in range(N_SG):
        u_scr[k] = proj[SUBLANES:, k * LANES:(k + 1) * LANES]

    def ucat(sg):
        return jnp.concatenate(
            [u_scr[sg, pl.ds(t, TAIL_CR, stride=CHUNK), :] for t in range(CHUNK)], axis=1)

    def dot3m(a):
        ah, al = _split(a)
        return (jnp.dot(ah, mh[...], preferred_element_type=F32)
                + jnp.dot(ah, ml[...], preferred_element_type=F32)
                + jnp.dot(al, mh[...], preferred_element_type=F32))

    for sg in range(N_SG):
        _expand_ws(abd_ref, sg, mh, "hi")
        _expand_ws(abd_ref, sg, ml, "lo")
        s = dot3m(ucat(sg))
        sre[:, sg * SG_ST:(sg + 1) * SG_ST] = s[:, :SG_ST]
        sim[:, sg * SG_ST:(sg + 1) * SG_ST] = s[:, SG_ST:]

    ar = alr_ref[...]
    ai = ali_ref[...]

    def step(c, carry):
        sr, si = carry
        lr = sre[pl.ds(c, 1), :]
        li = sim[pl.ds(c, 1), :]
        sre[pl.ds(c, 1), :] = sr
        sim[pl.ds(c, 1), :] = si
        return ar * sr - ai * si + lr, ar * si + ai * sr + li

    lax.fori_loop(0, TAIL_CR, step, (tsr_ref[...], tsi_ref[...]))

    for sg in range(N_SG):
        sp = jnp.concatenate([sre[:, sg * SG_ST:(sg + 1) * SG_ST],
                              sim[:, sg * SG_ST:(sg + 1) * SG_ST]], axis=1)
        _expand_wy(cac_ref, sg, mh, "hi")
        _expand_wy(cac_ref, sg, ml, "lo")
        y = dot3m(sp)
        _expand_kb(kc_ref, sg, mh, "hi")
        _expand_kb(kc_ref, sg, ml, "lo")
        y = y + dot3m(ucat(sg))
        for t in range(CHUNK):
            y_scr[sg, pl.ds(t, TAIL_CR, stride=CHUNK), :] = y[:, t * LANES:(t + 1) * LANES]

    zg = _gelu(jnp.concatenate([y_scr[k] for k in range(N_SG)], axis=1))
    out_a = zg * _sigmoid(_dot3(zg, wglu_ref[...]) + bglu_ref[...])
    mix = jnp.concatenate([_rms(out_a, ga_ref[...]), nb], axis=1)
    xmt_ref[...] = x[SUBLANES:, :] + _dot3(mix, wout_ref[...])


def _tail_call(xt, gmix, win, wconv, gb, abd, cac, kc, alr, ali, tsr, tsi,
               wglu, bglu, ga, wout):
    def whole(a):
        nd = a.ndim
        return pl.BlockSpec(a.shape, lambda i: (0,) * nd, pipeline_mode=pl.Buffered(1))

    args = (xt, gmix, win, wconv, gb, abd, cac, kc, alr, ali, tsr, tsi, wglu, bglu, ga, wout)
    return pl.pallas_call(
        _tail_kernel,
        grid=(1,),
        in_specs=[whole(a) for a in args],
        out_specs=pl.BlockSpec((TAIL, D_MODEL), lambda i: (0, 0)),
        out_shape=jax.ShapeDtypeStruct((TAIL, D_MODEL), F32),
        scratch_shapes=[pltpu.VMEM((CW, CW), BF16), pltpu.VMEM((CW, CW), BF16),
                        pltpu.VMEM((N_SG, TAIL, LANES), F32),
                        pltpu.VMEM((TAIL_CR, ST_W), F32), pltpu.VMEM((TAIL_CR, ST_W), F32),
                        pltpu.VMEM((N_SG, TAIL, LANES), F32)],
        compiler_params=pltpu.CompilerParams(
            dimension_semantics=("arbitrary",), vmem_limit_bytes=VMEM_LIMIT),
        name="tail",
    )(*args)


def _post_kernel(xp_ref, xs_ref, yvp_ref, yvs_ref, mixb_ref, xmt_ref, wglu_ref, bglu_ref, ga_ref,
                 wout_ref, gffn_ref, wrt_ref, brt_ref,
                 xmid_ref, ridx_ref, rgate_ref, cnt_ref, base_ref, y_scr, logit_ref, *,
                 tail_block):
    i = pl.program_id(0)
    is_sample = i == NB - 1

    @pl.when(i == 0)
    def _():
        base_ref[...] = jnp.zeros_like(base_ref)

    yv = jnp.where(is_sample, yvs_ref[...], yvp_ref[...])
    for t in range(CHUNK):
        for k in range(N_SG):
            y_scr[k, pl.ds(t, CRB, stride=CHUNK), :] = (
                yv[:, t * SSM_W + k * LANES:t * SSM_W + (k + 1) * LANES])
    y = jnp.concatenate([y_scr[k] for k in range(N_SG)], axis=1)
    z = _gelu(y)
    gl = jnp.dot(z.astype(BF16), wglu_ref[...], preferred_element_type=F32) + bglu_ref[...]
    out_a = z * _sigmoid(gl)
    mix = jnp.concatenate([_rms(out_a, ga_ref[...]).astype(BF16), mixb_ref[...]], axis=1)
    x = jnp.where(is_sample, xs_ref[...], xp_ref[...])
    xm = x + jnp.dot(mix, wout_ref[...], preferred_element_type=F32)
    if tail_block is not None:
        xm = jnp.where(i == tail_block, xmt_ref[...], xm)
    _put_token_tiles(xmid_ref, xm)
    hn = _rms(xm, gffn_ref[...])

    def dot_nt(a, b):
        return lax.dot_general(a, b, (((1,), (1,)), ((), ())), preferred_element_type=F32)

    h_hi = hn.astype(BF16)
    w_hi, w_lo = _split(wrt_ref[...])
    logit_ref[...] = dot_nt(w_hi, h_hi) + brt_ref[...]
    if tail_block is not None:
        @pl.when(i == tail_block)
        def _():
            h_lo = (hn - h_hi.astype(F32)).astype(BF16)
            logit_ref[...] += dot_nt(w_hi, h_lo) + dot_nt(w_lo, h_hi)
    logit = logit_ref[...]
    fine = logit[0:N_EXP, :]
    coarse = logit[N_EXP:N_EXP + N_EGRP, :]
    cmax = jnp.max(coarse, axis=0, keepdims=True)
    gi = lax.broadcasted_iota(I32, (N_EGRP, TB), 0).astype(F32)
    grp = jnp.min(jnp.where(coarse == cmax, gi, float(N_EGRP)), axis=0, keepdims=True)
    pg = 1.0 / jnp.sum(jnp.exp(coarse - cmax), axis=0, keepdims=True)
    eidx = lax.broadcasted_iota(I32, (N_EXP, TB), 0)
    ei = eidx.astype(F32)
    egrp = (eidx // EPG).astype(F32)
    neg = jnp.float32(-jnp.inf)
    fm = jnp.where(egrp == grp, fine, neg)
    v0 = jnp.max(fm, axis=0, keepdims=True)
    e0 = jnp.min(jnp.where(fm == v0, ei, float(N_EXP)), axis=0, keepdims=True)
    fm2 = jnp.where(ei == e0, neg, fm)
    v1 = jnp.max(fm2, axis=0, keepdims=True)
    e1 = jnp.min(jnp.where(fm2 == v1, ei, float(N_EXP)), axis=0, keepdims=True)
    tt = jnp.exp(v1 - v0)
    g0 = pg / (1.0 + tt)
    g1 = pg * tt / (1.0 + tt)

    sel0 = ei == e0
    sel1 = ei == e1
    cnt = jnp.where(sel0 | sel1, 1.0, 0.0)
    ta = lax.broadcasted_iota(I32, (TB, TB), 0)
    tb = lax.broadcasted_iota(I32, (TB, TB), 1)
    before = jnp.where(ta < tb, 1.0, 0.0).astype(BF16)
    cum = jnp.dot(cnt.astype(BF16), before, preferred_element_type=F32) + base_ref[...]
    rank0 = jnp.sum(jnp.where(sel0, cum, 0.0), axis=0, keepdims=True)
    rank1 = jnp.sum(jnp.where(sel1, cum, 0.0), axis=0, keepdims=True)
    base_new = base_ref[...] + jnp.sum(cnt, axis=1, keepdims=True)
    base_ref[...] = base_new
    cnt_ref[...] = jnp.broadcast_to(base_new, (N_EXP, LANES)).astype(I32)

    zi = jnp.zeros((1, TB), I32)
    ridx_ref[...] = jnp.concatenate(
        [e0.astype(I32), e1.astype(I32), rank0.astype(I32), rank1.astype(I32), zi, zi, zi, zi],
        axis=0)
    zf = jnp.zeros((1, TB), F32)
    rgate_ref[...] = jnp.concatenate([g0, g1, zf, zf, zf, zf, zf, zf], axis=0)


def _post_call(xp, xs, s_block, yvp, yvs, mixb, xmt, tail_block, wglu, bglu, ga, wout, gffn,
               wrt, brt):
    const = lambda i: (0, 0)
    return pl.pallas_call(
        functools.partial(_post_kernel, tail_block=tail_block),
        grid=(NB,),
        in_specs=_stream_specs(s_block) + [
            pl.BlockSpec((CRB, UVW), lambda i: (jnp.minimum(i, NBP - 1), 0)),
            pl.BlockSpec((CRB, UVW), const),
            pl.BlockSpec((TB, CONV_W), lambda i: (i, 0)),
            pl.BlockSpec((TB, D_MODEL), const),
            pl.BlockSpec((SSM_W, SSM_W), const),
            pl.BlockSpec((1, SSM_W), const),
            pl.BlockSpec((1, SSM_W), const),
            pl.BlockSpec((D_MODEL, D_MODEL), const),
            pl.BlockSpec((1, D_MODEL), const),
            pl.BlockSpec((RT_ROWS, D_MODEL), const),
            pl.BlockSpec((RT_ROWS, 1), const),
        ],
        out_specs=[
            pl.BlockSpec((TB * SUBLANES, LANES), lambda i: (i, 0)),
            pl.BlockSpec((SUBLANES, TB), lambda i: (0, i)),
            pl.BlockSpec((SUBLANES, TB), lambda i: (0, i)),
            pl.BlockSpec((N_EXP, LANES), const),
        ],
        out_shape=[
            jax.ShapeDtypeStruct((T_ALL * SUBLANES, LANES), F32),
            jax.ShapeDtypeStruct((SUBLANES, T_ALL), I32),
            jax.ShapeDtypeStruct((SUBLANES, T_ALL), F32),
            jax.ShapeDtypeStruct((N_EXP, LANES), I32),
        ],
        scratch_shapes=[pltpu.VMEM((N_EXP, 1), F32), pltpu.VMEM((N_SG, TB, LANES), F32),
                        pltpu.VMEM((RT_ROWS, TB), F32)],
        compiler_params=pltpu.CompilerParams(
            dimension_semantics=("arbitrary",), vmem_limit_bytes=VMEM_LIMIT),
        name="post",
    )(xp, xs, yvp, yvs, mixb, xmt, wglu, bglu, ga, wout, gffn, wrt, brt)


def _invert_kernel(cpos0_ref, cpos1_ref, csrc_ref):
    i = pl.program_id(0)

    def put(r, _):
        tok = i * TB + r
        csrc_ref[cpos0_ref[r]] = tok
        csrc_ref[cpos1_ref[r]] = tok
        return 0

    lax.fori_loop(0, TB, put, 0, unroll=ISSUE_UNROLL)


def _invert_call(cpos0, cpos1):
    blk = pl.BlockSpec((TB,), lambda i: (i,), memory_space=pltpu.SMEM)
    return pl.pallas_call(
        _invert_kernel,
        grid=(NB,),
        in_specs=[blk, blk],
        out_specs=pl.BlockSpec(memory_space=pltpu.SMEM),
        out_shape=jax.ShapeDtypeStruct((N_PAIRS,), I32),
        compiler_params=pltpu.CompilerParams(dimension_semantics=("arbitrary",)),
        name="invert",
    )(cpos0, cpos1)


def _expert_kernel(te_ref, nu_ref, ts_ref, csrc_ref, xmid_ref, gffn_ref, wg_ref, wu_ref, wd_ref,
                   r_ref, wgb, wub, wdb, hbuf, xbuf, sems):
    i = pl.program_id(0)
    n_used = nu_ref[0]
    prev = te_ref[jnp.maximum(i - 1, 0)]
    fresh = (i == 0) | (te_ref[i] != prev)

    @pl.when(fresh)
    def _():
        wgb[...] = wg_ref[...].astype(BF16)
        wub[...] = wu_ref[...].astype(BF16)
        wdb[...] = wd_ref[...].astype(BF16)

    def gather_rows(tile, slot):
        base = ts_ref[tile]
        for r in range(TM):
            tok = csrc_ref[jnp.minimum(base + r, N_PAIRS - 1)]
            src = xmid_ref.at[pl.ds(pl.multiple_of(tok * SUBLANES, SUBLANES), SUBLANES)]
            dst = xbuf.at[slot, pl.ds(r * SUBLANES, SUBLANES)]
            pltpu.make_async_copy(src, dst, sems.at[slot]).start(priority=r % N_DMA_THREADS)

    def wait_rows(slot):
        pltpu.make_async_copy(xbuf.at[slot], xbuf.at[slot], sems.at[slot]).wait()

    @pl.when(i == 0)
    def _():
        gather_rows(0, 0)

    used = i < n_used
    slot = i % 2

    @pl.when(used)
    def _():
        wait_rows(slot)
        hbuf[...] = _rms(_get_token_tiles(xbuf.at[slot], TM), gffn_ref[...]).astype(BF16)
        gather_rows(jnp.minimum(i + 1, n_used - 1), 1 - slot)
        hb = hbuf[...]
        g = jnp.dot(hb, wgb[...], preferred_element_type=F32)
        u = jnp.dot(hb, wub[...], preferred_element_type=F32)
        hid = (g * _sigmoid(g) * u).astype(BF16)
        _put_token_tiles(r_ref, jnp.dot(hid, wdb[...], preferred_element_type=F32))

    @pl.when(i == n_used - 1)
    def _():
        wait_rows(1 - slot)

    @pl.when(jnp.logical_not(used))
    def _():
        r_ref[...] = jnp.zeros_like(r_ref)


def _expert_call(layer, te, nu, ts, csrc, xmid, gffn, w_gate, w_up, w_down):
    def wmap(i, te, nu, ts, csrc):
        return (layer, te[i], 0, 0)

    return pl.pallas_call(
        _expert_kernel,
        grid_spec=pltpu.PrefetchScalarGridSpec(
            num_scalar_prefetch=4,
            grid=(NT,),
            in_specs=[
                pl.BlockSpec(memory_space=pl.ANY),
                pl.BlockSpec((1, D_MODEL), lambda i, *_: (0, 0)),
                pl.BlockSpec((None, None, D_MODEL, D_FF), wmap),
                pl.BlockSpec((None, None, D_MODEL, D_FF), wmap),
                pl.BlockSpec((None, None, D_FF, D_MODEL), wmap),
            ],
            out_specs=pl.BlockSpec((TM * SUBLANES, LANES), lambda i, *_: (i, 0)),
            scratch_shapes=[pltpu.VMEM((D_MODEL, D_FF), BF16),
                            pltpu.VMEM((D_MODEL, D_FF), BF16),
                            pltpu.VMEM((D_FF, D_MODEL), BF16),
                            pltpu.VMEM((TM, D_MODEL), BF16),
                            pltpu.VMEM((2, TM * SUBLANES, LANES), F32),
                            pltpu.SemaphoreType.DMA((2,))],
        ),
        out_shape=jax.ShapeDtypeStruct((P_CAP * SUBLANES, LANES), F32),
        compiler_params=pltpu.CompilerParams(
            dimension_semantics=("arbitrary",), vmem_limit_bytes=VMEM_LIMIT),
        name="experts",
    )(te, nu, ts, csrc, xmid, gffn, w_gate, w_up, w_down)


def _combine_rows(pos0_ref, pos1_ref, npos0_ref, npos1_ref, xmid_ref, gate_ref, r_ref, buf, sems):
    i = pl.program_id(0)
    slot = i % 2

    @pl.when(i == 0)
    def _():
        _issue_row_gather(pos0_ref, pos1_ref, r_ref, lambda k: buf.at[k], sems.at[0])

    for k in range(2):
        pltpu.make_async_copy(buf.at[k], buf.at[k], sems.at[slot]).wait()

    @pl.when(i < NB - 1)
    def _():
        other = 1 - slot
        _issue_row_gather(npos0_ref, npos1_ref, r_ref, lambda k: buf.at[2 * other + k],
                          sems.at[other])

    gt = gate_ref[...]
    return (_get_token_tiles(xmid_ref, TB)
            + gt[:, 0:1] * _get_token_tiles(buf.at[2 * slot], TB)
            + gt[:, 1:2] * _get_token_tiles(buf.at[2 * slot + 1], TB))


def _combine_specs():
    cur = pl.BlockSpec((TB,), lambda i: (i,), memory_space=pltpu.SMEM)
    nxt = pl.BlockSpec((TB,), lambda i: (jnp.minimum(i + 1, NB - 1),), memory_space=pltpu.SMEM)
    in_specs = [cur, cur, nxt, nxt,
                pl.BlockSpec((TB * SUBLANES, LANES), lambda i: (i, 0)),
                pl.BlockSpec((TB, 2), lambda i: (i, 0)),
                pl.BlockSpec(memory_space=pl.ANY)]
    scratch = [pltpu.VMEM((4, TB * SUBLANES, LANES), F32), pltpu.SemaphoreType.DMA((2,))]
    return in_specs, scratch


def _combine_final_kernel(pos0_ref, pos1_ref, npos0_ref, npos1_ref, xmid_ref, gate_ref, r_ref,
                          gfin_ref, outp_ref, outs_ref, buf, sems):
    i = pl.program_id(0)
    out = _rms(_combine_rows(pos0_ref, pos1_ref, npos0_ref, npos1_ref, xmid_ref, gate_ref, r_ref,
                             buf, sems), gfin_ref[...])

    @pl.when(i < NB - 1)
    def _():
        outp_ref[...] = out

    @pl.when(i == NB - 1)
    def _():
        outs_ref[...] = out


def _combine_final_call(pos0, pos1, xmid, gate_t, r, gfin):
    in_specs, scratch = _combine_specs()
    return pl.pallas_call(
        _combine_final_kernel,
        grid=(NB,),
        in_specs=in_specs + [pl.BlockSpec((1, D_MODEL), lambda i: (0, 0))],
        out_specs=[pl.BlockSpec((TB, D_MODEL), lambda i: (jnp.minimum(i, NBP - 1), 0)),
                   pl.BlockSpec((TB, D_MODEL), lambda i: (0, 0))],
        out_shape=[jax.ShapeDtypeStruct((T_PROMPT, D_MODEL), F32),
                   jax.ShapeDtypeStruct((T_SAMPLE, D_MODEL), F32)],
        scratch_shapes=scratch,
        compiler_params=pltpu.CompilerParams(
            dimension_semantics=("arbitrary",), vmem_limit_bytes=VMEM_LIMIT),
        name="combine_final",
    )(pos0, pos1, pos0, pos1, xmid, gate_t, r, gfin)


def _ssm_compact(lam_re, lam_im, log_dt, b_re, b_im, c_re, c_im, d_skip):
    nl = lam_re.shape[0]
    dt = jnp.exp(log_dt)[..., None]
    kpow = jnp.arange(CHUNK + 1, dtype=F32).reshape(CHUNK + 1, 1, 1, 1)
    mag = jnp.exp(kpow * (lam_re * dt))
    pw_re = mag * jnp.cos(kpow * (lam_im * dt))
    pw_im = mag * jnp.sin(kpow * (lam_im * dt))
    ab_re, ab_im = pw_re[1], pw_im[1]
    denom = lam_re * lam_re + lam_im * lam_im
    num_re = ab_re - 1.0
    f_re = (num_re * lam_re + ab_im * lam_im) / denom
    f_im = (ab_im * lam_re - num_re * lam_im) / denom
    bb_re = f_re[..., None] * b_re - f_im[..., None] * b_im
    bb_im = f_re[..., None] * b_im + f_im[..., None] * b_re

    krev = (CHUNK - 1.0) - kpow[:CHUNK]
    rmag = jnp.exp(krev * (lam_re * dt))
    rev_re = (rmag * jnp.cos(krev * (lam_im * dt)))[..., None]
    rev_im = (rmag * jnp.sin(krev * (lam_im * dt)))[..., None]
    ab = jnp.stack([rev_re * bb_re - rev_im * bb_im, rev_re * bb_im + rev_im * bb_re])
    ab = ab.reshape(2, CHUNK, nl, N_SG, GPS, N_ST, G_CH)
    ab = ab.transpose(2, 3, 1, 0, 4, 6, 5).reshape(nl, N_SG, CHUNK, 2, LANES, N_ST)
    abd = jnp.concatenate([ab, ab], axis=-1)

    pr = pw_re[:, :, :, None, :]
    pi = pw_im[:, :, :, None, :]
    ca_re = c_re * pr - c_im * pi
    ca_im = c_re * pi + c_im * pr
    ca = jnp.stack([ca_re[1:], -ca_im[1:]])
    ca = ca.reshape(2, CHUNK, nl, N_SG, GPS, G_CH, N_ST)
    cac = ca.transpose(2, 3, 1, 0, 6, 4, 5).reshape(nl, N_SG, CHUNK, 2, N_ST, LANES)

    bq_re = jnp.swapaxes(bb_re, -1, -2)[:, :, None]
    bq_im = jnp.swapaxes(bb_im, -1, -2)[:, :, None]
    kk = jnp.sum(ca_re[:CHUNK, :, :, :, None, :] * bq_re
                 - ca_im[:CHUNK, :, :, :, None, :] * bq_im, axis=-1)
    skip = d_skip.reshape(nl, N_G, G_CH)[..., None] * jnp.eye(G_CH, dtype=F32)
    kk = jnp.concatenate([kk[:1] + skip[None], kk[1:]], axis=0)
    kk = kk.reshape(CHUNK, nl, N_SG, GPS, G_CH, G_CH)
    kk = kk.transpose(1, 2, 0, 5, 3, 4)
    kc = kk.reshape(nl, N_SG, CHUNK, G_CH, LANES)

    al_re = pw_re[CHUNK].reshape(nl, 1, ST_W)
    al_im = pw_im[CHUNK].reshape(nl, 1, ST_W)
    return abd, cac, kc, al_re, al_im


def _tile_tables(counts):
    padded = ((counts + TM - 1) // TM) * TM
    ends = jnp.cumsum(padded)
    offs = ends - padded
    end_tile = ends // TM
    n_used = end_tile[-1].astype(I32)
    tile = jnp.arange(NT, dtype=I32)
    live = jnp.minimum(tile, n_used - 1)
    te = jnp.sum(end_tile[None, :] <= live[:, None], axis=1).astype(I32)
    cstart = jnp.cumsum(counts) - counts
    mine = te[:, None] == jnp.arange(N_EXP, dtype=I32)[None, :]
    first_row = jnp.sum(jnp.where(mine, (cstart - offs)[None, :], 0), axis=1)
    ts = (first_row + live * TM).astype(I32)
    return offs.astype(I32), cstart.astype(I32), te, n_used.reshape(1), ts


def _sorted_positions(ridx, offs, cstart):
    experts = jnp.arange(N_EXP, dtype=I32)[:, None]
    out = []
    for k in range(2):
        hit = ridx[k][None, :] == experts
        for table in (offs, cstart):
            out.append(jnp.sum(jnp.where(hit, table[:, None], 0), axis=0) + ridx[2 + k])
    pos0, cpos0, pos1, cpos1 = out
    return pos0, pos1, cpos0, cpos1


def kernel(x_prompt, x_sample, state_ssm_re, state_ssm_im, cache_conv, norm_mix, w_in, lam_re,
           lam_im, log_dt, ssm_b_re, ssm_b_im, ssm_c_re, ssm_c_im, ssm_d, w_glu, b_glu, w_conv,
           norm_a, norm_b, w_out, norm_ffn, w_router_group, b_router_group, w_router_expert,
           b_router_expert, w_gate, w_up, w_down, norm_final):
    xp = x_prompt.reshape(T_PROMPT, D_MODEL)
    xs = x_sample.reshape(T_SAMPLE, D_MODEL)
    s_block = 0
    gfin = norm_final.reshape(1, D_MODEL)
    abd, cac, kc, al_re, al_im = _ssm_compact(
        lam_re, lam_im, log_dt, ssm_b_re, ssm_b_im, ssm_c_re, ssm_c_im, ssm_d)
    p_re, p_im, p_buf, s_re, s_im, s_buf = [], [], [], [], [], []
    moe = None
    for l in range(DEPTH):
        cache = cache_conv[l]
        cz1 = jnp.pad(cache[:, 1:2], ((0, 0), (0, LEN_S - 1), (0, 0))).reshape(T_SAMPLE, CONV_W)
        cz2 = jnp.pad(cache, ((0, 0), (0, LEN_S - 2), (0, 0))).reshape(T_SAMPLE, CONV_W)
        inproj_params = (norm_mix[l].reshape(1, D_MODEL), w_in[l].astype(BF16), w_conv[l],
                         norm_b[l].reshape(1, CONV_W), cz1, cz2)
        if moe is None:
            uvp, uvs, mixb, ptail, zs = _inproj_call(xp, xs, s_block, *inproj_params)
        else:
            x, uvp, uvs, mixb, ptail, zs = _combine_inproj_call(*moe, *inproj_params)
            xp = xs = x
            s_block = NB - 1

        yvp, yvs, pst_r, pst_i, sst_r, sst_i, tst_r, tst_i = _ssm_call(
            uvp, uvs, abd[l].astype(BF16), cac[l].astype(BF16), kc[l].astype(BF16),
            al_re[l], al_im[l],
            state_ssm_re[l].reshape(N_SEQ_S, ST_W), state_ssm_im[l].reshape(N_SEQ_S, ST_W))

        tail_block = NBP - 1 if l < DEPTH - 1 else None
        if tail_block is not None:
            xmt = _tail_call(
                xp[T_PROMPT - TAIL_X:T_PROMPT], norm_mix[l].reshape(1, D_MODEL), w_in[l],
                w_conv[l], norm_b[l].reshape(1, CONV_W), abd[l], cac[l], kc[l],
                al_re[l], al_im[l], tst_r, tst_i, w_glu[l], b_glu[l].reshape(1, SSM_W),
                norm_a[l].reshape(1, SSM_W), w_out[l])

        wrt = jnp.concatenate([
            w_router_expert[l].transpose(0, 2, 1).reshape(N_EXP, D_MODEL),
            w_router_group[l].T,
            jnp.zeros((RT_ROWS - N_EXP - N_EGRP, D_MODEL), F32)], axis=0)
        brt = jnp.concatenate([b_router_expert[l].reshape(N_EXP), b_router_group[l],
                               jnp.zeros((RT_ROWS - N_EXP - N_EGRP,), F32)]).reshape(RT_ROWS, 1)
        gffn = norm_ffn[l].reshape(1, D_MODEL)
        xmid, ridx, rgate, cnt = _post_call(
            xp, xs, s_block, yvp, yvs, mixb, xmt, tail_block,
            w_glu[l].astype(BF16), b_glu[l].reshape(1, SSM_W),
            norm_a[l].reshape(1, SSM_W), w_out[l].astype(BF16), gffn, wrt, brt)

        offs, cstart, te, n_used, ts = _tile_tables(cnt[:, 0])
        pos0, pos1, cpos0, cpos1 = _sorted_positions(ridx, offs, cstart)
        csrc = _invert_call(cpos0, cpos1)
        r = _expert_call(l, te, n_used, ts, csrc, xmid, gffn, w_gate, w_up, w_down)
        moe = (pos0, pos1, xmid, rgate[:2].T, r)

        p_re.append(pst_r.reshape(1, N_G, N_ST))
        p_im.append(pst_i.reshape(1, N_G, N_ST))
        p_buf.append(ptail[SUBLANES - 2:].reshape(1, 2, CONV_W))
        s_re.append(sst_r.reshape(N_SEQ_S, N_G, N_ST))
        s_im.append(sst_i.reshape(N_SEQ_S, N_G, N_ST))
        s_buf.append(zs.reshape(N_SEQ_S, LEN_S, CONV_W)[:, LEN_S - 2:])

    y_prompt, y_sample = _combine_final_call(*moe, gfin)
    return (y_prompt.reshape(1, T_PROMPT, D_MODEL), y_sample.reshape(N_SEQ_S, LEN_S, D_MODEL),
            jnp.stack(p_re), jnp.stack(p_im), jnp.stack(p_buf),
            jnp.stack(s_re), jnp.stack(s_im), jnp.stack(s_buf))
```

```python
import functools

import jax
import jax.numpy as jnp
from jax import lax
from jax.experimental import pallas as pl
from jax.experimental.pallas import tpu as pltpu

F32 = jnp.float32
BF16 = jnp.bfloat16
I32 = jnp.int32

D_MODEL = 1024
DEPTH = 2
T_PROMPT = 16384
N_SEQ_S = 32
LEN_S = 16
T_SAMPLE = N_SEQ_S * LEN_S
T_ALL = T_PROMPT + T_SAMPLE
SSM_W = 512
CONV_W = 512
N_G = 32
G_CH = 16
N_ST = 64
ST_W = N_G * N_ST
N_EGRP = 4
EPG = 8
N_EXP = 32
D_FF = 512
EPS = 1e-6

LANES = 128
SUBLANES = 8
TB = 512
NB = T_ALL // TB
NBP = T_PROMPT // TB
CHUNK = 8
CRB = TB // CHUNK
GPS = LANES // G_CH
N_SG = SSM_W // LANES
SG_ST = GPS * N_ST
CW = CHUNK * LANES
UVW = CHUNK * SSM_W
CR_P = T_PROMPT // CHUNK
CR_S = T_SAMPLE // CHUNK
CPS = LEN_S // CHUNK
CB = 256
NPB = CR_P // CB
TM = 256
P_CAP = 2 * T_ALL + N_EXP * TM
NT = P_CAP // TM
N_PAIRS = 2 * T_ALL
TAIL = TB
TAIL_X = TAIL + SUBLANES
TAIL_CR = TAIL // CHUNK
N_DMA_THREADS = 2
GATHER_AHEAD = 2
ISSUE_UNROLL = 8
RT_ROWS = 40

VMEM_LIMIT = 56 * 1024 * 1024

assert T_SAMPLE == TB and TB % CHUNK == 0 and CR_S <= CB and LEN_S % CHUNK == 0


def _rms(x, g):
    return x * lax.rsqrt(jnp.mean(x * x, axis=-1, keepdims=True) + EPS) * g


def _sigmoid(x):
    return 1.0 / (1.0 + jnp.exp(-x))


def _gelu(y):
    return 0.5 * y * (1.0 + jnp.tanh(0.7978845608028654 * (y + 0.044715 * (y * y * y))))


def _put_token_tiles(ref, x):
    rows = x.shape[0]
    for j in range(D_MODEL // LANES):
        ref[pl.ds(j, rows, stride=SUBLANES), :] = x[:, j * LANES:(j + 1) * LANES]


def _get_token_tiles(ref, rows):
    return jnp.concatenate(
        [ref[pl.ds(j, rows, stride=SUBLANES), :] for j in range(D_MODEL // LANES)], axis=1)


def _stream_specs(s_block):
    return [pl.BlockSpec((TB, D_MODEL), lambda i: (jnp.minimum(i, NBP - 1), 0)),
            pl.BlockSpec((TB, D_MODEL), lambda i: (s_block, 0))]


def _inproj_kernel(xp_ref, xs_ref, *rest):
    is_sample = pl.program_id(0) == NB - 1
    _inproj_body(jnp.where(is_sample, xs_ref[...], xp_ref[...]), *rest)


def _inproj_body(x, gmix_ref, win_ref, wconv_ref, gb_ref, cz1_ref, cz2_ref,
                 uvp_ref, uvs_ref, mixb_ref, ptail_ref, zs_ref, carry_ref, u_scr):
    i = pl.program_id(0)
    is_sample = i == NB - 1

    @pl.when(i == 0)
    def _():
        carry_ref[...] = jnp.zeros_like(carry_ref)

    h = _rms(x, gmix_ref[...]).astype(BF16)
    proj = jnp.dot(h, win_ref[...], preferred_element_type=F32)
    for k in range(N_SG):
        u_scr[k] = proj[:, k * LANES:(k + 1) * LANES]
    gate_b = proj[:, SSM_W:SSM_W + CONV_W]
    gate_c = proj[:, SSM_W + CONV_W:SSM_W + 2 * CONV_W]
    v = proj[:, SSM_W + 2 * CONV_W:]
    z = gate_c * v
    row = lax.broadcasted_iota(I32, (TB, 1), 0)
    r1 = pltpu.roll(z, 1, 0)
    r2 = pltpu.roll(z, 2, 0)
    c6 = carry_ref[6:7, :]
    c7 = carry_ref[7:8, :]
    z1p = jnp.where(row == 0, c7, r1)
    z2p = jnp.where(row == 0, c6, jnp.where(row == 1, c7, r2))
    pos = row % LEN_S
    z1s = jnp.where(pos == 0, cz1_ref[...], r1)
    z2s = jnp.where(pos < 2, cz2_ref[...], r2)
    z1 = jnp.where(is_sample, z1s, z1p)
    z2 = jnp.where(is_sample, z2s, z2p)
    w = wconv_ref[...]
    conv = w[0:1, :] * z2 + w[1:2, :] * z1 + w[2:3, :] * z
    yb = gate_b * conv
    mixb_ref[...] = _rms(yb, gb_ref[...]).astype(BF16)
    carry_ref[...] = z[TB - SUBLANES:, :]

    def put_chunk_rows(uv_ref):
        for t in range(CHUNK):
            for k in range(N_SG):
                uv_ref[:, t * SSM_W + k * LANES:t * SSM_W + (k + 1) * LANES] = (
                    u_scr[k, pl.ds(t, CRB, stride=CHUNK), :].astype(BF16))

    @pl.when(jnp.logical_not(is_sample))
    def _():
        put_chunk_rows(uvp_ref)

    @pl.when(i == NB - 2)
    def _():
        ptail_ref[...] = z[TB - SUBLANES:, :]

    @pl.when(is_sample)
    def _():
        put_chunk_rows(uvs_ref)
        zs_ref[...] = z


def _inproj_specs():
    const = lambda i: (0, 0)
    in_specs = [
        pl.BlockSpec((1, D_MODEL), const),
        pl.BlockSpec((D_MODEL, 4 * SSM_W), const),
        pl.BlockSpec((3, CONV_W), const),
        pl.BlockSpec((1, CONV_W), const),
        pl.BlockSpec((TB, CONV_W), const),
        pl.BlockSpec((TB, CONV_W), const),
    ]
    out_specs = [
        pl.BlockSpec((CRB, UVW), lambda i: (jnp.minimum(i, NBP - 1), 0)),
        pl.BlockSpec((CRB, UVW), const),
        pl.BlockSpec((TB, CONV_W), lambda i: (i, 0)),
        pl.BlockSpec((SUBLANES, CONV_W), const),
        pl.BlockSpec((TB, CONV_W), const),
    ]
    out_shape = [
        jax.ShapeDtypeStruct((CR_P, UVW), BF16),
        jax.ShapeDtypeStruct((CR_S, UVW), BF16),
        jax.ShapeDtypeStruct((T_ALL, CONV_W), BF16),
        jax.ShapeDtypeStruct((SUBLANES, CONV_W), F32),
        jax.ShapeDtypeStruct((TB, CONV_W), F32),
    ]
    scratch = [pltpu.VMEM((SUBLANES, CONV_W), F32), pltpu.VMEM((N_SG, TB, LANES), F32)]
    return in_specs, out_specs, out_shape, scratch


def _inproj_call(xp, xs, s_block, gmix, win, wconv, gb, cz1, cz2):
    in_specs, out_specs, out_shape, scratch = _inproj_specs()
    return pl.pallas_call(
        _inproj_kernel,
        grid=(NB,),
        in_specs=_stream_specs(s_block) + in_specs,
        out_specs=out_specs,
        out_shape=out_shape,
        scratch_shapes=scratch,
        compiler_params=pltpu.CompilerParams(
            dimension_semantics=("arbitrary",), vmem_limit_bytes=VMEM_LIMIT),
        name="inproj",
    )(xp, xs, gmix, win, wconv, gb, cz1, cz2)


def _issue_row_gather(pos0_ref, pos1_ref, r_ref, dst_of, sem):
    def issue(r, _):
        for k, pos_ref in enumerate((pos0_ref, pos1_ref)):
            p = pos_ref[r]
            src = r_ref.at[pl.ds(pl.multiple_of(p * SUBLANES, SUBLANES), SUBLANES)]
            dst = dst_of(k).at[pl.ds(pl.multiple_of(r * SUBLANES, SUBLANES), SUBLANES)]
            pltpu.make_async_copy(src, dst, sem).start(priority=k % N_DMA_THREADS)
        return 0

    lax.fori_loop(0, TB, issue, 0, unroll=ISSUE_UNROLL)


def _combine_inproj_kernel(pos0_ref, pos1_ref, npos0_ref, npos1_ref, xmid_ref, gate_ref, r_ref,
                           gmix_ref, win_ref, wconv_ref, gb_ref, cz1_ref, cz2_ref,
                           x_ref, uvp_ref, uvs_ref, mixb_ref, ptail_ref, zs_ref,
                           buf, sems, carry_ref, u_scr):
    x = _combine_rows(pos0_ref, pos1_ref, npos0_ref, npos1_ref, xmid_ref, gate_ref, r_ref,
                      buf, sems)
    x_ref[...] = x
    _inproj_body(x, gmix_ref, win_ref, wconv_ref, gb_ref, cz1_ref, cz2_ref,
                 uvp_ref, uvs_ref, mixb_ref, ptail_ref, zs_ref, carry_ref, u_scr)


def _combine_inproj_call(pos0, pos1, xmid, gate_t, r, gmix, win, wconv, gb, cz1, cz2):
    in_specs, out_specs, out_shape, scratch = _inproj_specs()
    c_specs, c_scratch = _combine_specs()
    return pl.pallas_call(
        _combine_inproj_kernel,
        grid=(NB,),
        in_specs=c_specs + in_specs,
        out_specs=[pl.BlockSpec((TB, D_MODEL), lambda i: (i, 0))] + out_specs,
        out_shape=[jax.ShapeDtypeStruct((T_ALL, D_MODEL), F32)] + out_shape,
        scratch_shapes=c_scratch + scratch,
        compiler_params=pltpu.CompilerParams(
            dimension_semantics=("arbitrary",), vmem_limit_bytes=VMEM_LIMIT),
        name="combine_inproj",
    )(pos0, pos1, pos0, pos1, xmid, gate_t, r, gmix, win, wconv, gb, cz1, cz2)


def _expand_chunk_matrices(abd_ref, cac_ref, kc_ref, ws, wy, kb):
    for s in range(N_SG):
        _expand_ws(abd_ref, s, ws.at[s])
        _expand_wy(cac_ref, s, wy.at[s])
        _expand_kb(kc_ref, s, kb.at[s])


def _bf16_part(a, part):
    if part is None:
        return a
    hi = a.astype(BF16)
    return hi if part == "hi" else (a - hi.astype(F32)).astype(BF16)


def _expand_ws(abd_ref, s, dst, part=None):
    row_g = lax.broadcasted_iota(I32, (LANES, LANES), 0) // G_CH
    lane_half = lax.broadcasted_iota(I32, (LANES, LANES), 1) // N_ST
    for t in range(CHUNK):
        for r in range(2):
            a = _bf16_part(abd_ref[s, t, r], part)
            for m in range(GPS // 2):
                dst[t * LANES:(t + 1) * LANES,
                    r * SG_ST + m * LANES:r * SG_ST + (m + 1) * LANES] = jnp.where(
                        row_g == 2 * m + lane_half, a, jnp.zeros_like(a))


def _expand_wy(cac_ref, s, dst, part=None):
    col_g = lax.broadcasted_iota(I32, (N_ST, LANES), 1) // G_CH
    for t in range(CHUNK):
        for r in range(2):
            c = _bf16_part(cac_ref[s, t, r], part)
            for g in range(GPS):
                dst[r * SG_ST + g * N_ST:r * SG_ST + (g + 1) * N_ST,
                    t * LANES:(t + 1) * LANES] = jnp.where(col_g == g, c, jnp.zeros_like(c))


def _expand_kb(kc_ref, s, dst, part=None):
    row_g = lax.broadcasted_iota(I32, (LANES, LANES), 0) // G_CH
    col_g = lax.broadcasted_iota(I32, (LANES, LANES), 1) // G_CH
    zero_blk = jnp.zeros((LANES, LANES), BF16)
    for k in range(CHUNK):
        kq = _bf16_part(kc_ref[s, k], part)
        blk = jnp.where(row_g == col_g, jnp.concatenate([kq] * GPS, axis=0), zero_blk)
        for tp in range(CHUNK - k):
            t = tp + k
            dst[tp * LANES:(tp + 1) * LANES, t * LANES:(t + 1) * LANES] = blk
    for tp in range(CHUNK):
        for t in range(tp):
            dst[tp * LANES:(tp + 1) * LANES, t * LANES:(t + 1) * LANES] = zero_blk


def _ucat(uv_ref, sg):
    parts = [uv_ref[:, t * SSM_W + sg * LANES:t * SSM_W + (sg + 1) * LANES] for t in range(CHUNK)]
    return jnp.concatenate(parts, axis=1)


def _ssm_local(uv_ref, ws, sre, sim):
    for sg in range(N_SG):
        s = jnp.dot(_ucat(uv_ref, sg), ws[sg], preferred_element_type=F32)
        sre[:, sg * SG_ST:(sg + 1) * SG_ST] = s[:, :SG_ST]
        sim[:, sg * SG_ST:(sg + 1) * SG_ST] = s[:, SG_ST:]


def _ssm_out(uv_ref, wy, kb, sre, sim, yv_ref):
    for sg in range(N_SG):
        sp = jnp.concatenate([sre[:, sg * SG_ST:(sg + 1) * SG_ST],
                              sim[:, sg * SG_ST:(sg + 1) * SG_ST]], axis=1).astype(BF16)
        y = (jnp.dot(sp, wy[sg], preferred_element_type=F32)
             + jnp.dot(_ucat(uv_ref, sg), kb[sg], preferred_element_type=F32))
        for t in range(CHUNK):
            yv_ref[:, t * SSM_W + sg * LANES:t * SSM_W + (sg + 1) * LANES] = (
                y[:, t * LANES:(t + 1) * LANES])


def _ssm_kernel(uvp_ref, uvs_ref, abd_ref, cac_ref, kc_ref, alr_ref, ali_ref, h0r_ref, h0i_ref,
                yvp_ref, yvs_ref, pstr_ref, psti_ref, sstr_ref, ssti_ref, tstr_ref, tsti_ref,
                ws, wy, kb, sre, sim, cre, cim):
    i = pl.program_id(0)

    @pl.when(i == 0)
    def _():
        cre[...] = jnp.zeros_like(cre)
        cim[...] = jnp.zeros_like(cim)
        _expand_chunk_matrices(abd_ref, cac_ref, kc_ref, ws, wy, kb)

    ar = alr_ref[...]
    ai = ali_ref[...]

    @pl.when(i < NPB)
    def _():
        _ssm_local(uvp_ref, ws, sre, sim)

        def step(c, carry):
            sr, si = carry
            lr = sre[pl.ds(c, 1), :]
            li = sim[pl.ds(c, 1), :]
            sre[pl.ds(c, 1), :] = sr
            sim[pl.ds(c, 1), :] = si
            return ar * sr - ai * si + lr, ar * si + ai * sr + li

        sr, si = lax.fori_loop(0, CB, step, (cre[...], cim[...]))
        cre[...] = sr
        cim[...] = si
        pstr_ref[...] = sr
        psti_ref[...] = si
        tstr_ref[...] = sre[CB - TAIL_CR:CB - TAIL_CR + 1, :]
        tsti_ref[...] = sim[CB - TAIL_CR:CB - TAIL_CR + 1, :]
        _ssm_out(uvp_ref, wy, kb, sre, sim, yvp_ref)

    @pl.when(i == NPB)
    def _():
        sre_s = sre.at[0:CR_S]
        sim_s = sim.at[0:CR_S]
        _ssm_local(uvs_ref, ws, sre_s, sim_s)

        def per_stream(q, _):
            sr = h0r_ref[pl.ds(q, 1), :]
            si = h0i_ref[pl.ds(q, 1), :]
            for k in range(CPS):
                c = q * CPS + k
                lr = sre_s[pl.ds(c, 1), :]
                li = sim_s[pl.ds(c, 1), :]
                sre_s[pl.ds(c, 1), :] = sr
                sim_s[pl.ds(c, 1), :] = si
                sr, si = ar * sr - ai * si + lr, ar * si + ai * sr + li
            sstr_ref[pl.ds(q, 1), :] = sr
            ssti_ref[pl.ds(q, 1), :] = si
            return 0

        lax.fori_loop(0, N_SEQ_S, per_stream, 0)
        _ssm_out(uvs_ref, wy, kb, sre_s, sim_s, yvs_ref)


def _ssm_call(uvp, uvs, abd, cac, kc, alr, ali, h0r, h0i):
    c2 = lambda i: (0, 0)
    c4 = lambda i: (0, 0, 0, 0)
    c5 = lambda i: (0, 0, 0, 0, 0)
    one = pl.Buffered(1)
    pblk = lambda i: (jnp.minimum(i, NPB - 1), 0)
    return pl.pallas_call(
        _ssm_kernel,
        grid=(NPB + 1,),
        in_specs=[pl.BlockSpec((CB, UVW), pblk),
                  pl.BlockSpec((CR_S, UVW), c2),
                  pl.BlockSpec((N_SG, CHUNK, 2, LANES, LANES), c5, pipeline_mode=one),
                  pl.BlockSpec((N_SG, CHUNK, 2, N_ST, LANES), c5, pipeline_mode=one),
                  pl.BlockSpec((N_SG, CHUNK, G_CH, LANES), c4, pipeline_mode=one),
                  pl.BlockSpec((1, ST_W), c2),
                  pl.BlockSpec((1, ST_W), c2),
                  pl.BlockSpec((N_SEQ_S, ST_W), c2),
                  pl.BlockSpec((N_SEQ_S, ST_W), c2)],
        out_specs=[pl.BlockSpec((CB, UVW), pblk),
                   pl.BlockSpec((CR_S, UVW), c2),
                   pl.BlockSpec((1, ST_W), c2),
                   pl.BlockSpec((1, ST_W), c2),
                   pl.BlockSpec((N_SEQ_S, ST_W), c2),
                   pl.BlockSpec((N_SEQ_S, ST_W), c2),
                   pl.BlockSpec((1, ST_W), c2),
                   pl.BlockSpec((1, ST_W), c2)],
        out_shape=[jax.ShapeDtypeStruct((CR_P, UVW), F32),
                   jax.ShapeDtypeStruct((CR_S, UVW), F32),
                   jax.ShapeDtypeStruct((1, ST_W), F32),
                   jax.ShapeDtypeStruct((1, ST_W), F32),
                   jax.ShapeDtypeStruct((N_SEQ_S, ST_W), F32),
                   jax.ShapeDtypeStruct((N_SEQ_S, ST_W), F32),
                   jax.ShapeDtypeStruct((1, ST_W), F32),
                   jax.ShapeDtypeStruct((1, ST_W), F32)],
        scratch_shapes=[pltpu.VMEM((N_SG, CW, 2 * SG_ST), BF16),
                        pltpu.VMEM((N_SG, 2 * SG_ST, CW), BF16),
                        pltpu.VMEM((N_SG, CW, CW), BF16),
                        pltpu.VMEM((CB, ST_W), F32), pltpu.VMEM((CB, ST_W), F32),
                        pltpu.VMEM((1, ST_W), F32), pltpu.VMEM((1, ST_W), F32)],
        compiler_params=pltpu.CompilerParams(
            dimension_semantics=("arbitrary",), vmem_limit_bytes=VMEM_LIMIT),
        name="ssm",
    )(uvp, uvs, abd, cac, kc, alr, ali, h0r, h0i)


def _split(a):
    hi = a.astype(BF16)
    return hi, (a - hi.astype(F32)).astype(BF16)


def _dot3(a, b):
    ah, al = _split(a)
    bh, bl = _split(b)
    return (jnp.dot(ah, bh, preferred_element_type=F32)
            + jnp.dot(ah, bl, preferred_element_type=F32)
            + jnp.dot(al, bh, preferred_element_type=F32))


def _tail_kernel(xt_ref, gmix_ref, win_ref, wconv_ref, gb_ref, abd_ref, cac_ref, kc_ref,
                 alr_ref, ali_ref, tsr_ref, tsi_ref, wglu_ref, bglu_ref, ga_ref, wout_ref,
                 xmt_ref, mh, ml, u_scr, sre, sim, y_scr):
    x = xt_ref[...]
    proj = _dot3(_rms(x, gmix_ref[...]), win_ref[...])
    z = proj[:, SSM_W + CONV_W:SSM_W + 2 * CONV_W] * proj[:, SSM_W + 2 * CONV_W:]
    w = wconv_ref[...]
    conv = w[0:1, :] * pltpu.roll(z, 2, 0) + w[1:2, :] * pltpu.roll(z, 1, 0) + w[2:3, :] * z
    nb = _rms(proj[SUBLANES:, SSM_W:SSM_W + CONV_W] * conv[SUBLANES:], gb_ref[...])
    for k in range(N_SG):
        u_scr[k] = proj[SUBLANES:, k * LANES:(k + 1) * LANES]

    def ucat(sg):
        return jnp.concatenate(
            [u_scr[sg, pl.ds(t, TAIL_CR, stride=CHUNK), :] for t in range(CHUNK)], axis=1)

    def dot3m(a):
        ah, al = _split(a)
        return (jnp.dot(ah, mh[...], preferred_element_type=F32)
                + jnp.dot(ah, ml[...], preferred_element_type=F32)
                + jnp.dot(al, mh[...], preferred_element_type=F32))

    for sg in range(N_SG):
        _expand_ws(abd_ref, sg, mh, "hi")
        _expand_ws(abd_ref, sg, ml, "lo")
        s = dot3m(ucat(sg))
        sre[:, sg * SG_ST:(sg + 1) * SG_ST] = s[:, :SG_ST]
        sim[:, sg * SG_ST:(sg + 1) * SG_ST] = s[:, SG_ST:]

    ar = alr_ref[...]
    ai = ali_ref[...]

    def step(c, carry):
        sr, si = carry
        lr = sre[pl.ds(c, 1), :]
        li = sim[pl.ds(c, 1), :]
        sre[pl.ds(c, 1), :] = sr
        sim[pl.ds(c, 1), :] = si
        return ar * sr - ai * si + lr, ar * si + ai * sr + li

    lax.fori_loop(0, TAIL_CR, step, (tsr_ref[...], tsi_ref[...]))

    for sg in range(N_SG):
        sp = jnp.concatenate([sre[:, sg * SG_ST:(sg + 1) * SG_ST],
                              sim[:, sg * SG_ST:(sg + 1) * SG_ST]], axis=1)
        _expand_wy(cac_ref, sg, mh, "hi")
        _expand_wy(cac_ref, sg, ml, "lo")
        y = dot3m(sp)
        _expand_kb(kc_ref, sg, mh, "hi")
        _expand_kb(kc_ref, sg, ml, "lo")
        y = y + dot3m(ucat(sg))
        for t in range(CHUNK):
            y_scr[sg, pl.ds(t, TAIL_CR, stride=CHUNK), :] = y[:, t * LANES:(t + 1) * LANES]

    zg = _gelu(jnp.concatenate([y_scr[k] for k in range(N_SG)], axis=1))
    out_a = zg * _sigmoid(_dot3(zg, wglu_ref[...]) + bglu_ref[...])
    mix = jnp.concatenate([_rms(out_a, ga_ref[...]), nb], axis=1)
    xmt_ref[...] = x[SUBLANES:, :] + _dot3(mix, wout_ref[...])


def _tail_call(xt, gmix, win, wconv, gb, abd, cac, kc, alr, ali, tsr, tsi,
               wglu, bglu, ga, wout):
    def whole(a):
        nd = a.ndim
        return pl.BlockSpec(a.shape, lambda i: (0,) * nd, pipeline_mode=pl.Buffered(1))

    args = (xt, gmix, win, wconv, gb, abd, cac, kc, alr, ali, tsr, tsi, wglu, bglu, ga, wout)
    return pl.pallas_call(
        _tail_kernel,
        grid=(1,),
        in_specs=[whole(a) for a in args],
        out_specs=pl.BlockSpec((TAIL, D_MODEL), lambda i: (0, 0)),
        out_shape=jax.ShapeDtypeStruct((TAIL, D_MODEL), F32),
        scratch_shapes=[pltpu.VMEM((CW, CW), BF16), pltpu.VMEM((CW, CW), BF16),
                        pltpu.VMEM((N_SG, TAIL, LANES), F32),
                        pltpu.VMEM((TAIL_CR, ST_W), F32), pltpu.VMEM((TAIL_CR, ST_W), F32),
                        pltpu.VMEM((N_SG, TAIL, LANES), F32)],
        compiler_params=pltpu.CompilerParams(
            dimension_semantics=("arbitrary",), vmem_limit_bytes=VMEM_LIMIT),
        name="tail",
    )(*args)


def _post_kernel(xp_ref, xs_ref, yvp_ref, yvs_ref, mixb_ref, xmt_ref, wglu_ref, bglu_ref, ga_ref,
                 wout_ref, gffn_ref, wrt_ref, brt_ref,
                 xmid_ref, ridx_ref, rgate_ref, cnt_ref, base_ref, y_scr, logit_ref, before_ref, *,
                 tail_block):
    i = pl.program_id(0)
    is_sample = i == NB - 1

    @pl.when(i == 0)
    def _():
        base_ref[...] = jnp.zeros_like(base_ref)
        ta = lax.broadcasted_iota(I32, (TB, TB), 0)
        tb = lax.broadcasted_iota(I32, (TB, TB), 1)
        before_ref[...] = jnp.where(ta < tb, 1.0, 0.0).astype(BF16)

    yv = jnp.where(is_sample, yvs_ref[...], yvp_ref[...])
    for t in range(CHUNK):
        for k in range(N_SG):
            y_scr[k, pl.ds(t, CRB, stride=CHUNK), :] = (
                yv[:, t * SSM_W + k * LANES:t * SSM_W + (k + 1) * LANES])
    y = jnp.concatenate([y_scr[k] for k in range(N_SG)], axis=1)
    z = _gelu(y)
    gl = jnp.dot(z.astype(BF16), wglu_ref[...], preferred_element_type=F32) + bglu_ref[...]
    out_a = z * _sigmoid(gl)
    mix = jnp.concatenate([_rms(out_a, ga_ref[...]).astype(BF16), mixb_ref[...]], axis=1)
    x = jnp.where(is_sample, xs_ref[...], xp_ref[...])
    xm = x + jnp.dot(mix, wout_ref[...], preferred_element_type=F32)
    if tail_block is not None:
        xm = jnp.where(i == tail_block, xmt_ref[...], xm)
    _put_token_tiles(xmid_ref, xm)
    hn = _rms(xm, gffn_ref[...])

    def dot_nt(a, b):
        return lax.dot_general(a, b, (((1,), (1,)), ((), ())), preferred_element_type=F32)

    h_hi = hn.astype(BF16)
    w_hi, w_lo = _split(wrt_ref[...])
    logit_ref[...] = dot_nt(w_hi, h_hi) + brt_ref[...]
    if tail_block is not None:
        @pl.when(i == tail_block)
        def _():
            h_lo = (hn - h_hi.astype(F32)).astype(BF16)
            logit_ref[...] += dot_nt(w_hi, h_lo) + dot_nt(w_lo, h_hi)
    logit = logit_ref[...]
    fine = logit[0:N_EXP, :]
    coarse = logit[N_EXP:N_EXP + N_EGRP, :]
    cmax = jnp.max(coarse, axis=0, keepdims=True)
    gi = lax.broadcasted_iota(I32, (N_EGRP, TB), 0).astype(F32)
    grp = jnp.min(jnp.where(coarse == cmax, gi, float(N_EGRP)), axis=0, keepdims=True)
    pg = 1.0 / jnp.sum(jnp.exp(coarse - cmax), axis=0, keepdims=True)
    eidx = lax.broadcasted_iota(I32, (N_EXP, TB), 0)
    ei = eidx.astype(F32)
    egrp = (eidx // EPG).astype(F32)
    neg = jnp.float32(-jnp.inf)
    fm = jnp.where(egrp == grp, fine, neg)
    v0 = jnp.max(fm, axis=0, keepdims=True)
    e0 = jnp.min(jnp.where(fm == v0, ei, float(N_EXP)), axis=0, keepdims=True)
    fm2 = jnp.where(ei == e0, neg, fm)
    v1 = jnp.max(fm2, axis=0, keepdims=True)
    e1 = jnp.min(jnp.where(fm2 == v1, ei, float(N_EXP)), axis=0, keepdims=True)
    tt = jnp.exp(v1 - v0)
    g0 = pg / (1.0 + tt)
    g1 = pg * tt / (1.0 + tt)

    sel0 = ei == e0
    sel1 = ei == e1
    cnt = jnp.where(sel0 | sel1, 1.0, 0.0)
    cum = (jnp.dot(cnt.astype(BF16), before_ref[...], preferred_element_type=F32)
           + base_ref[...])
    rank0 = jnp.sum(jnp.where(sel0, cum, 0.0), axis=0, keepdims=True)
    rank1 = jnp.sum(jnp.where(sel1, cum, 0.0), axis=0, keepdims=True)
    base_new = base_ref[...] + jnp.sum(cnt, axis=1, keepdims=True)
    base_ref[...] = base_new
    cnt_ref[...] = jnp.broadcast_to(base_new, (N_EXP, LANES)).astype(I32)

    zi = jnp.zeros((1, TB), I32)
    ridx_ref[...] = jnp.concatenate(
        [e0.astype(I32), e1.astype(I32), rank0.astype(I32), rank1.astype(I32), zi, zi, zi, zi],
        axis=0)
    zf = jnp.zeros((1, TB), F32)
    rgate_ref[...] = jnp.concatenate([g0, g1, zf, zf, zf, zf, zf, zf], axis=0)


def _post_call(xp, xs, s_block, yvp, yvs, mixb, xmt, tail_block, wglu, bglu, ga, wout, gffn,
               wrt, brt):
    const = lambda i: (0, 0)
    return pl.pallas_call(
        functools.partial(_post_kernel, tail_block=tail_block),
        grid=(NB,),
        in_specs=_stream_specs(s_block) + [
            pl.BlockSpec((CRB, UVW), lambda i: (jnp.minimum(i, NBP - 1), 0)),
            pl.BlockSpec((CRB, UVW), const),
            pl.BlockSpec((TB, CONV_W), lambda i: (i, 0)),
            pl.BlockSpec((TB, D_MODEL), const),
            pl.BlockSpec((SSM_W, SSM_W), const),
            pl.BlockSpec((1, SSM_W), const),
            pl.BlockSpec((1, SSM_W), const),
            pl.BlockSpec((D_MODEL, D_MODEL), const),
            pl.BlockSpec((1, D_MODEL), const),
            pl.BlockSpec((RT_ROWS, D_MODEL), const),
            pl.BlockSpec((RT_ROWS, 1), const),
        ],
        out_specs=[
            pl.BlockSpec((TB * SUBLANES, LANES), lambda i: (i, 0)),
            pl.BlockSpec((SUBLANES, TB), lambda i: (0, i)),
            pl.BlockSpec((SUBLANES, TB), lambda i: (0, i)),
            pl.BlockSpec((N_EXP, LANES), const),
        ],
        out_shape=[
            jax.ShapeDtypeStruct((T_ALL * SUBLANES, LANES), F32),
            jax.ShapeDtypeStruct((SUBLANES, T_ALL), I32),
            jax.ShapeDtypeStruct((SUBLANES, T_ALL), F32),
            jax.ShapeDtypeStruct((N_EXP, LANES), I32),
        ],
        scratch_shapes=[pltpu.VMEM((N_EXP, 1), F32), pltpu.VMEM((N_SG, TB, LANES), F32),
                        pltpu.VMEM((RT_ROWS, TB), F32), pltpu.VMEM((TB, TB), BF16)],
        compiler_params=pltpu.CompilerParams(
            dimension_semantics=("arbitrary",), vmem_limit_bytes=VMEM_LIMIT),
        name="post",
    )(xp, xs, yvp, yvs, mixb, xmt, wglu, bglu, ga, wout, gffn, wrt, brt)


def _invert_kernel(cpos0_ref, cpos1_ref, csrc_ref):
    i = pl.program_id(0)

    def put(r, _):
        tok = i * TB + r
        csrc_ref[cpos0_ref[r]] = tok
        csrc_ref[cpos1_ref[r]] = tok
        return 0

    lax.fori_loop(0, TB, put, 0, unroll=ISSUE_UNROLL)


def _invert_call(cpos0, cpos1):
    blk = pl.BlockSpec((TB,), lambda i: (i,), memory_space=pltpu.SMEM)
    return pl.pallas_call(
        _invert_kernel,
        grid=(NB,),
        in_specs=[blk, blk],
        out_specs=pl.BlockSpec(memory_space=pltpu.SMEM),
        out_shape=jax.ShapeDtypeStruct((N_PAIRS,), I32),
        compiler_params=pltpu.CompilerParams(dimension_semantics=("arbitrary",)),
        name="invert",
    )(cpos0, cpos1)


def _expert_kernel(te_ref, nu_ref, ts_ref, csrc_ref, xmid_ref, gffn_ref, wg_ref, wu_ref, wd_ref,
                   r_ref, wgb, wub, wdb, hbuf, xbuf, sems):
    i = pl.program_id(0)
    n_used = nu_ref[0]
    prev = te_ref[jnp.maximum(i - 1, 0)]
    fresh = (i == 0) | (te_ref[i] != prev)

    @pl.when(fresh)
    def _():
        wgb[...] = wg_ref[...].astype(BF16)
        wub[...] = wu_ref[...].astype(BF16)
        wdb[...] = wd_ref[...].astype(BF16)

    def gather_rows(tile, slot):
        base = ts_ref[tile]
        for r in range(TM):
            tok = csrc_ref[jnp.minimum(base + r, N_PAIRS - 1)]
            src = xmid_ref.at[pl.ds(pl.multiple_of(tok * SUBLANES, SUBLANES), SUBLANES)]
            dst = xbuf.at[slot, pl.ds(r * SUBLANES, SUBLANES)]
            pltpu.make_async_copy(src, dst, sems.at[slot]).start(priority=r % N_DMA_THREADS)

    def wait_rows(slot):
        pltpu.make_async_copy(xbuf.at[slot], xbuf.at[slot], sems.at[slot]).wait()

    @pl.when(i == 0)
    def _():
        for a in range(GATHER_AHEAD):
            gather_rows(jnp.minimum(a, n_used - 1), a)

    used = i < n_used
    slot = i % (GATHER_AHEAD + 1)

    @pl.when(used)
    def _():
        wait_rows(slot)
        hbuf[...] = _rms(_get_token_tiles(xbuf.at[slot], TM), gffn_ref[...]).astype(BF16)
        gather_rows(jnp.minimum(i + GATHER_AHEAD, n_used - 1),
                    (i + GATHER_AHEAD) % (GATHER_AHEAD + 1))
        hb = hbuf[...]
        g = jnp.dot(hb, wgb[...], preferred_element_type=F32)
        u = jnp.dot(hb, wub[...], preferred_element_type=F32)
        hid = (g * _sigmoid(g) * u).astype(BF16)
        _put_token_tiles(r_ref, jnp.dot(hid, wdb[...], preferred_element_type=F32))

    @pl.when(i == n_used - 1)
    def _():
        for a in range(1, GATHER_AHEAD + 1):
            wait_rows((i + a) % (GATHER_AHEAD + 1))

    @pl.when(jnp.logical_not(used))
    def _():
        r_ref[...] = jnp.zeros_like(r_ref)


def _expert_call(layer, te, nu, ts, csrc, xmid, gffn, w_gate, w_up, w_down):
    def wmap(i, te, nu, ts, csrc):
        return (layer, te[i], 0, 0)

    return pl.pallas_call(
        _expert_kernel,
        grid_spec=pltpu.PrefetchScalarGridSpec(
            num_scalar_prefetch=4,
            grid=(NT,),
            in_specs=[
                pl.BlockSpec(memory_space=pl.ANY),
                pl.BlockSpec((1, D_MODEL), lambda i, *_: (0, 0)),
                pl.BlockSpec((None, None, D_MODEL, D_FF), wmap),
                pl.BlockSpec((None, None, D_MODEL, D_FF), wmap),
                pl.BlockSpec((None, None, D_FF, D_MODEL), wmap),
            ],
            out_specs=pl.BlockSpec((TM * SUBLANES, LANES), lambda i, *_: (i, 0)),
            scratch_shapes=[pltpu.VMEM((D_MODEL, D_FF), BF16),
                            pltpu.VMEM((D_MODEL, D_FF), BF16),
                            pltpu.VMEM((D_FF, D_MODEL), BF16),
                            pltpu.VMEM((TM, D_MODEL), BF16),
                            pltpu.VMEM((GATHER_AHEAD + 1, TM * SUBLANES, LANES), F32),
                            pltpu.SemaphoreType.DMA((GATHER_AHEAD + 1,))],
        ),
        out_shape=jax.ShapeDtypeStruct((P_CAP * SUBLANES, LANES), F32),
        compiler_params=pltpu.CompilerParams(
            dimension_semantics=("arbitrary",), vmem_limit_bytes=VMEM_LIMIT),
        name="experts",
    )(te, nu, ts, csrc, xmid, gffn, w_gate, w_up, w_down)


def _combine_rows(pos0_ref, pos1_ref, npos0_ref, npos1_ref, xmid_ref, gate_ref, r_ref, buf, sems):
    i = pl.program_id(0)
    slot = i % 2

    @pl.when(i == 0)
    def _():
        _issue_row_gather(pos0_ref, pos1_ref, r_ref, lambda k: buf.at[k], sems.at[0])

    for k in range(2):
        pltpu.make_async_copy(buf.at[k], buf.at[k], sems.at[slot]).wait()

    @pl.when(i < NB - 1)
    def _():
        other = 1 - slot
        _issue_row_gather(npos0_ref, npos1_ref, r_ref, lambda k: buf.at[2 * other + k],
                          sems.at[other])

    gt = gate_ref[...]
    return (_get_token_tiles(xmid_ref, TB)
            + gt[:, 0:1] * _get_token_tiles(buf.at[2 * slot], TB)
            + gt[:, 1:2] * _get_token_tiles(buf.at[2 * slot + 1], TB))


def _combine_specs():
    cur = pl.BlockSpec((TB,), lambda i: (i,), memory_space=pltpu.SMEM)
    nxt = pl.BlockSpec((TB,), lambda i: (jnp.minimum(i + 1, NB - 1),), memory_space=pltpu.SMEM)
    in_specs = [cur, cur, nxt, nxt,
                pl.BlockSpec((TB * SUBLANES, LANES), lambda i: (i, 0)),
                pl.BlockSpec((TB, 2), lambda i: (i, 0)),
                pl.BlockSpec(memory_space=pl.ANY)]
    scratch = [pltpu.VMEM((4, TB * SUBLANES, LANES), F32), pltpu.SemaphoreType.DMA((2,))]
    return in_specs, scratch


def _combine_final_kernel(pos0_ref, pos1_ref, npos0_ref, npos1_ref, xmid_ref, gate_ref, r_ref,
                          gfin_ref, outp_ref, outs_ref, buf, sems):
    i = pl.program_id(0)
    out = _rms(_combine_rows(pos0_ref, pos1_ref, npos0_ref, npos1_ref, xmid_ref, gate_ref, r_ref,
                             buf, sems), gfin_ref[...])

    @pl.when(i < NB - 1)
    def _():
        outp_ref[...] = out

    @pl.when(i == NB - 1)
    def _():
        outs_ref[...] = out


def _combine_final_call(pos0, pos1, xmid, gate_t, r, gfin):
    in_specs, scratch = _combine_specs()
    return pl.pallas_call(
        _combine_final_kernel,
        grid=(NB,),
        in_specs=in_specs + [pl.BlockSpec((1, D_MODEL), lambda i: (0, 0))],
        out_specs=[pl.BlockSpec((TB, D_MODEL), lambda i: (jnp.minimum(i, NBP - 1), 0)),
                   pl.BlockSpec((TB, D_MODEL), lambda i: (0, 0))],
        out_shape=[jax.ShapeDtypeStruct((T_PROMPT, D_MODEL), F32),
                   jax.ShapeDtypeStruct((T_SAMPLE, D_MODEL), F32)],
        scratch_shapes=scratch,
        compiler_params=pltpu.CompilerParams(
            dimension_semantics=("arbitrary",), vmem_limit_bytes=VMEM_LIMIT),
        name="combine_final",
    )(pos0, pos1, pos0, pos1, xmid, gate_t, r, gfin)


def _ssm_compact(lam_re, lam_im, log_dt, b_re, b_im, c_re, c_im, d_skip):
    nl = lam_re.shape[0]
    dt = jnp.exp(log_dt)[..., None]
    kpow = jnp.arange(CHUNK + 1, dtype=F32).reshape(CHUNK + 1, 1, 1, 1)
    mag = jnp.exp(kpow * (lam_re * dt))
    pw_re = mag * jnp.cos(kpow * (lam_im * dt))
    pw_im = mag * jnp.sin(kpow * (lam_im * dt))
    ab_re, ab_im = pw_re[1], pw_im[1]
    denom = lam_re * lam_re + lam_im * lam_im
    num_re = ab_re - 1.0
    f_re = (num_re * lam_re + ab_im * lam_im) / denom
    f_im = (ab_im * lam_re - num_re * lam_im) / denom
    bb_re = f_re[..., None] * b_re - f_im[..., None] * b_im
    bb_im = f_re[..., None] * b_im + f_im[..., None] * b_re

    krev = (CHUNK - 1.0) - kpow[:CHUNK]
    rmag = jnp.exp(krev * (lam_re * dt))
    rev_re = (rmag * jnp.cos(krev * (lam_im * dt)))[..., None]
    rev_im = (rmag * jnp.sin(krev * (lam_im * dt)))[..., None]
    ab = jnp.stack([rev_re * bb_re - rev_im * bb_im, rev_re * bb_im + rev_im * bb_re])
    ab = ab.reshape(2, CHUNK, nl, N_SG, GPS, N_ST, G_CH)
    ab = ab.transpose(2, 3, 1, 0, 4, 6, 5).reshape(nl, N_SG, CHUNK, 2, LANES, N_ST)
    abd = jnp.concatenate([ab, ab], axis=-1)

    pr = pw_re[:, :, :, None, :]
    pi = pw_im[:, :, :, None, :]
    ca_re = c_re * pr - c_im * pi
    ca_im = c_re * pi + c_im * pr
    ca = jnp.stack([ca_re[1:], -ca_im[1:]])
    ca = ca.reshape(2, CHUNK, nl, N_SG, GPS, G_CH, N_ST)
    cac = ca.transpose(2, 3, 1, 0, 6, 4, 5).reshape(nl, N_SG, CHUNK, 2, N_ST, LANES)

    bq_re = jnp.swapaxes(bb_re, -1, -2)[:, :, None]
    bq_im = jnp.swapaxes(bb_im, -1, -2)[:, :, None]
    kk = jnp.sum(ca_re[:CHUNK, :, :, :, None, :] * bq_re
                 - ca_im[:CHUNK, :, :, :, None, :] * bq_im, axis=-1)
    skip = d_skip.reshape(nl, N_G, G_CH)[..., None] * jnp.eye(G_CH, dtype=F32)
    kk = jnp.concatenate([kk[:1] + skip[None], kk[1:]], axis=0)
    kk = kk.reshape(CHUNK, nl, N_SG, GPS, G_CH, G_CH)
    kk = kk.transpose(1, 2, 0, 5, 3, 4)
    kc = kk.reshape(nl, N_SG, CHUNK, G_CH, LANES)

    al_re = pw_re[CHUNK].reshape(nl, 1, ST_W)
    al_im = pw_im[CHUNK].reshape(nl, 1, ST_W)
    return abd, cac, kc, al_re, al_im


def _tile_tables(counts):
    padded = ((counts + TM - 1) // TM) * TM
    ends = jnp.cumsum(padded)
    offs = ends - padded
    end_tile = ends // TM
    n_used = end_tile[-1].astype(I32)
    tile = jnp.arange(NT, dtype=I32)
    live = jnp.minimum(tile, n_used - 1)
    te = jnp.sum(end_tile[None, :] <= live[:, None], axis=1).astype(I32)
    cstart = jnp.cumsum(counts) - counts
    mine = te[:, None] == jnp.arange(N_EXP, dtype=I32)[None, :]
    first_row = jnp.sum(jnp.where(mine, (cstart - offs)[None, :], 0), axis=1)
    ts = (first_row + live * TM).astype(I32)
    return offs.astype(I32), cstart.astype(I32), te, n_used.reshape(1), ts


def _sorted_positions(ridx, offs, cstart):
    experts = jnp.arange(N_EXP, dtype=I32)[:, None]
    out = []
    for k in range(2):
        hit = ridx[k][None, :] == experts
        for table in (offs, cstart):
            out.append(jnp.sum(jnp.where(hit, table[:, None], 0), axis=0) + ridx[2 + k])
    pos0, cpos0, pos1, cpos1 = out
    return pos0, pos1, cpos0, cpos1


def kernel(x_prompt, x_sample, state_ssm_re, state_ssm_im, cache_conv, norm_mix, w_in, lam_re,
           lam_im, log_dt, ssm_b_re, ssm_b_im, ssm_c_re, ssm_c_im, ssm_d, w_glu, b_glu, w_conv,
           norm_a, norm_b, w_out, norm_ffn, w_router_group, b_router_group, w_router_expert,
           b_router_expert, w_gate, w_up, w_down, norm_final):
    xp = x_prompt.reshape(T_PROMPT, D_MODEL)
    xs = x_sample.reshape(T_SAMPLE, D_MODEL)
    s_block = 0
    gfin = norm_final.reshape(1, D_MODEL)
    abd, cac, kc, al_re, al_im = _ssm_compact(
        lam_re, lam_im, log_dt, ssm_b_re, ssm_b_im, ssm_c_re, ssm_c_im, ssm_d)
    p_re, p_im, p_buf, s_re, s_im, s_buf = [], [], [], [], [], []
    moe = None
    for l in range(DEPTH):
        cache = cache_conv[l]
        cz1 = jnp.pad(cache[:, 1:2], ((0, 0), (0, LEN_S - 1), (0, 0))).reshape(T_SAMPLE, CONV_W)
        cz2 = jnp.pad(cache, ((0, 0), (0, LEN_S - 2), (0, 0))).reshape(T_SAMPLE, CONV_W)
        inproj_params = (norm_mix[l].reshape(1, D_MODEL), w_in[l].astype(BF16), w_conv[l],
                         norm_b[l].reshape(1, CONV_W), cz1, cz2)
        if moe is None:
            uvp, uvs, mixb, ptail, zs = _inproj_call(xp, xs, s_block, *inproj_params)
        else:
            x, uvp, uvs, mixb, ptail, zs = _combine_inproj_call(*moe, *inproj_params)
            xp = xs = x
            s_block = NB - 1

        yvp, yvs, pst_r, pst_i, sst_r, sst_i, tst_r, tst_i = _ssm_call(
            uvp, uvs, abd[l].astype(BF16), cac[l].astype(BF16), kc[l].astype(BF16),
            al_re[l], al_im[l],
            state_ssm_re[l].reshape(N_SEQ_S, ST_W), state_ssm_im[l].reshape(N_SEQ_S, ST_W))

        tail_block = NBP - 1 if l < DEPTH - 1 else None
        if tail_block is not None:
            xmt = _tail_call(
                xp[T_PROMPT - TAIL_X:T_PROMPT], norm_mix[l].reshape(1, D_MODEL), w_in[l],
                w_conv[l], norm_b[l].reshape(1, CONV_W), abd[l], cac[l], kc[l],
                al_re[l], al_im[l], tst_r, tst_i, w_glu[l], b_glu[l].reshape(1, SSM_W),
                norm_a[l].reshape(1, SSM_W), w_out[l])

        wrt = jnp.concatenate([
            w_router_expert[l].transpose(0, 2, 1).reshape(N_EXP, D_MODEL),
            w_router_group[l].T,
            jnp.zeros((RT_ROWS - N_EXP - N_EGRP, D_MODEL), F32)], axis=0)
        brt = jnp.concatenate([b_router_expert[l].reshape(N_EXP), b_router_group[l],
                               jnp.zeros((RT_ROWS - N_EXP - N_EGRP,), F32)]).reshape(RT_ROWS, 1)
        gffn = norm_ffn[l].reshape(1, D_MODEL)
        xmid, ridx, rgate, cnt = _post_call(
            xp, xs, s_block, yvp, yvs, mixb, xmt, tail_block,
            w_glu[l].astype(BF16), b_glu[l].reshape(1, SSM_W),
            norm_a[l].reshape(1, SSM_W), w_out[l].astype(BF16), gffn, wrt, brt)

        offs, cstart, te, n_used, ts = _tile_tables(cnt[:, 0])
        pos0, pos1, cpos0, cpos1 = _sorted_positions(ridx, offs, cstart)
        csrc = _invert_call(cpos0, cpos1)
        r = _expert_call(l, te, n_used, ts, csrc, xmid, gffn, w_gate, w_up, w_down)
        moe = (pos0, pos1, xmid, rgate[:2].T, r)

        p_re.append(pst_r.reshape(1, N_G, N_ST))
        p_im.append(pst_i.reshape(1, N_G, N_ST))
        p_buf.append(ptail[SUBLANES - 2:].reshape(1, 2, CONV_W))
        s_re.append(sst_r.reshape(N_SEQ_S, N_G, N_ST))
        s_im.append(sst_i.reshape(N_SEQ_S, N_G, N_ST))
        s_buf.append(zs.reshape(N_SEQ_S, LEN_S, CONV_W)[:, LEN_S - 2:])

    y_prompt, y_sample = _combine_final_call(*moe, gfin)
    return (y_prompt.reshape(1, T_PROMPT, D_MODEL), y_sample.reshape(N_SEQ_S, LEN_S, D_MODEL),
            jnp.stack(p_re), jnp.stack(p_im), jnp.stack(p_buf),
            jnp.stack(s_re), jnp.stack(s_im), jnp.stack(s_buf))
```

```python
import functools

import jax
import jax.numpy as jnp
from jax import lax
from jax.experimental import pallas as pl
from jax.experimental.pallas import tpu as pltpu

F32 = jnp.float32
BF16 = jnp.bfloat16
I32 = jnp.int32

D_MODEL = 1024
DEPTH = 2
T_PROMPT = 16384
N_SEQ_S = 32
LEN_S = 16
T_SAMPLE = N_SEQ_S * LEN_S
T_ALL = T_PROMPT + T_SAMPLE
SSM_W = 512
CONV_W = 512
N_G = 32
G_CH = 16
N_ST = 64
ST_W = N_G * N_ST
N_EGRP = 4
EPG = 8
N_EXP = 32
D_FF = 512
EPS = 1e-6

LANES = 128
SUBLANES = 8
TB = 512
NB = T_ALL // TB
NBP = T_PROMPT // TB
CHUNK = 8
CRB = TB // CHUNK
GPS = LANES // G_CH
N_SG = SSM_W // LANES
SG_ST = GPS * N_ST
CW = CHUNK * LANES
UVW = CHUNK * SSM_W
CR_P = T_PROMPT // CHUNK
CR_S = T_SAMPLE // CHUNK
CPS = LEN_S // CHUNK
CB = 256
NPB = CR_P // CB
TM = 256
P_CAP = 2 * T_ALL + N_EXP * TM
NT = P_CAP // TM
N_PAIRS = 2 * T_ALL
TAIL = TB
TAIL_X = TAIL + SUBLANES
TAIL_CR = TAIL // CHUNK
N_DMA_THREADS = 2
GATHER_AHEAD = 3
ISSUE_UNROLL = 8
RT_ROWS = 40

VMEM_LIMIT = 56 * 1024 * 1024

assert T_SAMPLE == TB and TB % CHUNK == 0 and CR_S <= CB and LEN_S % CHUNK == 0


def _rms(x, g):
    return x * lax.rsqrt(jnp.mean(x * x, axis=-1, keepdims=True) + EPS) * g


def _sigmoid(x):
    return 1.0 / (1.0 + jnp.exp(-x))


def _gelu(y):
    return 0.5 * y * (1.0 + jnp.tanh(0.7978845608028654 * (y + 0.044715 * (y * y * y))))


def _put_token_tiles(ref, x):
    rows = x.shape[0]
    for j in range(D_MODEL // LANES):
        ref[pl.ds(j, rows, stride=SUBLANES), :] = x[:, j * LANES:(j + 1) * LANES]


def _get_token_tiles(ref, rows):
    return jnp.concatenate(
        [ref[pl.ds(j, rows, stride=SUBLANES), :] for j in range(D_MODEL // LANES)], axis=1)


def _stream_specs(s_block):
    return [pl.BlockSpec((TB, D_MODEL), lambda i: (jnp.minimum(i, NBP - 1), 0)),
            pl.BlockSpec((TB, D_MODEL), lambda i: (s_block, 0))]


def _inproj_kernel(xp_ref, xs_ref, *rest):
    is_sample = pl.program_id(0) == NB - 1
    _inproj_body(jnp.where(is_sample, xs_ref[...], xp_ref[...]), *rest)


def _inproj_body(x, gmix_ref, win_ref, wconv_ref, gb_ref, cz1_ref, cz2_ref,
                 uvp_ref, uvs_ref, mixb_ref, ptail_ref, zs_ref, carry_ref, u_scr):
    i = pl.program_id(0)
    is_sample = i == NB - 1

    @pl.when(i == 0)
    def _():
        carry_ref[...] = jnp.zeros_like(carry_ref)

    h = _rms(x, gmix_ref[...]).astype(BF16)
    proj = jnp.dot(h, win_ref[...], preferred_element_type=F32)
    for k in range(N_SG):
        u_scr[k] = proj[:, k * LANES:(k + 1) * LANES]
    gate_b = proj[:, SSM_W:SSM_W + CONV_W]
    gate_c = proj[:, SSM_W + CONV_W:SSM_W + 2 * CONV_W]
    v = proj[:, SSM_W + 2 * CONV_W:]
    z = gate_c * v
    row = lax.broadcasted_iota(I32, (TB, 1), 0)
    r1 = pltpu.roll(z, 1, 0)
    r2 = pltpu.roll(z, 2, 0)
    c6 = carry_ref[6:7, :]
    c7 = carry_ref[7:8, :]
    z1p = jnp.where(row == 0, c7, r1)
    z2p = jnp.where(row == 0, c6, jnp.where(row == 1, c7, r2))
    pos = row % LEN_S
    z1s = jnp.where(pos == 0, cz1_ref[...], r1)
    z2s = jnp.where(pos < 2, cz2_ref[...], r2)
    z1 = jnp.where(is_sample, z1s, z1p)
    z2 = jnp.where(is_sample, z2s, z2p)
    w = wconv_ref[...]
    conv = w[0:1, :] * z2 + w[1:2, :] * z1 + w[2:3, :] * z
    yb = gate_b * conv
    mixb_ref[...] = _rms(yb, gb_ref[...]).astype(BF16)
    carry_ref[...] = z[TB - SUBLANES:, :]

    def put_chunk_rows(uv_ref):
        for t in range(CHUNK):
            for k in range(N_SG):
                uv_ref[:, t * SSM_W + k * LANES:t * SSM_W + (k + 1) * LANES] = (
                    u_scr[k, pl.ds(t, CRB, stride=CHUNK), :].astype(BF16))

    @pl.when(jnp.logical_not(is_sample))
    def _():
        put_chunk_rows(uvp_ref)

    @pl.when(i == NB - 2)
    def _():
        ptail_ref[...] = z[TB - SUBLANES:, :]

    @pl.when(is_sample)
    def _():
        put_chunk_rows(uvs_ref)
        zs_ref[...] = z


def _inproj_specs():
    const = lambda i: (0, 0)
    in_specs = [
        pl.BlockSpec((1, D_MODEL), const),
        pl.BlockSpec((D_MODEL, 4 * SSM_W), const),
        pl.BlockSpec((3, CONV_W), const),
        pl.BlockSpec((1, CONV_W), const),
        pl.BlockSpec((TB, CONV_W), const),
        pl.BlockSpec((TB, CONV_W), const),
    ]
    out_specs = [
        pl.BlockSpec((CRB, UVW), lambda i: (jnp.minimum(i, NBP - 1), 0)),
        pl.BlockSpec((CRB, UVW), const),
        pl.BlockSpec((TB, CONV_W), lambda i: (i, 0)),
        pl.BlockSpec((SUBLANES, CONV_W), const),
        pl.BlockSpec((TB, CONV_W), const),
    ]
    out_shape = [
        jax.ShapeDtypeStruct((CR_P, UVW), BF16),
        jax.ShapeDtypeStruct((CR_S, UVW), BF16),
        jax.ShapeDtypeStruct((T_ALL, CONV_W), BF16),
        jax.ShapeDtypeStruct((SUBLANES, CONV_W), F32),
        jax.ShapeDtypeStruct((TB, CONV_W), F32),
    ]
    scratch = [pltpu.VMEM((SUBLANES, CONV_W), F32), pltpu.VMEM((N_SG, TB, LANES), F32)]
    return in_specs, out_specs, out_shape, scratch


def _inproj_call(xp, xs, s_block, gmix, win, wconv, gb, cz1, cz2):
    in_specs, out_specs, out_shape, scratch = _inproj_specs()
    return pl.pallas_call(
        _inproj_kernel,
        grid=(NB,),
        in_specs=_stream_specs(s_block) + in_specs,
        out_specs=out_specs,
        out_shape=out_shape,
        scratch_shapes=scratch,
        compiler_params=pltpu.CompilerParams(
            dimension_semantics=("arbitrary",), vmem_limit_bytes=VMEM_LIMIT),
        name="inproj",
    )(xp, xs, gmix, win, wconv, gb, cz1, cz2)


def _issue_row_gather(pos0_ref, pos1_ref, r_ref, dst_of, sem):
    def issue(r, _):
        for k, pos_ref in enumerate((pos0_ref, pos1_ref)):
            p = pos_ref[r]
            src = r_ref.at[pl.ds(pl.multiple_of(p * SUBLANES, SUBLANES), SUBLANES)]
            dst = dst_of(k).at[pl.ds(pl.multiple_of(r * SUBLANES, SUBLANES), SUBLANES)]
            pltpu.make_async_copy(src, dst, sem).start(priority=k % N_DMA_THREADS)
        return 0

    lax.fori_loop(0, TB, issue, 0, unroll=ISSUE_UNROLL)


def _combine_inproj_kernel(pos0_ref, pos1_ref, npos0_ref, npos1_ref, xmid_ref, gate_ref, r_ref,
                           gmix_ref, win_ref, wconv_ref, gb_ref, cz1_ref, cz2_ref,
                           x_ref, uvp_ref, uvs_ref, mixb_ref, ptail_ref, zs_ref,
                           buf, sems, carry_ref, u_scr):
    x = _combine_rows(pos0_ref, pos1_ref, npos0_ref, npos1_ref, xmid_ref, gate_ref, r_ref,
                      buf, sems)
    x_ref[...] = x
    _inproj_body(x, gmix_ref, win_ref, wconv_ref, gb_ref, cz1_ref, cz2_ref,
                 uvp_ref, uvs_ref, mixb_ref, ptail_ref, zs_ref, carry_ref, u_scr)


def _combine_inproj_call(pos0, pos1, xmid, gate_t, r, gmix, win, wconv, gb, cz1, cz2):
    in_specs, out_specs, out_shape, scratch = _inproj_specs()
    c_specs, c_scratch = _combine_specs()
    return pl.pallas_call(
        _combine_inproj_kernel,
        grid=(NB,),
        in_specs=c_specs + in_specs,
        out_specs=[pl.BlockSpec((TB, D_MODEL), lambda i: (i, 0))] + out_specs,
        out_shape=[jax.ShapeDtypeStruct((T_ALL, D_MODEL), F32)] + out_shape,
        scratch_shapes=c_scratch + scratch,
        compiler_params=pltpu.CompilerParams(
            dimension_semantics=("arbitrary",), vmem_limit_bytes=VMEM_LIMIT),
        name="combine_inproj",
    )(pos0, pos1, pos0, pos1, xmid, gate_t, r, gmix, win, wconv, gb, cz1, cz2)


def _expand_chunk_matrices(abd_ref, cac_ref, kc_ref, ws, wy, kb):
    for s in range(N_SG):
        _expand_ws(abd_ref, s, ws.at[s])
        _expand_wy(cac_ref, s, wy.at[s])
        _expand_kb(kc_ref, s, kb.at[s])


def _bf16_part(a, part):
    if part is None:
        return a
    hi = a.astype(BF16)
    return hi if part == "hi" else (a - hi.astype(F32)).astype(BF16)


def _expand_ws(abd_ref, s, dst, part=None):
    row_g = lax.broadcasted_iota(I32, (LANES, LANES), 0) // G_CH
    lane_half = lax.broadcasted_iota(I32, (LANES, LANES), 1) // N_ST
    for t in range(CHUNK):
        for r in range(2):
            a = _bf16_part(abd_ref[s, t, r], part)
            for m in range(GPS // 2):
                dst[t * LANES:(t + 1) * LANES,
                    r * SG_ST + m * LANES:r * SG_ST + (m + 1) * LANES] = jnp.where(
                        row_g == 2 * m + lane_half, a, jnp.zeros_like(a))


def _expand_wy(cac_ref, s, dst, part=None):
    col_g = lax.broadcasted_iota(I32, (N_ST, LANES), 1) // G_CH
    for t in range(CHUNK):
        for r in range(2):
            c = _bf16_part(cac_ref[s, t, r], part)
            for g in range(GPS):
                dst[r * SG_ST + g * N_ST:r * SG_ST + (g + 1) * N_ST,
                    t * LANES:(t + 1) * LANES] = jnp.where(col_g == g, c, jnp.zeros_like(c))


def _expand_kb(kc_ref, s, dst, part=None):
    row_g = lax.broadcasted_iota(I32, (LANES, LANES), 0) // G_CH
    col_g = lax.broadcasted_iota(I32, (LANES, LANES), 1) // G_CH
    zero_blk = jnp.zeros((LANES, LANES), BF16)
    for k in range(CHUNK):
        kq = _bf16_part(kc_ref[s, k], part)
        blk = jnp.where(row_g == col_g, jnp.concatenate([kq] * GPS, axis=0), zero_blk)
        for tp in range(CHUNK - k):
            t = tp + k
            dst[tp * LANES:(tp + 1) * LANES, t * LANES:(t + 1) * LANES] = blk
    for tp in range(CHUNK):
        for t in range(tp):
            dst[tp * LANES:(tp + 1) * LANES, t * LANES:(t + 1) * LANES] = zero_blk


def _ucat(uv_ref, sg):
    parts = [uv_ref[:, t * SSM_W + sg * LANES:t * SSM_W + (sg + 1) * LANES] for t in range(CHUNK)]
    return jnp.concatenate(parts, axis=1)


def _ssm_local(uv_ref, ws, sre, sim):
    for sg in range(N_SG):
        s = jnp.dot(_ucat(uv_ref, sg), ws[sg], preferred_element_type=F32)
        sre[:, sg * SG_ST:(sg + 1) * SG_ST] = s[:, :SG_ST]
        sim[:, sg * SG_ST:(sg + 1) * SG_ST] = s[:, SG_ST:]


def _ssm_out(uv_ref, wy, kb, sre, sim, yv_ref):
    for sg in range(N_SG):
        sp = jnp.concatenate([sre[:, sg * SG_ST:(sg + 1) * SG_ST],
                              sim[:, sg * SG_ST:(sg + 1) * SG_ST]], axis=1).astype(BF16)
        y = (jnp.dot(sp, wy[sg], preferred_element_type=F32)
             + jnp.dot(_ucat(uv_ref, sg), kb[sg], preferred_element_type=F32))
        for t in range(CHUNK):
            yv_ref[:, t * SSM_W + sg * LANES:t * SSM_W + (sg + 1) * LANES] = (
                y[:, t * LANES:(t + 1) * LANES])


def _ssm_kernel(uvp_ref, uvs_ref, abd_ref, cac_ref, kc_ref, alr_ref, ali_ref, h0r_ref, h0i_ref,
                yvp_ref, yvs_ref, pstr_ref, psti_ref, sstr_ref, ssti_ref, tstr_ref, tsti_ref,
                ws, wy, kb, sre, sim, cre, cim):
    i = pl.program_id(0)

    @pl.when(i == 0)
    def _():
        cre[...] = jnp.zeros_like(cre)
        cim[...] = jnp.zeros_like(cim)
        _expand_chunk_matrices(abd_ref, cac_ref, kc_ref, ws, wy, kb)

    ar = alr_ref[...]
    ai = ali_ref[...]

    @pl.when(i < NPB)
    def _():
        _ssm_local(uvp_ref, ws, sre, sim)

        def step(c, carry):
            sr, si = carry
            lr = sre[pl.ds(c, 1), :]
            li = sim[pl.ds(c, 1), :]
            sre[pl.ds(c, 1), :] = sr
            sim[pl.ds(c, 1), :] = si
            return ar * sr - ai * si + lr, ar * si + ai * sr + li

        sr, si = lax.fori_loop(0, CB, step, (cre[...], cim[...]))
        cre[...] = sr
        cim[...] = si
        pstr_ref[...] = sr
        psti_ref[...] = si
        tstr_ref[...] = sre[CB - TAIL_CR:CB - TAIL_CR + 1, :]
        tsti_ref[...] = sim[CB - TAIL_CR:CB - TAIL_CR + 1, :]
        _ssm_out(uvp_ref, wy, kb, sre, sim, yvp_ref)

    @pl.when(i == NPB)
    def _():
        sre_s = sre.at[0:CR_S]
        sim_s = sim.at[0:CR_S]
        _ssm_local(uvs_ref, ws, sre_s, sim_s)

        def per_stream(q, _):
            sr = h0r_ref[pl.ds(q, 1), :]
            si = h0i_ref[pl.ds(q, 1), :]
            for k in range(CPS):
                c = q * CPS + k
                lr = sre_s[pl.ds(c, 1), :]
                li = sim_s[pl.ds(c, 1), :]
                sre_s[pl.ds(c, 1), :] = sr
                sim_s[pl.ds(c, 1), :] = si
                sr, si = ar * sr - ai * si + lr, ar * si + ai * sr + li
            sstr_ref[pl.ds(q, 1), :] = sr
            ssti_ref[pl.ds(q, 1), :] = si
            return 0

        lax.fori_loop(0, N_SEQ_S, per_stream, 0)
        _ssm_out(uvs_ref, wy, kb, sre_s, sim_s, yvs_ref)


def _ssm_call(uvp, uvs, abd, cac, kc, alr, ali, h0r, h0i):
    c2 = lambda i: (0, 0)
    c4 = lambda i: (0, 0, 0, 0)
    c5 = lambda i: (0, 0, 0, 0, 0)
    one = pl.Buffered(1)
    pblk = lambda i: (jnp.minimum(i, NPB - 1), 0)
    return pl.pallas_call(
        _ssm_kernel,
        grid=(NPB + 1,),
        in_specs=[pl.BlockSpec((CB, UVW), pblk),
                  pl.BlockSpec((CR_S, UVW), c2),
                  pl.BlockSpec((N_SG, CHUNK, 2, LANES, LANES), c5, pipeline_mode=one),
                  pl.BlockSpec((N_SG, CHUNK, 2, N_ST, LANES), c5, pipeline_mode=one),
                  pl.BlockSpec((N_SG, CHUNK, G_CH, LANES), c4, pipeline_mode=one),
                  pl.BlockSpec((1, ST_W), c2),
                  pl.BlockSpec((1, ST_W), c2),
                  pl.BlockSpec((N_SEQ_S, ST_W), c2),
                  pl.BlockSpec((N_SEQ_S, ST_W), c2)],
        out_specs=[pl.BlockSpec((CB, UVW), pblk),
                   pl.BlockSpec((CR_S, UVW), c2),
                   pl.BlockSpec((1, ST_W), c2),
                   pl.BlockSpec((1, ST_W), c2),
                   pl.BlockSpec((N_SEQ_S, ST_W), c2),
                   pl.BlockSpec((N_SEQ_S, ST_W), c2),
                   pl.BlockSpec((1, ST_W), c2),
                   pl.BlockSpec((1, ST_W), c2)],
        out_shape=[jax.ShapeDtypeStruct((CR_P, UVW), F32),
                   jax.ShapeDtypeStruct((CR_S, UVW), F32),
                   jax.ShapeDtypeStruct((1, ST_W), F32),
                   jax.ShapeDtypeStruct((1, ST_W), F32),
                   jax.ShapeDtypeStruct((N_SEQ_S, ST_W), F32),
                   jax.ShapeDtypeStruct((N_SEQ_S, ST_W), F32),
                   jax.ShapeDtypeStruct((1, ST_W), F32),
                   jax.ShapeDtypeStruct((1, ST_W), F32)],
        scratch_shapes=[pltpu.VMEM((N_SG, CW, 2 * SG_ST), BF16),
                        pltpu.VMEM((N_SG, 2 * SG_ST, CW), BF16),
                        pltpu.VMEM((N_SG, CW, CW), BF16),
                        pltpu.VMEM((CB, ST_W), F32), pltpu.VMEM((CB, ST_W), F32),
                        pltpu.VMEM((1, ST_W), F32), pltpu.VMEM((1, ST_W), F32)],
        compiler_params=pltpu.CompilerParams(
            dimension_semantics=("arbitrary",), vmem_limit_bytes=VMEM_LIMIT),
        name="ssm",
    )(uvp, uvs, abd, cac, kc, alr, ali, h0r, h0i)


def _split(a):
    hi = a.astype(BF16)
    return hi, (a - hi.astype(F32)).astype(BF16)


def _dot3(a, b):
    ah, al = _split(a)
    bh, bl = _split(b)
    return (jnp.dot(ah, bh, preferred_element_type=F32)
            + jnp.dot(ah, bl, preferred_element_type=F32)
            + jnp.dot(al, bh, preferred_element_type=F32))


def _tail_kernel(xt_ref, gmix_ref, win_ref, wconv_ref, gb_ref, abd_ref, cac_ref, kc_ref,
                 alr_ref, ali_ref, tsr_ref, tsi_ref, wglu_ref, bglu_ref, ga_ref, wout_ref,
                 xmt_ref, mh, ml, u_scr, sre, sim, y_scr):
    x = xt_ref[...]
    proj = _dot3(_rms(x, gmix_ref[...]), win_ref[...])
    z = proj[:, SSM_W + CONV_W:SSM_W + 2 * CONV_W] * proj[:, SSM_W + 2 * CONV_W:]
    w = wconv_ref[...]
    conv = w[0:1, :] * pltpu.roll(z, 2, 0) + w[1:2, :] * pltpu.roll(z, 1, 0) + w[2:3, :] * z
    nb = _rms(proj[SUBLANES:, SSM_W:SSM_W + CONV_W] * conv[SUBLANES:], gb_ref[...])
    for k in range(N_SG):
        u_scr[k] = proj[SUBLANES:, k * LANES:(k + 1) * LANES]

    def ucat(sg):
        return jnp.concatenate(
            [u_scr[sg, pl.ds(t, TAIL_CR, stride=CHUNK), :] for t in range(CHUNK)], axis=1)

    def dot3m(a):
        ah, al = _split(a)
        return (jnp.dot(ah, mh[...], preferred_element_type=F32)
                + jnp.dot(ah, ml[...], preferred_element_type=F32)
                + jnp.dot(al, mh[...], preferred_element_type=F32))

    for sg in range(N_SG):
        _expand_ws(abd_ref, sg, mh, "hi")
        _expand_ws(abd_ref, sg, ml, "lo")
        s = dot3m(ucat(sg))
        sre[:, sg * SG_ST:(sg + 1) * SG_ST] = s[:, :SG_ST]
        sim[:, sg * SG_ST:(sg + 1) * SG_ST] = s[:, SG_ST:]

    ar = alr_ref[...]
    ai = ali_ref[...]

    def step(c, carry):
        sr, si = carry
        lr = sre[pl.ds(c, 1), :]
        li = sim[pl.ds(c, 1), :]
        sre[pl.ds(c, 1), :] = sr
        sim[pl.ds(c, 1), :] = si
        return ar * sr - ai * si + lr, ar * si + ai * sr + li

    lax.fori_loop(0, TAIL_CR, step, (tsr_ref[...], tsi_ref[...]))

    for sg in range(N_SG):
        sp = jnp.concatenate([sre[:, sg * SG_ST:(sg + 1) * SG_ST],
                              sim[:, sg * SG_ST:(sg + 1) * SG_ST]], axis=1)
        _expand_wy(cac_ref, sg, mh, "hi")
        _expand_wy(cac_ref, sg, ml, "lo")
        y = dot3m(sp)
        _expand_kb(kc_ref, sg, mh, "hi")
        _expand_kb(kc_ref, sg, ml, "lo")
        y = y + dot3m(ucat(sg))
        for t in range(CHUNK):
            y_scr[sg, pl.ds(t, TAIL_CR, stride=CHUNK), :] = y[:, t * LANES:(t + 1) * LANES]

    zg = _gelu(jnp.concatenate([y_scr[k] for k in range(N_SG)], axis=1))
    out_a = zg * _sigmoid(_dot3(zg, wglu_ref[...]) + bglu_ref[...])
    mix = jnp.concatenate([_rms(out_a, ga_ref[...]), nb], axis=1)
    xmt_ref[...] = x[SUBLANES:, :] + _dot3(mix, wout_ref[...])


def _tail_call(xt, gmix, win, wconv, gb, abd, cac, kc, alr, ali, tsr, tsi,
               wglu, bglu, ga, wout):
    def whole(a):
        nd = a.ndim
        return pl.BlockSpec(a.shape, lambda i: (0,) * nd, pipeline_mode=pl.Buffered(1))

    args = (xt, gmix, win, wconv, gb, abd, cac, kc, alr, ali, tsr, tsi, wglu, bglu, ga, wout)
    return pl.pallas_call(
        _tail_kernel,
        grid=(1,),
        in_specs=[whole(a) for a in args],
        out_specs=pl.BlockSpec((TAIL, D_MODEL), lambda i: (0, 0)),
        out_shape=jax.ShapeDtypeStruct((TAIL, D_MODEL), F32),
        scratch_shapes=[pltpu.VMEM((CW, CW), BF16), pltpu.VMEM((CW, CW), BF16),
                        pltpu.VMEM((N_SG, TAIL, LANES), F32),
                        pltpu.VMEM((TAIL_CR, ST_W), F32), pltpu.VMEM((TAIL_CR, ST_W), F32),
                        pltpu.VMEM((N_SG, TAIL, LANES), F32)],
        compiler_params=pltpu.CompilerParams(
            dimension_semantics=("arbitrary",), vmem_limit_bytes=VMEM_LIMIT),
        name="tail",
    )(*args)


def _post_kernel(xp_ref, xs_ref, yvp_ref, yvs_ref, mixb_ref, xmt_ref, wglu_ref, bglu_ref, ga_ref,
                 wout_ref, gffn_ref, wrt_ref, brt_ref,
                 xmid_ref, ridx_ref, rgate_ref, cnt_ref, base_ref, y_scr, logit_ref, before_ref, *,
                 tail_block):
    i = pl.program_id(0)
    is_sample = i == NB - 1

    @pl.when(i == 0)
    def _():
        base_ref[...] = jnp.zeros_like(base_ref)
        ta = lax.broadcasted_iota(I32, (TB, TB), 0)
        tb = lax.broadcasted_iota(I32, (TB, TB), 1)
        before_ref[...] = jnp.where(ta < tb, 1.0, 0.0).astype(BF16)

    yv = jnp.where(is_sample, yvs_ref[...], yvp_ref[...])
    for t in range(CHUNK):
        for k in range(N_SG):
            y_scr[k, pl.ds(t, CRB, stride=CHUNK), :] = (
                yv[:, t * SSM_W + k * LANES:t * SSM_W + (k + 1) * LANES])
    y = jnp.concatenate([y_scr[k] for k in range(N_SG)], axis=1)
    z = _gelu(y)
    gl = jnp.dot(z.astype(BF16), wglu_ref[...], preferred_element_type=F32) + bglu_ref[...]
    out_a = z * _sigmoid(gl)
    mix = jnp.concatenate([_rms(out_a, ga_ref[...]).astype(BF16), mixb_ref[...]], axis=1)
    x = jnp.where(is_sample, xs_ref[...], xp_ref[...])
    xm = x + jnp.dot(mix, wout_ref[...], preferred_element_type=F32)
    if tail_block is not None:
        xm = jnp.where(i == tail_block, xmt_ref[...], xm)
    _put_token_tiles(xmid_ref, xm)
    hn = _rms(xm, gffn_ref[...])

    def dot_nt(a, b):
        return lax.dot_general(a, b, (((1,), (1,)), ((), ())), preferred_element_type=F32)

    h_hi = hn.astype(BF16)
    w_hi, w_lo = _split(wrt_ref[...])
    logit_ref[...] = dot_nt(w_hi, h_hi) + brt_ref[...]
    if tail_block is not None:
        @pl.when(i == tail_block)
        def _():
            h_lo = (hn - h_hi.astype(F32)).astype(BF16)
            logit_ref[...] += dot_nt(w_hi, h_lo) + dot_nt(w_lo, h_hi)
    logit = logit_ref[...]
    fine = logit[0:N_EXP, :]
    coarse = logit[N_EXP:N_EXP + N_EGRP, :]
    cmax = jnp.max(coarse, axis=0, keepdims=True)
    gi = lax.broadcasted_iota(I32, (N_EGRP, TB), 0).astype(F32)
    grp = jnp.min(jnp.where(coarse == cmax, gi, float(N_EGRP)), axis=0, keepdims=True)
    pg = 1.0 / jnp.sum(jnp.exp(coarse - cmax), axis=0, keepdims=True)
    eidx = lax.broadcasted_iota(I32, (N_EXP, TB), 0)
    ei = eidx.astype(F32)
    egrp = (eidx // EPG).astype(F32)
    neg = jnp.float32(-jnp.inf)
    fm = jnp.where(egrp == grp, fine, neg)
    v0 = jnp.max(fm, axis=0, keepdims=True)
    e0 = jnp.min(jnp.where(fm == v0, ei, float(N_EXP)), axis=0, keepdims=True)
    fm2 = jnp.where(ei == e0, neg, fm)
    v1 = jnp.max(fm2, axis=0, keepdims=True)
    e1 = jnp.min(jnp.where(fm2 == v1, ei, float(N_EXP)), axis=0, keepdims=True)
    tt = jnp.exp(v1 - v0)
    g0 = pg / (1.0 + tt)
    g1 = pg * tt / (1.0 + tt)

    sel0 = ei == e0
    sel1 = ei == e1
    cnt = jnp.where(sel0 | sel1, 1.0, 0.0)
    cum = (jnp.dot(cnt.astype(BF16), before_ref[...], preferred_element_type=F32)
           + base_ref[...])
    rank0 = jnp.sum(jnp.where(sel0, cum, 0.0), axis=0, keepdims=True)
    rank1 = jnp.sum(jnp.where(sel1, cum, 0.0), axis=0, keepdims=True)
    base_new = base_ref[...] + jnp.sum(cnt, axis=1, keepdims=True)
    base_ref[...] = base_new
    cnt_ref[...] = jnp.broadcast_to(base_new, (N_EXP, LANES)).astype(I32)

    zi = jnp.zeros((1, TB), I32)
    ridx_ref[...] = jnp.concatenate(
        [e0.astype(I32), e1.astype(I32), rank0.astype(I32), rank1.astype(I32), zi, zi, zi, zi],
        axis=0)
    zf = jnp.zeros((1, TB), F32)
    rgate_ref[...] = jnp.concatenate([g0, g1, zf, zf, zf, zf, zf, zf], axis=0)


def _post_call(xp, xs, s_block, yvp, yvs, mixb, xmt, tail_block, wglu, bglu, ga, wout, gffn,
               wrt, brt):
    const = lambda i: (0, 0)
    return pl.pallas_call(
        functools.partial(_post_kernel, tail_block=tail_block),
        grid=(NB,),
        in_specs=_stream_specs(s_block) + [
            pl.BlockSpec((CRB, UVW), lambda i: (jnp.minimum(i, NBP - 1), 0)),
            pl.BlockSpec((CRB, UVW), const),
            pl.BlockSpec((TB, CONV_W), lambda i: (i, 0)),
            pl.BlockSpec((TB, D_MODEL), const),
            pl.BlockSpec((SSM_W, SSM_W), const),
            pl.BlockSpec((1, SSM_W), const),
            pl.BlockSpec((1, SSM_W), const),
            pl.BlockSpec((D_MODEL, D_MODEL), const),
            pl.BlockSpec((1, D_MODEL), const),
            pl.BlockSpec((RT_ROWS, D_MODEL), const),
            pl.BlockSpec((RT_ROWS, 1), const),
        ],
        out_specs=[
            pl.BlockSpec((TB * SUBLANES, LANES), lambda i: (i, 0)),
            pl.BlockSpec((SUBLANES, TB), lambda i: (0, i)),
            pl.BlockSpec((SUBLANES, TB), lambda i: (0, i)),
            pl.BlockSpec((N_EXP, LANES), const),
        ],
        out_shape=[
            jax.ShapeDtypeStruct((T_ALL * SUBLANES, LANES), F32),
            jax.ShapeDtypeStruct((SUBLANES, T_ALL), I32),
            jax.ShapeDtypeStruct((SUBLANES, T_ALL), F32),
            jax.ShapeDtypeStruct((N_EXP, LANES), I32),
        ],
        scratch_shapes=[pltpu.VMEM((N_EXP, 1), F32), pltpu.VMEM((N_SG, TB, LANES), F32),
                        pltpu.VMEM((RT_ROWS, TB), F32), pltpu.VMEM((TB, TB), BF16)],
        compiler_params=pltpu.CompilerParams(
            dimension_semantics=("arbitrary",), vmem_limit_bytes=VMEM_LIMIT),
        name="post",
    )(xp, xs, yvp, yvs, mixb, xmt, wglu, bglu, ga, wout, gffn, wrt, brt)


def _invert_kernel(cpos0_ref, cpos1_ref, csrc_ref):
    i = pl.program_id(0)

    def put(r, _):
        tok = i * TB + r
        csrc_ref[cpos0_ref[r]] = tok
        csrc_ref[cpos1_ref[r]] = tok
        return 0

    lax.fori_loop(0, TB, put, 0, unroll=ISSUE_UNROLL)


def _invert_call(cpos0, cpos1):
    blk = pl.BlockSpec((TB,), lambda i: (i,), memory_space=pltpu.SMEM)
    return pl.pallas_call(
        _invert_kernel,
        grid=(NB,),
        in_specs=[blk, blk],
        out_specs=pl.BlockSpec(memory_space=pltpu.SMEM),
        out_shape=jax.ShapeDtypeStruct((N_PAIRS,), I32),
        compiler_params=pltpu.CompilerParams(dimension_semantics=("arbitrary",)),
        name="invert",
    )(cpos0, cpos1)


def _expert_kernel(te_ref, nu_ref, ts_ref, csrc_ref, xmid_ref, gffn_ref, wg_ref, wu_ref, wd_ref,
                   r_ref, wgb, wub, wdb, hbuf, xbuf, sems):
    i = pl.program_id(0)
    n_used = nu_ref[0]
    prev = te_ref[jnp.maximum(i - 1, 0)]
    fresh = (i == 0) | (te_ref[i] != prev)

    @pl.when(fresh)
    def _():
        wgb[...] = wg_ref[...].astype(BF16)
        wub[...] = wu_ref[...].astype(BF16)
        wdb[...] = wd_ref[...].astype(BF16)

    def gather_rows(tile, slot):
        base = ts_ref[tile]
        for r in range(TM):
            tok = csrc_ref[jnp.minimum(base + r, N_PAIRS - 1)]
            src = xmid_ref.at[pl.ds(pl.multiple_of(tok * SUBLANES, SUBLANES), SUBLANES)]
            dst = xbuf.at[slot, pl.ds(r * SUBLANES, SUBLANES)]
            pltpu.make_async_copy(src, dst, sems.at[slot]).start(priority=r % N_DMA_THREADS)

    def wait_rows(slot):
        pltpu.make_async_copy(xbuf.at[slot], xbuf.at[slot], sems.at[slot]).wait()

    @pl.when(i == 0)
    def _():
        for a in range(GATHER_AHEAD):
            gather_rows(jnp.minimum(a, n_used - 1), a)

    used = i < n_used
    slot = i % (GATHER_AHEAD + 1)

    @pl.when(used)
    def _():
        wait_rows(slot)
        hbuf[...] = _rms(_get_token_tiles(xbuf.at[slot], TM), gffn_ref[...]).astype(BF16)
        gather_rows(jnp.minimum(i + GATHER_AHEAD, n_used - 1),
                    (i + GATHER_AHEAD) % (GATHER_AHEAD + 1))
        hb = hbuf[...]
        g = jnp.dot(hb, wgb[...], preferred_element_type=F32)
        u = jnp.dot(hb, wub[...], preferred_element_type=F32)
        hid = (g * _sigmoid(g) * u).astype(BF16)
        _put_token_tiles(r_ref, jnp.dot(hid, wdb[...], preferred_element_type=F32))

    @pl.when(i == n_used - 1)
    def _():
        for a in range(1, GATHER_AHEAD + 1):
            wait_rows((i + a) % (GATHER_AHEAD + 1))

    @pl.when(jnp.logical_not(used))
    def _():
        r_ref[...] = jnp.zeros_like(r_ref)


def _expert_call(layer, te, nu, ts, csrc, xmid, gffn, w_gate, w_up, w_down):
    def wmap(i, te, nu, ts, csrc):
        return (layer, te[i], 0, 0)

    return pl.pallas_call(
        _expert_kernel,
        grid_spec=pltpu.PrefetchScalarGridSpec(
            num_scalar_prefetch=4,
            grid=(NT,),
            in_specs=[
                pl.BlockSpec(memory_space=pl.ANY),
                pl.BlockSpec((1, D_MODEL), lambda i, *_: (0, 0)),
                pl.BlockSpec((None, None, D_MODEL, D_FF), wmap),
                pl.BlockSpec((None, None, D_MODEL, D_FF), wmap),
                pl.BlockSpec((None, None, D_FF, D_MODEL), wmap),
            ],
            out_specs=pl.BlockSpec((TM * SUBLANES, LANES), lambda i, *_: (i, 0)),
            scratch_shapes=[pltpu.VMEM((D_MODEL, D_FF), BF16),
                            pltpu.VMEM((D_MODEL, D_FF), BF16),
                            pltpu.VMEM((D_FF, D_MODEL), BF16),
                            pltpu.VMEM((TM, D_MODEL), BF16),
                            pltpu.VMEM((GATHER_AHEAD + 1, TM * SUBLANES, LANES), F32),
                            pltpu.SemaphoreType.DMA((GATHER_AHEAD + 1,))],
        ),
        out_shape=jax.ShapeDtypeStruct((P_CAP * SUBLANES, LANES), F32),
        compiler_params=pltpu.CompilerParams(
            dimension_semantics=("arbitrary",), vmem_limit_bytes=VMEM_LIMIT),
        name="experts",
    )(te, nu, ts, csrc, xmid, gffn, w_gate, w_up, w_down)


def _combine_rows(pos0_ref, pos1_ref, npos0_ref, npos1_ref, xmid_ref, gate_ref, r_ref, buf, sems):
    i = pl.program_id(0)
    slot = i % 2

    @pl.when(i == 0)
    def _():
        _issue_row_gather(pos0_ref, pos1_ref, r_ref, lambda k: buf.at[k], sems.at[0])

    for k in range(2):
        pltpu.make_async_copy(buf.at[k], buf.at[k], sems.at[slot]).wait()

    @pl.when(i < NB - 1)
    def _():
        other = 1 - slot
        _issue_row_gather(npos0_ref, npos1_ref, r_ref, lambda k: buf.at[2 * other + k],
                          sems.at[other])

    gt = gate_ref[...]
    return (_get_token_tiles(xmid_ref, TB)
            + gt[:, 0:1] * _get_token_tiles(buf.at[2 * slot], TB)
            + gt[:, 1:2] * _get_token_tiles(buf.at[2 * slot + 1], TB))


def _combine_specs():
    cur = pl.BlockSpec((TB,), lambda i: (i,), memory_space=pltpu.SMEM)
    nxt = pl.BlockSpec((TB,), lambda i: (jnp.minimum(i + 1, NB - 1),), memory_space=pltpu.SMEM)
    in_specs = [cur, cur, nxt, nxt,
                pl.BlockSpec((TB * SUBLANES, LANES), lambda i: (i, 0)),
                pl.BlockSpec((TB, 2), lambda i: (i, 0)),
                pl.BlockSpec(memory_space=pl.ANY)]
    scratch = [pltpu.VMEM((4, TB * SUBLANES, LANES), F32), pltpu.SemaphoreType.DMA((2,))]
    return in_specs, scratch


def _combine_final_kernel(pos0_ref, pos1_ref, npos0_ref, npos1_ref, xmid_ref, gate_ref, r_ref,
                          gfin_ref, outp_ref, outs_ref, buf, sems):
    i = pl.program_id(0)
    out = _rms(_combine_rows(pos0_ref, pos1_ref, npos0_ref, npos1_ref, xmid_ref, gate_ref, r_ref,
                             buf, sems), gfin_ref[...])

    @pl.when(i < NB - 1)
    def _():
        outp_ref[...] = out

    @pl.when(i == NB - 1)
    def _():
        outs_ref[...] = out


def _combine_final_call(pos0, pos1, xmid, gate_t, r, gfin):
    in_specs, scratch = _combine_specs()
    return pl.pallas_call(
        _combine_final_kernel,
        grid=(NB,),
        in_specs=in_specs + [pl.BlockSpec((1, D_MODEL), lambda i: (0, 0))],
        out_specs=[pl.BlockSpec((TB, D_MODEL), lambda i: (jnp.minimum(i, NBP - 1), 0)),
                   pl.BlockSpec((TB, D_MODEL), lambda i: (0, 0))],
        out_shape=[jax.ShapeDtypeStruct((T_PROMPT, D_MODEL), F32),
                   jax.ShapeDtypeStruct((T_SAMPLE, D_MODEL), F32)],
        scratch_shapes=scratch,
        compiler_params=pltpu.CompilerParams(
            dimension_semantics=("arbitrary",), vmem_limit_bytes=VMEM_LIMIT),
        name="combine_final",
    )(pos0, pos1, pos0, pos1, xmid, gate_t, r, gfin)


def _ssm_compact(lam_re, lam_im, log_dt, b_re, b_im, c_re, c_im, d_skip):
    nl = lam_re.shape[0]
    dt = jnp.exp(log_dt)[..., None]
    kpow = jnp.arange(CHUNK + 1, dtype=F32).reshape(CHUNK + 1, 1, 1, 1)
    mag = jnp.exp(kpow * (lam_re * dt))
    pw_re = mag * jnp.cos(kpow * (lam_im * dt))
    pw_im = mag * jnp.sin(kpow * (lam_im * dt))
    ab_re, ab_im = pw_re[1], pw_im[1]
    denom = lam_re * lam_re + lam_im * lam_im
    num_re = ab_re - 1.0
    f_re = (num_re * lam_re + ab_im * lam_im) / denom
    f_im = (ab_im * lam_re - num_re * lam_im) / denom
    bb_re = f_re[..., None] * b_re - f_im[..., None] * b_im
    bb_im = f_re[..., None] * b_im + f_im[..., None] * b_re

    krev = (CHUNK - 1.0) - kpow[:CHUNK]
    rmag = jnp.exp(krev * (lam_re * dt))
    rev_re = (rmag * jnp.cos(krev * (lam_im * dt)))[..., None]
    rev_im = (rmag * jnp.sin(krev * (lam_im * dt)))[..., None]
    ab = jnp.stack([rev_re * bb_re - rev_im * bb_im, rev_re * bb_im + rev_im * bb_re])
    ab = ab.reshape(2, CHUNK, nl, N_SG, GPS, N_ST, G_CH)
    ab = ab.transpose(2, 3, 1, 0, 4, 6, 5).reshape(nl, N_SG, CHUNK, 2, LANES, N_ST)
    abd = jnp.concatenate([ab, ab], axis=-1)

    pr = pw_re[:, :, :, None, :]
    pi = pw_im[:, :, :, None, :]
    ca_re = c_re * pr - c_im * pi
    ca_im = c_re * pi + c_im * pr
    ca = jnp.stack([ca_re[1:], -ca_im[1:]])
    ca = ca.reshape(2, CHUNK, nl, N_SG, GPS, G_CH, N_ST)
    cac = ca.transpose(2, 3, 1, 0, 6, 4, 5).reshape(nl, N_SG, CHUNK, 2, N_ST, LANES)

    bq_re = jnp.swapaxes(bb_re, -1, -2)[:, :, None]
    bq_im = jnp.swapaxes(bb_im, -1, -2)[:, :, None]
    kk = jnp.sum(ca_re[:CHUNK, :, :, :, None, :] * bq_re
                 - ca_im[:CHUNK, :, :, :, None, :] * bq_im, axis=-1)
    skip = d_skip.reshape(nl, N_G, G_CH)[..., None] * jnp.eye(G_CH, dtype=F32)
    kk = jnp.concatenate([kk[:1] + skip[None], kk[1:]], axis=0)
    kk = kk.reshape(CHUNK, nl, N_SG, GPS, G_CH, G_CH)
    kk = kk.transpose(1, 2, 0, 5, 3, 4)
    kc = kk.reshape(nl, N_SG, CHUNK, G_CH, LANES)

    al_re = pw_re[CHUNK].reshape(nl, 1, ST_W)
    al_im = pw_im[CHUNK].reshape(nl, 1, ST_W)
    return abd, cac, kc, al_re, al_im


def _tile_tables(counts):
    padded = ((counts + TM - 1) // TM) * TM
    ends = jnp.cumsum(padded)
    offs = ends - padded
    end_tile = ends // TM
    n_used = end_tile[-1].astype(I32)
    tile = jnp.arange(NT, dtype=I32)
    live = jnp.minimum(tile, n_used - 1)
    te = jnp.sum(end_tile[None, :] <= live[:, None], axis=1).astype(I32)
    cstart = jnp.cumsum(counts) - counts
    mine = te[:, None] == jnp.arange(N_EXP, dtype=I32)[None, :]
    first_row = jnp.sum(jnp.where(mine, (cstart - offs)[None, :], 0), axis=1)
    ts = (first_row + live * TM).astype(I32)
    return offs.astype(I32), cstart.astype(I32), te, n_used.reshape(1), ts


def _sorted_positions(ridx, offs, cstart):
    experts = jnp.arange(N_EXP, dtype=I32)[:, None]
    out = []
    for k in range(2):
        hit = ridx[k][None, :] == experts
        for table in (offs, cstart):
            out.append(jnp.sum(jnp.where(hit, table[:, None], 0), axis=0) + ridx[2 + k])
    pos0, cpos0, pos1, cpos1 = out
    return pos0, pos1, cpos0, cpos1


def kernel(x_prompt, x_sample, state_ssm_re, state_ssm_im, cache_conv, norm_mix, w_in, lam_re,
           lam_im, log_dt, ssm_b_re, ssm_b_im, ssm_c_re, ssm_c_im, ssm_d, w_glu, b_glu, w_conv,
           norm_a, norm_b, w_out, norm_ffn, w_router_group, b_router_group, w_router_expert,
           b_router_expert, w_gate, w_up, w_down, norm_final):
    xp = x_prompt.reshape(T_PROMPT, D_MODEL)
    xs = x_sample.reshape(T_SAMPLE, D_MODEL)
    s_block = 0
    gfin = norm_final.reshape(1, D_MODEL)
    abd, cac, kc, al_re, al_im = _ssm_compact(
        lam_re, lam_im, log_dt, ssm_b_re, ssm_b_im, ssm_c_re, ssm_c_im, ssm_d)
    p_re, p_im, p_buf, s_re, s_im, s_buf = [], [], [], [], [], []
    moe = None
    for l in range(DEPTH):
        cache = cache_conv[l]
        cz1 = jnp.pad(cache[:, 1:2], ((0, 0), (0, LEN_S - 1), (0, 0))).reshape(T_SAMPLE, CONV_W)
        cz2 = jnp.pad(cache, ((0, 0), (0, LEN_S - 2), (0, 0))).reshape(T_SAMPLE, CONV_W)
        inproj_params = (norm_mix[l].reshape(1, D_MODEL), w_in[l].astype(BF16), w_conv[l],
                         norm_b[l].reshape(1, CONV_W), cz1, cz2)
        if moe is None:
            uvp, uvs, mixb, ptail, zs = _inproj_call(xp, xs, s_block, *inproj_params)
        else:
            x, uvp, uvs, mixb, ptail, zs = _combine_inproj_call(*moe, *inproj_params)
            xp = xs = x
            s_block = NB - 1

        yvp, yvs, pst_r, pst_i, sst_r, sst_i, tst_r, tst_i = _ssm_call(
            uvp, uvs, abd[l].astype(BF16), cac[l].astype(BF16), kc[l].astype(BF16),
            al_re[l], al_im[l],
            state_ssm_re[l].reshape(N_SEQ_S, ST_W), state_ssm_im[l].reshape(N_SEQ_S, ST_W))

        tail_block = NBP - 1 if l < DEPTH - 1 else None
        if tail_block is not None:
            xmt = _tail_call(
                xp[T_PROMPT - TAIL_X:T_PROMPT], norm_mix[l].reshape(1, D_MODEL), w_in[l],
                w_conv[l], norm_b[l].reshape(1, CONV_W), abd[l], cac[l], kc[l],
                al_re[l], al_im[l], tst_r, tst_i, w_glu[l], b_glu[l].reshape(1, SSM_W),
                norm_a[l].reshape(1, SSM_W), w_out[l])

        wrt = jnp.concatenate([
            w_router_expert[l].transpose(0, 2, 1).reshape(N_EXP, D_MODEL),
            w_router_group[l].T,
            jnp.zeros((RT_ROWS - N_EXP - N_EGRP, D_MODEL), F32)], axis=0)
        brt = jnp.concatenate([b_router_expert[l].reshape(N_EXP), b_router_group[l],
                               jnp.zeros((RT_ROWS - N_EXP - N_EGRP,), F32)]).reshape(RT_ROWS, 1)
        gffn = norm_ffn[l].reshape(1, D_MODEL)
        xmid, ridx, rgate, cnt = _post_call(
            xp, xs, s_block, yvp, yvs, mixb, xmt, tail_block,
            w_glu[l].astype(BF16), b_glu[l].reshape(1, SSM_W),
            norm_a[l].reshape(1, SSM_W), w_out[l].astype(BF16), gffn, wrt, brt)

        offs, cstart, te, n_used, ts = _tile_tables(cnt[:, 0])
        pos0, pos1, cpos0, cpos1 = _sorted_positions(ridx, offs, cstart)
        csrc = _invert_call(cpos0, cpos1)
        r = _expert_call(l, te, n_used, ts, csrc, xmid, gffn, w_gate, w_up, w_down)
        moe = (pos0, pos1, xmid, rgate[:2].T, r)

        p_re.append(pst_r.reshape(1, N_G, N_ST))
        p_im.append(pst_i.reshape(1, N_G, N_ST))
        p_buf.append(ptail[SUBLANES - 2:].reshape(1, 2, CONV_W))
        s_re.append(sst_r.reshape(N_SEQ_S, N_G, N_ST))
        s_im.append(sst_i.reshape(N_SEQ_S, N_G, N_ST))
        s_buf.append(zs.reshape(N_SEQ_S, LEN_S, CONV_W)[:, LEN_S - 2:])

    y_prompt, y_sample = _combine_final_call(*moe, gfin)
    return (y_prompt.reshape(1, T_PROMPT, D_MODEL), y_sample.reshape(N_SEQ_S, LEN_S, D_MODEL),
            jnp.stack(p_re), jnp.stack(p_im), jnp.stack(p_buf),
            jnp.stack(s_re), jnp.stack(s_im), jnp.stack(s_buf))
```

```python
import functools

import jax
import jax.numpy as jnp
from jax import lax
from jax.experimental import pallas as pl
from jax.experimental.pallas import tpu as pltpu

F32 = jnp.float32
BF16 = jnp.bfloat16
I32 = jnp.int32

D_MODEL = 1024
DEPTH = 2
T_PROMPT = 16384
N_SEQ_S = 32
LEN_S = 16
T_SAMPLE = N_SEQ_S * LEN_S
T_ALL = T_PROMPT + T_SAMPLE
SSM_W = 512
CONV_W = 512
N_G = 32
G_CH = 16
N_ST = 64
ST_W = N_G * N_ST
N_EGRP = 4
EPG = 8
N_EXP = 32
D_FF = 512
EPS = 1e-6

LANES = 128
SUBLANES = 8
TB = 512
NB = T_ALL // TB
NBP = T_PROMPT // TB
CHUNK = 8
CRB = TB // CHUNK
GPS = LANES // G_CH
N_SG = SSM_W // LANES
SG_ST = GPS * N_ST
CW = CHUNK * LANES
UVW = CHUNK * SSM_W
CR_P = T_PROMPT // CHUNK
CR_S = T_SAMPLE // CHUNK
CPS = LEN_S // CHUNK
CB = 256
NPB = CR_P // CB
TM = 256
P_CAP = 2 * T_ALL + N_EXP * TM
NT = P_CAP // TM
N_PAIRS = 2 * T_ALL
TAIL = TB
TAIL_X = TAIL + SUBLANES
TAIL_CR = TAIL // CHUNK
N_DMA_THREADS = 2
GATHER_AHEAD = 2
ISSUE_UNROLL = 8
RT_ROWS = 40

VMEM_LIMIT = 56 * 1024 * 1024

assert T_SAMPLE == TB and TB % CHUNK == 0 and CR_S <= CB and LEN_S % CHUNK == 0


def _rms(x, g):
    return x * lax.rsqrt(jnp.mean(x * x, axis=-1, keepdims=True) + EPS) * g


def _sigmoid(x):
    return 1.0 / (1.0 + jnp.exp(-x))


def _gelu(y):
    return 0.5 * y * (1.0 + jnp.tanh(0.7978845608028654 * (y + 0.044715 * (y * y * y))))


def _put_token_tiles(ref, x):
    rows = x.shape[0]
    for j in range(D_MODEL // LANES):
        ref[pl.ds(j, rows, stride=SUBLANES), :] = x[:, j * LANES:(j + 1) * LANES]


def _get_token_tiles(ref, rows):
    return jnp.concatenate(
        [ref[pl.ds(j, rows, stride=SUBLANES), :] for j in range(D_MODEL // LANES)], axis=1)


def _stream_specs(s_block):
    return [pl.BlockSpec((TB, D_MODEL), lambda i: (jnp.minimum(i, NBP - 1), 0)),
            pl.BlockSpec((TB, D_MODEL), lambda i: (s_block, 0))]


def _inproj_kernel(xp_ref, xs_ref, *rest):
    is_sample = pl.program_id(0) == NB - 1
    _inproj_body(jnp.where(is_sample, xs_ref[...], xp_ref[...]), *rest)


def _inproj_body(x, gmix_ref, win_ref, wconv_ref, gb_ref, cz1_ref, cz2_ref,
                 uvp_ref, uvs_ref, mixb_ref, ptail_ref, zs_ref, carry_ref, u_scr):
    i = pl.program_id(0)
    is_sample = i == NB - 1

    @pl.when(i == 0)
    def _():
        carry_ref[...] = jnp.zeros_like(carry_ref)

    h = _rms(x, gmix_ref[...]).astype(BF16)
    proj = jnp.dot(h, win_ref[...], preferred_element_type=F32)
    for k in range(N_SG):
        u_scr[k] = proj[:, k * LANES:(k + 1) * LANES]
    gate_b = proj[:, SSM_W:SSM_W + CONV_W]
    gate_c = proj[:, SSM_W + CONV_W:SSM_W + 2 * CONV_W]
    v = proj[:, SSM_W + 2 * CONV_W:]
    z = gate_c * v
    row = lax.broadcasted_iota(I32, (TB, 1), 0)
    r1 = pltpu.roll(z, 1, 0)
    r2 = pltpu.roll(z, 2, 0)
    c6 = carry_ref[6:7, :]
    c7 = carry_ref[7:8, :]
    z1p = jnp.where(row == 0, c7, r1)
    z2p = jnp.where(row == 0, c6, jnp.where(row == 1, c7, r2))
    pos = row % LEN_S
    z1s = jnp.where(pos == 0, cz1_ref[...], r1)
    z2s = jnp.where(pos < 2, cz2_ref[...], r2)
    z1 = jnp.where(is_sample, z1s, z1p)
    z2 = jnp.where(is_sample, z2s, z2p)
    w = wconv_ref[...]
    conv = w[0:1, :] * z2 + w[1:2, :] * z1 + w[2:3, :] * z
    yb = gate_b * conv
    mixb_ref[...] = _rms(yb, gb_ref[...]).astype(BF16)
    carry_ref[...] = z[TB - SUBLANES:, :]

    def put_chunk_rows(uv_ref):
        for t in range(CHUNK):
            for k in range(N_SG):
                uv_ref[:, t * SSM_W + k * LANES:t * SSM_W + (k + 1) * LANES] = (
                    u_scr[k, pl.ds(t, CRB, stride=CHUNK), :].astype(BF16))

    @pl.when(jnp.logical_not(is_sample))
    def _():
        put_chunk_rows(uvp_ref)

    @pl.when(i == NB - 2)
    def _():
        ptail_ref[...] = z[TB - SUBLANES:, :]

    @pl.when(is_sample)
    def _():
        put_chunk_rows(uvs_ref)
        zs_ref[...] = z


def _inproj_specs():
    const = lambda i: (0, 0)
    in_specs = [
        pl.BlockSpec((1, D_MODEL), const),
        pl.BlockSpec((D_MODEL, 4 * SSM_W), const),
        pl.BlockSpec((3, CONV_W), const),
        pl.BlockSpec((1, CONV_W), const),
        pl.BlockSpec((TB, CONV_W), const),
        pl.BlockSpec((TB, CONV_W), const),
    ]
    out_specs = [
        pl.BlockSpec((CRB, UVW), lambda i: (jnp.minimum(i, NBP - 1), 0)),
        pl.BlockSpec((CRB, UVW), const),
        pl.BlockSpec((TB, CONV_W), lambda i: (i, 0)),
        pl.BlockSpec((SUBLANES, CONV_W), const),
        pl.BlockSpec((TB, CONV_W), const),
    ]
    out_shape = [
        jax.ShapeDtypeStruct((CR_P, UVW), BF16),
        jax.ShapeDtypeStruct((CR_S, UVW), BF16),
        jax.ShapeDtypeStruct((T_ALL, CONV_W), BF16),
        jax.ShapeDtypeStruct((SUBLANES, CONV_W), F32),
        jax.ShapeDtypeStruct((TB, CONV_W), F32),
    ]
    scratch = [pltpu.VMEM((SUBLANES, CONV_W), F32), pltpu.VMEM((N_SG, TB, LANES), F32)]
    return in_specs, out_specs, out_shape, scratch


def _inproj_call(xp, xs, s_block, gmix, win, wconv, gb, cz1, cz2):
    in_specs, out_specs, out_shape, scratch = _inproj_specs()
    return pl.pallas_call(
        _inproj_kernel,
        grid=(NB,),
        in_specs=_stream_specs(s_block) + in_specs,
        out_specs=out_specs,
        out_shape=out_shape,
        scratch_shapes=scratch,
        compiler_params=pltpu.CompilerParams(
            dimension_semantics=("arbitrary",), vmem_limit_bytes=VMEM_LIMIT),
        name="inproj",
    )(xp, xs, gmix, win, wconv, gb, cz1, cz2)


def _issue_row_gather(pos0_ref, pos1_ref, r_ref, dst_of, sem):
    def issue(r, _):
        for k, pos_ref in enumerate((pos0_ref, pos1_ref)):
            p = pos_ref[r]
            src = r_ref.at[pl.ds(pl.multiple_of(p * SUBLANES, SUBLANES), SUBLANES)]
            dst = dst_of(k).at[pl.ds(pl.multiple_of(r * SUBLANES, SUBLANES), SUBLANES)]
            pltpu.make_async_copy(src, dst, sem).start(priority=k % N_DMA_THREADS)
        return 0

    lax.fori_loop(0, TB, issue, 0, unroll=ISSUE_UNROLL)


def _combine_inproj_kernel(pos0_ref, pos1_ref, npos0_ref, npos1_ref, xmid_ref, gate_ref, r_ref,
                           gmix_ref, win_ref, wconv_ref, gb_ref, cz1_ref, cz2_ref,
                           x_ref, uvp_ref, uvs_ref, mixb_ref, ptail_ref, zs_ref,
                           buf, sems, carry_ref, u_scr):
    x = _combine_rows(pos0_ref, pos1_ref, npos0_ref, npos1_ref, xmid_ref, gate_ref, r_ref,
                      buf, sems)
    x_ref[...] = x
    _inproj_body(x, gmix_ref, win_ref, wconv_ref, gb_ref, cz1_ref, cz2_ref,
                 uvp_ref, uvs_ref, mixb_ref, ptail_ref, zs_ref, carry_ref, u_scr)


def _combine_inproj_call(pos0, pos1, xmid, gate_t, r, gmix, win, wconv, gb, cz1, cz2):
    in_specs, out_specs, out_shape, scratch = _inproj_specs()
    c_specs, c_scratch = _combine_specs()
    return pl.pallas_call(
        _combine_inproj_kernel,
        grid=(NB,),
        in_specs=c_specs + in_specs,
        out_specs=[pl.BlockSpec((TB, D_MODEL), lambda i: (i, 0))] + out_specs,
        out_shape=[jax.ShapeDtypeStruct((T_ALL, D_MODEL), F32)] + out_shape,
        scratch_shapes=c_scratch + scratch,
        compiler_params=pltpu.CompilerParams(
            dimension_semantics=("arbitrary",), vmem_limit_bytes=VMEM_LIMIT),
        name="combine_inproj",
    )(pos0, pos1, pos0, pos1, xmid, gate_t, r, gmix, win, wconv, gb, cz1, cz2)


def _expand_chunk_matrices(abd_ref, cac_ref, kc_ref, ws, wy, kb):
    for s in range(N_SG):
        _expand_ws(abd_ref, s, ws.at[s])
        _expand_wy(cac_ref, s, wy.at[s])
        _expand_kb(kc_ref, s, kb.at[s])


def _bf16_part(a, part):
    if part is None:
        return a
    hi = a.astype(BF16)
    return hi if part == "hi" else (a - hi.astype(F32)).astype(BF16)


def _expand_ws(abd_ref, s, dst, part=None):
    row_g = lax.broadcasted_iota(I32, (LANES, LANES), 0) // G_CH
    lane_half = lax.broadcasted_iota(I32, (LANES, LANES), 1) // N_ST
    for t in range(CHUNK):
        for r in range(2):
            a = _bf16_part(abd_ref[s, t, r], part)
            for m in range(GPS // 2):
                dst[t * LANES:(t + 1) * LANES,
                    r * SG_ST + m * LANES:r * SG_ST + (m + 1) * LANES] = jnp.where(
                        row_g == 2 * m + lane_half, a, jnp.zeros_like(a))


def _expand_wy(cac_ref, s, dst, part=None):
    col_g = lax.broadcasted_iota(I32, (N_ST, LANES), 1) // G_CH
    for t in range(CHUNK):
        for r in range(2):
            c = _bf16_part(cac_ref[s, t, r], part)
            for g in range(GPS):
                dst[r * SG_ST + g * N_ST:r * SG_ST + (g + 1) * N_ST,
                    t * LANES:(t + 1) * LANES] = jnp.where(col_g == g, c, jnp.zeros_like(c))


def _expand_kb(kc_ref, s, dst, part=None):
    row_g = lax.broadcasted_iota(I32, (LANES, LANES), 0) // G_CH
    col_g = lax.broadcasted_iota(I32, (LANES, LANES), 1) // G_CH
    zero_blk = jnp.zeros((LANES, LANES), BF16)
    for k in range(CHUNK):
        kq = _bf16_part(kc_ref[s, k], part)
        blk = jnp.where(row_g == col_g, jnp.concatenate([kq] * GPS, axis=0), zero_blk)
        for tp in range(CHUNK - k):
            t = tp + k
            dst[tp * LANES:(tp + 1) * LANES, t * LANES:(t + 1) * LANES] = blk
    for tp in range(CHUNK):
        for t in range(tp):
            dst[tp * LANES:(tp + 1) * LANES, t * LANES:(t + 1) * LANES] = zero_blk


def _ucat(uv_ref, sg):
    parts = [uv_ref[:, t * SSM_W + sg * LANES:t * SSM_W + (sg + 1) * LANES] for t in range(CHUNK)]
    return jnp.concatenate(parts, axis=1)


def _ssm_local(uv_ref, ws, sre, sim):
    for sg in range(N_SG):
        s = jnp.dot(_ucat(uv_ref, sg), ws[sg], preferred_element_type=F32)
        sre[:, sg * SG_ST:(sg + 1) * SG_ST] = s[:, :SG_ST]
        sim[:, sg * SG_ST:(sg + 1) * SG_ST] = s[:, SG_ST:]


def _ssm_out(uv_ref, wy, kb, sre, sim, yv_ref):
    for sg in range(N_SG):
        sp = jnp.concatenate([sre[:, sg * SG_ST:(sg + 1) * SG_ST],
                              sim[:, sg * SG_ST:(sg + 1) * SG_ST]], axis=1).astype(BF16)
        y = (jnp.dot(sp, wy[sg], preferred_element_type=F32)
             + jnp.dot(_ucat(uv_ref, sg), kb[sg], preferred_element_type=F32))
        for t in range(CHUNK):
            yv_ref[:, t * SSM_W + sg * LANES:t * SSM_W + (sg + 1) * LANES] = (
                y[:, t * LANES:(t + 1) * LANES])


def _ssm_kernel(uvp_ref, uvs_ref, abd_ref, cac_ref, kc_ref, alr_ref, ali_ref, h0r_ref, h0i_ref,
                yvp_ref, yvs_ref, pstr_ref, psti_ref, sstr_ref, ssti_ref, tstr_ref, tsti_ref,
                ws, wy, kb, sre, sim, cre, cim):
    i = pl.program_id(0)

    @pl.when(i == 0)
    def _():
        cre[...] = jnp.zeros_like(cre)
        cim[...] = jnp.zeros_like(cim)
        _expand_chunk_matrices(abd_ref, cac_ref, kc_ref, ws, wy, kb)

    ar = alr_ref[...]
    ai = ali_ref[...]

    @pl.when(i < NPB)
    def _():
        _ssm_local(uvp_ref, ws, sre, sim)

        def step(c, carry):
            sr, si = carry
            lr = sre[pl.ds(c, 1), :]
            li = sim[pl.ds(c, 1), :]
            sre[pl.ds(c, 1), :] = sr
            sim[pl.ds(c, 1), :] = si
            return ar * sr - ai * si + lr, ar * si + ai * sr + li

        sr, si = lax.fori_loop(0, CB, step, (cre[...], cim[...]))
        cre[...] = sr
        cim[...] = si
        pstr_ref[...] = sr
        psti_ref[...] = si
        tstr_ref[...] = sre[CB - TAIL_CR:CB - TAIL_CR + 1, :]
        tsti_ref[...] = sim[CB - TAIL_CR:CB - TAIL_CR + 1, :]
        _ssm_out(uvp_ref, wy, kb, sre, sim, yvp_ref)

    @pl.when(i == NPB)
    def _():
        sre_s = sre.at[0:CR_S]
        sim_s = sim.at[0:CR_S]
        _ssm_local(uvs_ref, ws, sre_s, sim_s)

        def per_stream(q, _):
            sr = h0r_ref[pl.ds(q, 1), :]
            si = h0i_ref[pl.ds(q, 1), :]
            for k in range(CPS):
                c = q * CPS + k
                lr = sre_s[pl.ds(c, 1), :]
                li = sim_s[pl.ds(c, 1), :]
                sre_s[pl.ds(c, 1), :] = sr
                sim_s[pl.ds(c, 1), :] = si
                sr, si = ar * sr - ai * si + lr, ar * si + ai * sr + li
            sstr_ref[pl.ds(q, 1), :] = sr
            ssti_ref[pl.ds(q, 1), :] = si
            return 0

        lax.fori_loop(0, N_SEQ_S, per_stream, 0)
        _ssm_out(uvs_ref, wy, kb, sre_s, sim_s, yvs_ref)


def _ssm_call(uvp, uvs, abd, cac, kc, alr, ali, h0r, h0i):
    c2 = lambda i: (0, 0)
    c4 = lambda i: (0, 0, 0, 0)
    c5 = lambda i: (0, 0, 0, 0, 0)
    one = pl.Buffered(1)
    pblk = lambda i: (jnp.minimum(i, NPB - 1), 0)
    return pl.pallas_call(
        _ssm_kernel,
        grid=(NPB + 1,),
        in_specs=[pl.BlockSpec((CB, UVW), pblk),
                  pl.BlockSpec((CR_S, UVW), c2),
                  pl.BlockSpec((N_SG, CHUNK, 2, LANES, LANES), c5, pipeline_mode=one),
                  pl.BlockSpec((N_SG, CHUNK, 2, N_ST, LANES), c5, pipeline_mode=one),
                  pl.BlockSpec((N_SG, CHUNK, G_CH, LANES), c4, pipeline_mode=one),
                  pl.BlockSpec((1, ST_W), c2),
                  pl.BlockSpec((1, ST_W), c2),
                  pl.BlockSpec((N_SEQ_S, ST_W), c2),
                  pl.BlockSpec((N_SEQ_S, ST_W), c2)],
        out_specs=[pl.BlockSpec((CB, UVW), pblk),
                   pl.BlockSpec((CR_S, UVW), c2),
                   pl.BlockSpec((1, ST_W), c2),
                   pl.BlockSpec((1, ST_W), c2),
                   pl.BlockSpec((N_SEQ_S, ST_W), c2),
                   pl.BlockSpec((N_SEQ_S, ST_W), c2),
                   pl.BlockSpec((1, ST_W), c2),
                   pl.BlockSpec((1, ST_W), c2)],
        out_shape=[jax.ShapeDtypeStruct((CR_P, UVW), F32),
                   jax.ShapeDtypeStruct((CR_S, UVW), F32),
                   jax.ShapeDtypeStruct((1, ST_W), F32),
                   jax.ShapeDtypeStruct((1, ST_W), F32),
                   jax.ShapeDtypeStruct((N_SEQ_S, ST_W), F32),
                   jax.ShapeDtypeStruct((N_SEQ_S, ST_W), F32),
                   jax.ShapeDtypeStruct((1, ST_W), F32),
                   jax.ShapeDtypeStruct((1, ST_W), F32)],
        scratch_shapes=[pltpu.VMEM((N_SG, CW, 2 * SG_ST), BF16),
                        pltpu.VMEM((N_SG, 2 * SG_ST, CW), BF16),
                        pltpu.VMEM((N_SG, CW, CW), BF16),
                        pltpu.VMEM((CB, ST_W), F32), pltpu.VMEM((CB, ST_W), F32),
                        pltpu.VMEM((1, ST_W), F32), pltpu.VMEM((1, ST_W), F32)],
        compiler_params=pltpu.CompilerParams(
            dimension_semantics=("arbitrary",), vmem_limit_bytes=VMEM_LIMIT),
        name="ssm",
    )(uvp, uvs, abd, cac, kc, alr, ali, h0r, h0i)


def _split(a):
    hi = a.astype(BF16)
    return hi, (a - hi.astype(F32)).astype(BF16)


def _dot3(a, b):
    ah, al = _split(a)
    bh, bl = _split(b)
    return (jnp.dot(ah, bh, preferred_element_type=F32)
            + jnp.dot(ah, bl, preferred_element_type=F32)
            + jnp.dot(al, bh, preferred_element_type=F32))


def _tail_kernel(xt_ref, gmix_ref, win_ref, wconv_ref, gb_ref, abd_ref, cac_ref, kc_ref,
                 alr_ref, ali_ref, tsr_ref, tsi_ref, wglu_ref, bglu_ref, ga_ref, wout_ref,
                 xmt_ref, mh, ml, u_scr, sre, sim, y_scr):
    x = xt_ref[...]
    proj = _dot3(_rms(x, gmix_ref[...]), win_ref[...])
    z = proj[:, SSM_W + CONV_W:SSM_W + 2 * CONV_W] * proj[:, SSM_W + 2 * CONV_W:]
    w = wconv_ref[...]
    conv = w[0:1, :] * pltpu.roll(z, 2, 0) + w[1:2, :] * pltpu.roll(z, 1, 0) + w[2:3, :] * z
    nb = _rms(proj[SUBLANES:, SSM_W:SSM_W + CONV_W] * conv[SUBLANES:], gb_ref[...])
    for k in range(N_SG):
        u_scr[k] = proj[SUBLANES:, k * LANES:(k + 1) * LANES]

    def ucat(sg):
        return jnp.concatenate(
            [u_scr[sg, pl.ds(t, TAIL_CR, stride=CHUNK), :] for t in range(CHUNK)], axis=1)

    def dot3m(a):
        ah, al = _split(a)
        return (jnp.dot(ah, mh[...], preferred_element_type=F32)
                + jnp.dot(ah, ml[...], preferred_element_type=F32)
                + jnp.dot(al, mh[...], preferred_element_type=F32))

    for sg in range(N_SG):
        _expand_ws(abd_ref, sg, mh, "hi")
        _expand_ws(abd_ref, sg, ml, "lo")
        s = dot3m(ucat(sg))
        sre[:, sg * SG_ST:(sg + 1) * SG_ST] = s[:, :SG_ST]
        sim[:, sg * SG_ST:(sg + 1) * SG_ST] = s[:, SG_ST:]

    ar = alr_ref[...]
    ai = ali_ref[...]

    def step(c, carry):
        sr, si = carry
        lr = sre[pl.ds(c, 1), :]
        li = sim[pl.ds(c, 1), :]
        sre[pl.ds(c, 1), :] = sr
        sim[pl.ds(c, 1), :] = si
        return ar * sr - ai * si + lr, ar * si + ai * sr + li

    lax.fori_loop(0, TAIL_CR, step, (tsr_ref[...], tsi_ref[...]))

    for sg in range(N_SG):
        sp = jnp.concatenate([sre[:, sg * SG_ST:(sg + 1) * SG_ST],
                              sim[:, sg * SG_ST:(sg + 1) * SG_ST]], axis=1)
        _expand_wy(cac_ref, sg, mh, "hi")
        _expand_wy(cac_ref, sg, ml, "lo")
        y = dot3m(sp)
        _expand_kb(kc_ref, sg, mh, "hi")
        _expand_kb(kc_ref, sg, ml, "lo")
        y = y + dot3m(ucat(sg))
        for t in range(CHUNK):
            y_scr[sg, pl.ds(t, TAIL_CR, stride=CHUNK), :] = y[:, t * LANES:(t + 1) * LANES]

    zg = _gelu(jnp.concatenate([y_scr[k] for k in range(N_SG)], axis=1))
    out_a = zg * _sigmoid(_dot3(zg, wglu_ref[...]) + bglu_ref[...])
    mix = jnp.concatenate([_rms(out_a, ga_ref[...]), nb], axis=1)
    xmt_ref[...] = x[SUBLANES:, :] + _dot3(mix, wout_ref[...])


def _tail_call(xt, gmix, win, wconv, gb, abd, cac, kc, alr, ali, tsr, tsi,
               wglu, bglu, ga, wout):
    def whole(a):
        nd = a.ndim
        return pl.BlockSpec(a.shape, lambda i: (0,) * nd, pipeline_mode=pl.Buffered(1))

    args = (xt, gmix, win, wconv, gb, abd, cac, kc, alr, ali, tsr, tsi, wglu, bglu, ga, wout)
    return pl.pallas_call(
        _tail_kernel,
        grid=(1,),
        in_specs=[whole(a) for a in args],
        out_specs=pl.BlockSpec((TAIL, D_MODEL), lambda i: (0, 0)),
        out_shape=jax.ShapeDtypeStruct((TAIL, D_MODEL), F32),
        scratch_shapes=[pltpu.VMEM((CW, CW), BF16), pltpu.VMEM((CW, CW), BF16),
                        pltpu.VMEM((N_SG, TAIL, LANES), F32),
                        pltpu.VMEM((TAIL_CR, ST_W), F32), pltpu.VMEM((TAIL_CR, ST_W), F32),
                        pltpu.VMEM((N_SG, TAIL, LANES), F32)],
        compiler_params=pltpu.CompilerParams(
            dimension_semantics=("arbitrary",), vmem_limit_bytes=VMEM_LIMIT),
        name="tail",
    )(*args)


def _post_kernel(xp_ref, xs_ref, yvp_ref, yvs_ref, mixb_ref, xmt_ref, wglu_ref, bglu_ref, ga_ref,
                 wout_ref, gffn_ref, wrt_ref, brt_ref,
                 xmid_ref, ridx_ref, rgate_ref, cnt_ref, base_ref, y_scr, logit_ref, before_ref, *,
                 tail_block):
    i = pl.program_id(0)
    is_sample = i == NB - 1

    @pl.when(i == 0)
    def _():
        base_ref[...] = jnp.zeros_like(base_ref)
        ta = lax.broadcasted_iota(I32, (TB, TB), 0)
        tb = lax.broadcasted_iota(I32, (TB, TB), 1)
        before_ref[...] = jnp.where(ta < tb, 1.0, 0.0).astype(BF16)

    yv = jnp.where(is_sample, yvs_ref[...], yvp_ref[...])
    for t in range(CHUNK):
        for k in range(N_SG):
            y_scr[k, pl.ds(t, CRB, stride=CHUNK), :] = (
                yv[:, t * SSM_W + k * LANES:t * SSM_W + (k + 1) * LANES])
    y = jnp.concatenate([y_scr[k] for k in range(N_SG)], axis=1)
    z = _gelu(y)
    gl = jnp.dot(z.astype(BF16), wglu_ref[...], preferred_element_type=F32) + bglu_ref[...]
    out_a = z * _sigmoid(gl)
    mix = jnp.concatenate([_rms(out_a, ga_ref[...]).astype(BF16), mixb_ref[...]], axis=1)
    x = jnp.where(is_sample, xs_ref[...], xp_ref[...])
    xm = x + jnp.dot(mix, wout_ref[...], preferred_element_type=F32)
    if tail_block is not None:
        xm = jnp.where(i == tail_block, xmt_ref[...], xm)
    _put_token_tiles(xmid_ref, xm)
    hn = _rms(xm, gffn_ref[...])

    def dot_nt(a, b):
        return lax.dot_general(a, b, (((1,), (1,)), ((), ())), preferred_element_type=F32)

    h_hi = hn.astype(BF16)
    w_hi, w_lo = _split(wrt_ref[...])
    logit_ref[...] = dot_nt(w_hi, h_hi) + brt_ref[...]
    if tail_block is not None:
        @pl.when(i == tail_block)
        def _():
            h_lo = (hn - h_hi.astype(F32)).astype(BF16)
            logit_ref[...] += dot_nt(w_hi, h_lo) + dot_nt(w_lo, h_hi)
    logit = logit_ref[...]
    fine = logit[0:N_EXP, :]
    coarse = logit[N_EXP:N_EXP + N_EGRP, :]
    cmax = jnp.max(coarse, axis=0, keepdims=True)
    gi = lax.broadcasted_iota(I32, (N_EGRP, TB), 0).astype(F32)
    grp = jnp.min(jnp.where(coarse == cmax, gi, float(N_EGRP)), axis=0, keepdims=True)
    pg = 1.0 / jnp.sum(jnp.exp(coarse - cmax), axis=0, keepdims=True)
    eidx = lax.broadcasted_iota(I32, (N_EXP, TB), 0)
    ei = eidx.astype(F32)
    egrp = (eidx // EPG).astype(F32)
    neg = jnp.float32(-jnp.inf)
    fm = jnp.where(egrp == grp, fine, neg)
    v0 = jnp.max(fm, axis=0, keepdims=True)
    e0 = jnp.min(jnp.where(fm == v0, ei, float(N_EXP)), axis=0, keepdims=True)
    fm2 = jnp.where(ei == e0, neg, fm)
    v1 = jnp.max(fm2, axis=0, keepdims=True)
    e1 = jnp.min(jnp.where(fm2 == v1, ei, float(N_EXP)), axis=0, keepdims=True)
    tt = jnp.exp(v1 - v0)
    g0 = pg / (1.0 + tt)
    g1 = pg * tt / (1.0 + tt)

    sel0 = ei == e0
    sel1 = ei == e1
    cnt = jnp.where(sel0 | sel1, 1.0, 0.0)
    cum = (jnp.dot(cnt.astype(BF16), before_ref[...], preferred_element_type=F32)
           + base_ref[...])
    rank0 = jnp.sum(jnp.where(sel0, cum, 0.0), axis=0, keepdims=True)
    rank1 = jnp.sum(jnp.where(sel1, cum, 0.0), axis=0, keepdims=True)
    base_new = base_ref[...] + jnp.sum(cnt, axis=1, keepdims=True)
    base_ref[...] = base_new
    cnt_ref[...] = jnp.broadcast_to(base_new, (N_EXP, LANES)).astype(I32)

    zi = jnp.zeros((1, TB), I32)
    ridx_ref[...] = jnp.concatenate(
        [e0.astype(I32), e1.astype(I32), rank0.astype(I32), rank1.astype(I32), zi, zi, zi, zi],
        axis=0)
    zf = jnp.zeros((1, TB), F32)
    rgate_ref[...] = jnp.concatenate([g0, g1, zf, zf, zf, zf, zf, zf], axis=0)


def _post_call(xp, xs, s_block, yvp, yvs, mixb, xmt, tail_block, wglu, bglu, ga, wout, gffn,
               wrt, brt):
    const = lambda i: (0, 0)
    return pl.pallas_call(
        functools.partial(_post_kernel, tail_block=tail_block),
        grid=(NB,),
        in_specs=_stream_specs(s_block) + [
            pl.BlockSpec((CRB, UVW), lambda i: (jnp.minimum(i, NBP - 1), 0)),
            pl.BlockSpec((CRB, UVW), const),
            pl.BlockSpec((TB, CONV_W), lambda i: (i, 0)),
            pl.BlockSpec((TB, D_MODEL), const),
            pl.BlockSpec((SSM_W, SSM_W), const),
            pl.BlockSpec((1, SSM_W), const),
            pl.BlockSpec((1, SSM_W), const),
            pl.BlockSpec((D_MODEL, D_MODEL), const),
            pl.BlockSpec((1, D_MODEL), const),
            pl.BlockSpec((RT_ROWS, D_MODEL), const),
            pl.BlockSpec((RT_ROWS, 1), const),
        ],
        out_specs=[
            pl.BlockSpec((TB * SUBLANES, LANES), lambda i: (i, 0)),
            pl.BlockSpec((SUBLANES, TB), lambda i: (0, i)),
            pl.BlockSpec((SUBLANES, TB), lambda i: (0, i)),
            pl.BlockSpec((N_EXP, LANES), const),
        ],
        out_shape=[
            jax.ShapeDtypeStruct((T_ALL * SUBLANES, LANES), F32),
            jax.ShapeDtypeStruct((SUBLANES, T_ALL), I32),
            jax.ShapeDtypeStruct((SUBLANES, T_ALL), F32),
            jax.ShapeDtypeStruct((N_EXP, LANES), I32),
        ],
        scratch_shapes=[pltpu.VMEM((N_EXP, 1), F32), pltpu.VMEM((N_SG, TB, LANES), F32),
                        pltpu.VMEM((RT_ROWS, TB), F32), pltpu.VMEM((TB, TB), BF16)],
        compiler_params=pltpu.CompilerParams(
            dimension_semantics=("arbitrary",), vmem_limit_bytes=VMEM_LIMIT),
        name="post",
    )(xp, xs, yvp, yvs, mixb, xmt, wglu, bglu, ga, wout, gffn, wrt, brt)


def _invert_kernel(cpos0_ref, cpos1_ref, csrc_ref):
    i = pl.program_id(0)

    def put(r, _):
        tok = i * TB + r
        csrc_ref[cpos0_ref[r]] = tok
        csrc_ref[cpos1_ref[r]] = tok
        return 0

    lax.fori_loop(0, TB, put, 0, unroll=ISSUE_UNROLL)


def _invert_call(cpos0, cpos1):
    blk = pl.BlockSpec((TB,), lambda i: (i,), memory_space=pltpu.SMEM)
    return pl.pallas_call(
        _invert_kernel,
        grid=(NB,),
        in_specs=[blk, blk],
        out_specs=pl.BlockSpec(memory_space=pltpu.SMEM),
        out_shape=jax.ShapeDtypeStruct((N_PAIRS,), I32),
        compiler_params=pltpu.CompilerParams(dimension_semantics=("arbitrary",)),
        name="invert",
    )(cpos0, cpos1)


def _expert_kernel(te_ref, nu_ref, ts_ref, csrc_ref, nexte_ref, xmid_ref, gffn_ref,
                   wg_ref, wu_ref, wd_ref, r_ref,
                   wgb, wub, wdb, hbuf, xbuf, sems, wgf, wuf, wdf, wsems, nexp_ref, *, layer):
    i = pl.program_id(0)
    n_used = nu_ref[0]
    prev = te_ref[jnp.maximum(i - 1, 0)]
    fresh = (i == 0) | (te_ref[i] != prev)

    def weight_copies(e, s):
        return [pltpu.make_async_copy(src.at[layer, e], dst.at[s], wsems.at[s])
                for src, dst in ((wg_ref, wgf), (wu_ref, wuf), (wd_ref, wdf))]

    @pl.when(i == 0)
    def _():
        nexp_ref[0] = 0
        for c in weight_copies(te_ref[0], 0):
            c.start()

    @pl.when(fresh)
    def _():
        s = nexp_ref[0] % 2
        for c in weight_copies(te_ref[i], s):
            c.wait()
        nxt = nexte_ref[te_ref[i]]

        @pl.when(nxt >= 0)
        def _():
            for c in weight_copies(nxt, 1 - s):
                c.start()

        wgb[...] = wgf[s].astype(BF16)
        wub[...] = wuf[s].astype(BF16)
        wdb[...] = wdf[s].astype(BF16)
        nexp_ref[0] = nexp_ref[0] + 1

    def gather_rows(tile, slot):
        base = ts_ref[tile]
        for r in range(TM):
            tok = csrc_ref[jnp.minimum(base + r, N_PAIRS - 1)]
            src = xmid_ref.at[pl.ds(pl.multiple_of(tok * SUBLANES, SUBLANES), SUBLANES)]
            dst = xbuf.at[slot, pl.ds(r * SUBLANES, SUBLANES)]
            pltpu.make_async_copy(src, dst, sems.at[slot]).start(priority=r % N_DMA_THREADS)

    def wait_rows(slot):
        pltpu.make_async_copy(xbuf.at[slot], xbuf.at[slot], sems.at[slot]).wait()

    @pl.when(i == 0)
    def _():
        for a in range(GATHER_AHEAD):
            gather_rows(jnp.minimum(a, n_used - 1), a)

    used = i < n_used
    slot = i % (GATHER_AHEAD + 1)

    @pl.when(used)
    def _():
        wait_rows(slot)
        hbuf[...] = _rms(_get_token_tiles(xbuf.at[slot], TM), gffn_ref[...]).astype(BF16)
        gather_rows(jnp.minimum(i + GATHER_AHEAD, n_used - 1),
                    (i + GATHER_AHEAD) % (GATHER_AHEAD + 1))
        hb = hbuf[...]
        g = jnp.dot(hb, wgb[...], preferred_element_type=F32)
        u = jnp.dot(hb, wub[...], preferred_element_type=F32)
        hid = (g * _sigmoid(g) * u).astype(BF16)
        _put_token_tiles(r_ref, jnp.dot(hid, wdb[...], preferred_element_type=F32))

    @pl.when(i == n_used - 1)
    def _():
        for a in range(1, GATHER_AHEAD + 1):
            wait_rows((i + a) % (GATHER_AHEAD + 1))

    @pl.when(jnp.logical_not(used))
    def _():
        r_ref[...] = jnp.zeros_like(r_ref)


def _expert_call(layer, te, nu, ts, csrc, nexte, xmid, gffn, w_gate, w_up, w_down):
    return pl.pallas_call(
        functools.partial(_expert_kernel, layer=layer),
        grid_spec=pltpu.PrefetchScalarGridSpec(
            num_scalar_prefetch=5,
            grid=(NT,),
            in_specs=[
                pl.BlockSpec(memory_space=pl.ANY),
                pl.BlockSpec((1, D_MODEL), lambda i, *_: (0, 0)),
                pl.BlockSpec(memory_space=pl.ANY),
                pl.BlockSpec(memory_space=pl.ANY),
                pl.BlockSpec(memory_space=pl.ANY),
            ],
            out_specs=pl.BlockSpec((TM * SUBLANES, LANES), lambda i, *_: (i, 0)),
            scratch_shapes=[pltpu.VMEM((D_MODEL, D_FF), BF16),
                            pltpu.VMEM((D_MODEL, D_FF), BF16),
                            pltpu.VMEM((D_FF, D_MODEL), BF16),
                            pltpu.VMEM((TM, D_MODEL), BF16),
                            pltpu.VMEM((GATHER_AHEAD + 1, TM * SUBLANES, LANES), F32),
                            pltpu.SemaphoreType.DMA((GATHER_AHEAD + 1,)),
                            pltpu.VMEM((2, D_MODEL, D_FF), F32),
                            pltpu.VMEM((2, D_MODEL, D_FF), F32),
                            pltpu.VMEM((2, D_FF, D_MODEL), F32),
                            pltpu.SemaphoreType.DMA((2,)),
                            pltpu.SMEM((1,), I32)],
        ),
        out_shape=jax.ShapeDtypeStruct((P_CAP * SUBLANES, LANES), F32),
        compiler_params=pltpu.CompilerParams(
            dimension_semantics=("arbitrary",), vmem_limit_bytes=VMEM_LIMIT),
        name="experts",
    )(te, nu, ts, csrc, nexte, xmid, gffn, w_gate, w_up, w_down)


def _combine_rows(pos0_ref, pos1_ref, npos0_ref, npos1_ref, xmid_ref, gate_ref, r_ref, buf, sems):
    i = pl.program_id(0)
    slot = i % 2

    @pl.when(i == 0)
    def _():
        _issue_row_gather(pos0_ref, pos1_ref, r_ref, lambda k: buf.at[k], sems.at[0])

    for k in range(2):
        pltpu.make_async_copy(buf.at[k], buf.at[k], sems.at[slot]).wait()

    @pl.when(i < NB - 1)
    def _():
        other = 1 - slot
        _issue_row_gather(npos0_ref, npos1_ref, r_ref, lambda k: buf.at[2 * other + k],
                          sems.at[other])

    gt = gate_ref[...]
    return (_get_token_tiles(xmid_ref, TB)
            + gt[:, 0:1] * _get_token_tiles(buf.at[2 * slot], TB)
            + gt[:, 1:2] * _get_token_tiles(buf.at[2 * slot + 1], TB))


def _combine_specs():
    cur = pl.BlockSpec((TB,), lambda i: (i,), memory_space=pltpu.SMEM)
    nxt = pl.BlockSpec((TB,), lambda i: (jnp.minimum(i + 1, NB - 1),), memory_space=pltpu.SMEM)
    in_specs = [cur, cur, nxt, nxt,
                pl.BlockSpec((TB * SUBLANES, LANES), lambda i: (i, 0)),
                pl.BlockSpec((TB, 2), lambda i: (i, 0)),
                pl.BlockSpec(memory_space=pl.ANY)]
    scratch = [pltpu.VMEM((4, TB * SUBLANES, LANES), F32), pltpu.SemaphoreType.DMA((2,))]
    return in_specs, scratch


def _combine_final_kernel(pos0_ref, pos1_ref, npos0_ref, npos1_ref, xmid_ref, gate_ref, r_ref,
                          gfin_ref, outp_ref, outs_ref, buf, sems):
    i = pl.program_id(0)
    out = _rms(_combine_rows(pos0_ref, pos1_ref, npos0_ref, npos1_ref, xmid_ref, gate_ref, r_ref,
                             buf, sems), gfin_ref[...])

    @pl.when(i < NB - 1)
    def _():
        outp_ref[...] = out

    @pl.when(i == NB - 1)
    def _():
        outs_ref[...] = out


def _combine_final_call(pos0, pos1, xmid, gate_t, r, gfin):
    in_specs, scratch = _combine_specs()
    return pl.pallas_call(
        _combine_final_kernel,
        grid=(NB,),
        in_specs=in_specs + [pl.BlockSpec((1, D_MODEL), lambda i: (0, 0))],
        out_specs=[pl.BlockSpec((TB, D_MODEL), lambda i: (jnp.minimum(i, NBP - 1), 0)),
                   pl.BlockSpec((TB, D_MODEL), lambda i: (0, 0))],
        out_shape=[jax.ShapeDtypeStruct((T_PROMPT, D_MODEL), F32),
                   jax.ShapeDtypeStruct((T_SAMPLE, D_MODEL), F32)],
        scratch_shapes=scratch,
        compiler_params=pltpu.CompilerParams(
            dimension_semantics=("arbitrary",), vmem_limit_bytes=VMEM_LIMIT),
        name="combine_final",
    )(pos0, pos1, pos0, pos1, xmid, gate_t, r, gfin)


def _ssm_compact(lam_re, lam_im, log_dt, b_re, b_im, c_re, c_im, d_skip):
    nl = lam_re.shape[0]
    dt = jnp.exp(log_dt)[..., None]
    kpow = jnp.arange(CHUNK + 1, dtype=F32).reshape(CHUNK + 1, 1, 1, 1)
    mag = jnp.exp(kpow * (lam_re * dt))
    pw_re = mag * jnp.cos(kpow * (lam_im * dt))
    pw_im = mag * jnp.sin(kpow * (lam_im * dt))
    ab_re, ab_im = pw_re[1], pw_im[1]
    denom = lam_re * lam_re + lam_im * lam_im
    num_re = ab_re - 1.0
    f_re = (num_re * lam_re + ab_im * lam_im) / denom
    f_im = (ab_im * lam_re - num_re * lam_im) / denom
    bb_re = f_re[..., None] * b_re - f_im[..., None] * b_im
    bb_im = f_re[..., None] * b_im + f_im[..., None] * b_re

    krev = (CHUNK - 1.0) - kpow[:CHUNK]
    rmag = jnp.exp(krev * (lam_re * dt))
    rev_re = (rmag * jnp.cos(krev * (lam_im * dt)))[..., None]
    rev_im = (rmag * jnp.sin(krev * (lam_im * dt)))[..., None]
    ab = jnp.stack([rev_re * bb_re - rev_im * bb_im, rev_re * bb_im + rev_im * bb_re])
    ab = ab.reshape(2, CHUNK, nl, N_SG, GPS, N_ST, G_CH)
    ab = ab.transpose(2, 3, 1, 0, 4, 6, 5).reshape(nl, N_SG, CHUNK, 2, LANES, N_ST)
    abd = jnp.concatenate([ab, ab], axis=-1)

    pr = pw_re[:, :, :, None, :]
    pi = pw_im[:, :, :, None, :]
    ca_re = c_re * pr - c_im * pi
    ca_im = c_re * pi + c_im * pr
    ca = jnp.stack([ca_re[1:], -ca_im[1:]])
    ca = ca.reshape(2, CHUNK, nl, N_SG, GPS, G_CH, N_ST)
    cac = ca.transpose(2, 3, 1, 0, 6, 4, 5).reshape(nl, N_SG, CHUNK, 2, N_ST, LANES)

    bq_re = jnp.swapaxes(bb_re, -1, -2)[:, :, None]
    bq_im = jnp.swapaxes(bb_im, -1, -2)[:, :, None]
    kk = jnp.sum(ca_re[:CHUNK, :, :, :, None, :] * bq_re
                 - ca_im[:CHUNK, :, :, :, None, :] * bq_im, axis=-1)
    skip = d_skip.reshape(nl, N_G, G_CH)[..., None] * jnp.eye(G_CH, dtype=F32)
    kk = jnp.concatenate([kk[:1] + skip[None], kk[1:]], axis=0)
    kk = kk.reshape(CHUNK, nl, N_SG, GPS, G_CH, G_CH)
    kk = kk.transpose(1, 2, 0, 5, 3, 4)
    kc = kk.reshape(nl, N_SG, CHUNK, G_CH, LANES)

    al_re = pw_re[CHUNK].reshape(nl, 1, ST_W)
    al_im = pw_im[CHUNK].reshape(nl, 1, ST_W)
    return abd, cac, kc, al_re, al_im


def _tile_tables(counts):
    padded = ((counts + TM - 1) // TM) * TM
    ends = jnp.cumsum(padded)
    offs = ends - padded
    end_tile = ends // TM
    n_used = end_tile[-1].astype(I32)
    tile = jnp.arange(NT, dtype=I32)
    live = jnp.minimum(tile, n_used - 1)
    te = jnp.sum(end_tile[None, :] <= live[:, None], axis=1).astype(I32)
    cstart = jnp.cumsum(counts) - counts
    mine = te[:, None] == jnp.arange(N_EXP, dtype=I32)[None, :]
    first_row = jnp.sum(jnp.where(mine, (cstart - offs)[None, :], 0), axis=1)
    ts = (first_row + live * TM).astype(I32)
    ex = jnp.arange(N_EXP, dtype=I32)
    later = (ex[None, :] > ex[:, None]) & (counts[None, :] > 0)
    nexte = jnp.min(jnp.where(later, ex[None, :], N_EXP), axis=1)
    nexte = jnp.where(nexte == N_EXP, -1, nexte).astype(I32)
    return offs.astype(I32), cstart.astype(I32), te, n_used.reshape(1), ts, nexte


def _sorted_positions(ridx, offs, cstart):
    experts = jnp.arange(N_EXP, dtype=I32)[:, None]
    out = []
    for k in range(2):
        hit = ridx[k][None, :] == experts
        for table in (offs, cstart):
            out.append(jnp.sum(jnp.where(hit, table[:, None], 0), axis=0) + ridx[2 + k])
    pos0, cpos0, pos1, cpos1 = out
    return pos0, pos1, cpos0, cpos1


def kernel(x_prompt, x_sample, state_ssm_re, state_ssm_im, cache_conv, norm_mix, w_in, lam_re,
           lam_im, log_dt, ssm_b_re, ssm_b_im, ssm_c_re, ssm_c_im, ssm_d, w_glu, b_glu, w_conv,
           norm_a, norm_b, w_out, norm_ffn, w_router_group, b_router_group, w_router_expert,
           b_router_expert, w_gate, w_up, w_down, norm_final):
    xp = x_prompt.reshape(T_PROMPT, D_MODEL)
    xs = x_sample.reshape(T_SAMPLE, D_MODEL)
    s_block = 0
    gfin = norm_final.reshape(1, D_MODEL)
    abd, cac, kc, al_re, al_im = _ssm_compact(
        lam_re, lam_im, log_dt, ssm_b_re, ssm_b_im, ssm_c_re, ssm_c_im, ssm_d)
    p_re, p_im, p_buf, s_re, s_im, s_buf = [], [], [], [], [], []
    moe = None
    for l in range(DEPTH):
        cache = cache_conv[l]
        cz1 = jnp.pad(cache[:, 1:2], ((0, 0), (0, LEN_S - 1), (0, 0))).reshape(T_SAMPLE, CONV_W)
        cz2 = jnp.pad(cache, ((0, 0), (0, LEN_S - 2), (0, 0))).reshape(T_SAMPLE, CONV_W)
        inproj_params = (norm_mix[l].reshape(1, D_MODEL), w_in[l].astype(BF16), w_conv[l],
                         norm_b[l].reshape(1, CONV_W), cz1, cz2)
        if moe is None:
            uvp, uvs, mixb, ptail, zs = _inproj_call(xp, xs, s_block, *inproj_params)
        else:
            x, uvp, uvs, mixb, ptail, zs = _combine_inproj_call(*moe, *inproj_params)
            xp = xs = x
            s_block = NB - 1

        yvp, yvs, pst_r, pst_i, sst_r, sst_i, tst_r, tst_i = _ssm_call(
            uvp, uvs, abd[l].astype(BF16), cac[l].astype(BF16), kc[l].astype(BF16),
            al_re[l], al_im[l],
            state_ssm_re[l].reshape(N_SEQ_S, ST_W), state_ssm_im[l].reshape(N_SEQ_S, ST_W))

        tail_block = NBP - 1 if l < DEPTH - 1 else None
        if tail_block is not None:
            xmt = _tail_call(
                xp[T_PROMPT - TAIL_X:T_PROMPT], norm_mix[l].reshape(1, D_MODEL), w_in[l],
                w_conv[l], norm_b[l].reshape(1, CONV_W), abd[l], cac[l], kc[l],
                al_re[l], al_im[l], tst_r, tst_i, w_glu[l], b_glu[l].reshape(1, SSM_W),
                norm_a[l].reshape(1, SSM_W), w_out[l])

        wrt = jnp.concatenate([
            w_router_expert[l].transpose(0, 2, 1).reshape(N_EXP, D_MODEL),
            w_router_group[l].T,
            jnp.zeros((RT_ROWS - N_EXP - N_EGRP, D_MODEL), F32)], axis=0)
        brt = jnp.concatenate([b_router_expert[l].reshape(N_EXP), b_router_group[l],
                               jnp.zeros((RT_ROWS - N_EXP - N_EGRP,), F32)]).reshape(RT_ROWS, 1)
        gffn = norm_ffn[l].reshape(1, D_MODEL)
        xmid, ridx, rgate, cnt = _post_call(
            xp, xs, s_block, yvp, yvs, mixb, xmt, tail_block,
            w_glu[l].astype(BF16), b_glu[l].reshape(1, SSM_W),
            norm_a[l].reshape(1, SSM_W), w_out[l].astype(BF16), gffn, wrt, brt)

        offs, cstart, te, n_used, ts, nexte = _tile_tables(cnt[:, 0])
        pos0, pos1, cpos0, cpos1 = _sorted_positions(ridx, offs, cstart)
        csrc = _invert_call(cpos0, cpos1)
        r = _expert_call(l, te, n_used, ts, csrc, nexte, xmid, gffn, w_gate, w_up, w_down)
        moe = (pos0, pos1, xmid, rgate[:2].T, r)

        p_re.append(pst_r.reshape(1, N_G, N_ST))
        p_im.append(pst_i.reshape(1, N_G, N_ST))
        p_buf.append(ptail[SUBLANES - 2:].reshape(1, 2, CONV_W))
        s_re.append(sst_r.reshape(N_SEQ_S, N_G, N_ST))
        s_im.append(sst_i.reshape(N_SEQ_S, N_G, N_ST))
        s_buf.append(zs.reshape(N_SEQ_S, LEN_S, CONV_W)[:, LEN_S - 2:])

    y_prompt, y_sample = _combine_final_call(*moe, gfin)
    return (y_prompt.reshape(1, T_PROMPT, D_MODEL), y_sample.reshape(N_SEQ_S, LEN_S, D_MODEL),
            jnp.stack(p_re), jnp.stack(p_im), jnp.stack(p_buf),
            jnp.stack(s_re), jnp.stack(s_im), jnp.stack(s_buf))
```
